```python
import jax, jax.numpy as jnp
from jax import lax
import numpy as np

D_MODEL = 4096
BATCH = 2
SEQ = 4096
DEPTH = 4

N_MIXERS = 3
BLOCK = 128
ROPE_THETA = 500000.0
ROPE_FRACTION = 4
NORM_EPS = 1e-6
LN_EPS = 1e-6
NEG_INF = -1e30

A_HEADS = 32
A_KV_HEADS = 8
A_HEAD_DIM = D_MODEL // A_HEADS
IDX_HEADS = 16
IDX_HEAD_DIM = 128
INDEX_TOPK = 256
A_Q = A_HEADS * A_HEAD_DIM
A_KV = A_KV_HEADS * A_HEAD_DIM
A_IQ = IDX_HEADS * IDX_HEAD_DIM
A_IN = A_Q + 2 * A_KV + A_IQ + IDX_HEAD_DIM + IDX_HEADS

B_HEADS = 32
B_HEAD_DIM = D_MODEL // B_HEADS
B_IN = 3 * D_MODEL + B_HEADS

C_HEADS = 64
C_KV_HEADS = 8
C_HEAD_DIM = D_MODEL // C_HEADS
WINDOW = 128
C_Q = C_HEADS * C_HEAD_DIM
C_KV = C_KV_HEADS * C_HEAD_DIM
C_IN = C_Q + 2 * C_KV

D_FF = 11008
CONV_WIDTH = 3

N_A = (DEPTH + 2) // 3
N_B = (DEPTH + 1) // 3
N_C = DEPTH // 3

kernel_name = 'hybrid_dsa_fox_swa_convffn'


def _split(t, sizes):
    return jnp.split(t, np.cumsum(sizes)[:-1].tolist(), axis=-1)


def rmsnorm(x, g):
    xf = x.astype(jnp.float32)
    y = xf * lax.rsqrt(jnp.mean(xf * xf, axis=-1, keepdims=True) + NORM_EPS)
    return (y * g.astype(jnp.float32)).astype(x.dtype)


def layer_norm(x, g, b):
    xf = x.astype(jnp.float32)
    mu = jnp.mean(xf, axis=-1, keepdims=True)
    var = jnp.mean(jnp.square(xf - mu), axis=-1, keepdims=True)
    y = (xf - mu) * lax.rsqrt(var + LN_EPS)
    return (y * g.astype(jnp.float32) + b.astype(jnp.float32)).astype(x.dtype)


def partial_rope(x, rot_dim):
    seq = x.shape[1]
    half = rot_dim // 2
    inv_freq = jnp.power(jnp.float32(ROPE_THETA), -jnp.arange(half, dtype=jnp.float32) * (2.0 / rot_dim))
    ang = jnp.arange(seq, dtype=jnp.float32)[:, None] * inv_freq[None, :]
    cos = jnp.cos(ang)[None, :, None, :]
    sin = jnp.sin(ang)[None, :, None, :]
    xf = x.astype(jnp.float32)
    x1 = xf[..., :half]
    x2 = xf[..., half:rot_dim]
    out = jnp.concatenate([x1 * cos - x2 * sin, x2 * cos + x1 * sin, xf[..., rot_dim:]], axis=-1)
    return out.astype(x.dtype)


def to_blocks(t):
    bsz, seq = t.shape[0], t.shape[1]
    return t.reshape((bsz, seq // BLOCK, BLOCK) + t.shape[2:]).swapaxes(0, 1)


def from_blocks(t):
    nb, bsz = t.shape[0], t.shape[1]
    return t.swapaxes(0, 1).reshape((bsz, nb * BLOCK) + t.shape[3:])


def dsa_mixer(h, w_in, ik_g, ik_b, w_out):
    bsz, seq, _ = h.shape
    nb = seq // BLOCK
    groups = A_HEADS // A_KV_HEADS
    q, k, v, iq, ik, iw = _split(h @ w_in, [A_Q, A_KV, A_KV, A_IQ, IDX_HEAD_DIM, IDX_HEADS])
    rot = A_HEAD_DIM // ROPE_FRACTION
    q = partial_rope(q.reshape(bsz, seq, A_HEADS, A_HEAD_DIM), rot)
    k = partial_rope(k.reshape(bsz, seq, A_KV_HEADS, A_HEAD_DIM), rot)
    v = v.reshape(bsz, seq, A_KV_HEADS, A_HEAD_DIM)
    kv = jnp.stack([k, v], axis=2)
    irot = IDX_HEAD_DIM // ROPE_FRACTION
    iq = partial_rope(iq.reshape(bsz, seq, IDX_HEADS, IDX_HEAD_DIM), irot)
    ik = partial_rope(layer_norm(ik, ik_g, ik_b)[:, :, None, :], irot)[:, :, 0]
    iw = iw.astype(jnp.float32) * (IDX_HEADS ** -0.5 * IDX_HEAD_DIM ** -0.5)
    n_sel = min(INDEX_TOPK, seq // 4)
    key_pos = jnp.arange(seq)
    scale = A_HEAD_DIM ** -0.5

    def block(args):
        qb, iqb, iwb, t0 = args
        q_pos = t0 + jnp.arange(BLOCK)
        causal = key_pos[None, :] <= q_pos[:, None]
        rel = jnp.einsum('bqhd,bkd->bqhk', iqb, ik, preferred_element_type=jnp.float32)
        score = jnp.einsum('bqhk,bqh->bqk', jax.nn.relu(rel), iwb)
        score = jnp.where(causal[None], score, -jnp.inf)
        _, idx = lax.top_k(score, n_sel)
        valid = idx <= q_pos[None, :, None]
        kv_sel = jax.vmap(lambda kv_b, idx_b: kv_b[idx_b])(kv, idx)
        qg = qb.reshape(bsz, BLOCK, A_KV_HEADS, groups, A_HEAD_DIM)
        logits = jnp.einsum('bqgrd,bqkgd->bqgrk', qg, kv_sel[:, :, :, 0],
                            preferred_element_type=jnp.float32) * scale
        logits = jnp.where(valid[:, :, None, None, :], logits, NEG_INF)
        p = jax.nn.softmax(logits, axis=-1).astype(v.dtype)
        o = jnp.einsum('bqgrk,bqkgd->bqgrd', p, kv_sel[:, :, :, 1])
        return o.reshape(bsz, BLOCK, A_Q)

    starts = jnp.arange(nb, dtype=jnp.int32) * BLOCK
    o = lax.map(block, (to_blocks(q), to_blocks(iq), to_blocks(iw), starts))
    return from_blocks(o) @ w_out


def fox_mixer(h, w_in, f_bias, w_out):
    bsz, seq, _ = h.shape
    nb = seq // BLOCK
    q, k, v, f = _split(h @ w_in, [D_MODEL, D_MODEL, D_MODEL, B_HEADS])
    q = q.reshape(bsz, seq, B_HEADS, B_HEAD_DIM)
    k = k.reshape(bsz, seq, B_HEADS, B_HEAD_DIM)
    v = v.reshape(bsz, seq, B_HEADS, B_HEAD_DIM)
    log_f = jax.nn.log_sigmoid(f.astype(jnp.float32) + f_bias.astype(jnp.float32))
    cum = jnp.cumsum(log_f, axis=1)
    cum_k = cum.transpose(0, 2, 1)
    key_pos = jnp.arange(seq)
    scale = B_HEAD_DIM ** -0.5

    def block(args):
        qb, cq, t0 = args
        q_pos = t0 + jnp.arange(BLOCK)
        causal = key_pos[None, :] <= q_pos[:, None]
        logits = jnp.einsum('bqhd,bkhd->bhqk', qb, k, preferred_element_type=jnp.float32) * scale
        logits = logits + cq.transpose(0, 2, 1)[..., None] - cum_k[:, :, None, :]
        logits = jnp.where(causal[None, None], logits, NEG_INF)
        p = jax.nn.softmax(logits, axis=-1).astype(v.dtype)
        o = jnp.einsum('bhqk,bkhd->bqhd', p, v)
        return o.reshape(bsz, BLOCK, D_MODEL)

    starts = jnp.arange(nb, dtype=jnp.int32) * BLOCK
    o = lax.map(block, (to_blocks(q), to_blocks(cum), starts))
    return from_blocks(o) @ w_out


def swa_sink_mixer(h, w_in, sinks, w_out):
    bsz, seq, _ = h.shape
    nb = seq // WINDOW
    groups = C_HEADS // C_KV_HEADS
    q, k, v = _split(h @ w_in, [C_Q, C_KV, C_KV])
    rot = C_HEAD_DIM // ROPE_FRACTION
    q = partial_rope(q.reshape(bsz, seq, C_HEADS, C_HEAD_DIM), rot)
    q = q.reshape(bsz, nb, WINDOW, C_KV_HEADS, groups, C_HEAD_DIM)
    k = partial_rope(k.reshape(bsz, seq, C_KV_HEADS, C_HEAD_DIM), rot)
    k = k.reshape(bsz, nb, WINDOW, C_KV_HEADS, C_HEAD_DIM)
    v = v.reshape(bsz, nb, WINDOW, C_KV_HEADS, C_HEAD_DIM)

    def with_prev(t):
        prev = jnp.pad(t, ((0, 0), (1, 0), (0, 0), (0, 0), (0, 0)))[:, :-1]
        return jnp.concatenate([prev, t], axis=2)

    kk, vv = with_prev(k), with_prev(v)
    logits = jnp.einsum('bnqgrd,bnkgd->bngrqk', q, kk,
                        preferred_element_type=jnp.float32) * (C_HEAD_DIM ** -0.5)
    i = jnp.arange(WINDOW)[:, None]
    j = jnp.arange(2 * WINDOW)[None, :]
    n = jnp.arange(nb)[:, None, None]
    valid = (j > i) & (j <= i + WINDOW) & (n * WINDOW + j >= WINDOW)
    logits = jnp.where(valid[None, :, None, None], logits, NEG_INF)
    sink = jnp.broadcast_to(sinks.astype(jnp.float32).reshape(C_KV_HEADS, groups)[None, None, :, :, None, None],
                            logits.shape[:-1] + (1,))
    p = jax.nn.softmax(jnp.concatenate([logits, sink], axis=-1), axis=-1)[..., :-1].astype(v.dtype)
    o = jnp.einsum('bngrqk,bnkgd->bnqgrd', p, vv).reshape(bsz, seq, C_Q)
    return o @ w_out


def conv_glu_ffn(h, w_up, conv_w, conv_b, w_down):
    u = h @ w_up
    seq = u.shape[1]
    u_pad = jnp.pad(u, ((0, 0), (CONV_WIDTH - 1, 0), (0, 0)))
    u = conv_b + sum(u_pad[:, tap:tap + seq] * conv_w[tap] for tap in range(CONV_WIDTH))
    gate, val = jnp.split(u, 2, axis=-1)
    return (jax.nn.silu(gate) * val) @ w_down


def setup_inputs(seed: int = 0) -> dict:
    key = jax.random.key(seed)
    ks = jax.random.split(key, 20)
    f32 = jnp.float32
    out_scale = (2 * DEPTH) ** -0.5

    def nrm(k, shape, scale):
        return jax.random.normal(k, shape, f32) * scale

    conv_center = (jnp.arange(CONV_WIDTH) == CONV_WIDTH - 1).astype(f32)[None, :, None]
    return {
        'x': nrm(ks[0], (BATCH, SEQ, D_MODEL), 1.0),
        'attn_norm': 1.0 + nrm(ks[1], (DEPTH, D_MODEL), 0.1),
        'ffn_norm': 1.0 + nrm(ks[2], (DEPTH, D_MODEL), 0.1),
        'final_norm': 1.0 + nrm(ks[3], (D_MODEL,), 0.1),
        'a_w_in': nrm(ks[4], (N_A, D_MODEL, A_IN), D_MODEL ** -0.5),
        'a_idx_k_norm_g': 1.0 + nrm(ks[5], (N_A, IDX_HEAD_DIM), 0.1),
        'a_idx_k_norm_b': nrm(ks[6], (N_A, IDX_HEAD_DIM), 0.02),
        'a_w_out': nrm(ks[7], (N_A, A_Q, D_MODEL), A_Q ** -0.5 * out_scale),
        'b_w_in': nrm(ks[8], (N_B, D_MODEL, B_IN), D_MODEL ** -0.5),
        'b_f_bias': 2.0 + nrm(ks[9], (N_B, B_HEADS), 0.5),
        'b_w_out': nrm(ks[10], (N_B, D_MODEL, D_MODEL), D_MODEL ** -0.5 * out_scale),
        'c_w_in': nrm(ks[11], (N_C, D_MODEL, C_IN), D_MODEL ** -0.5),
        'c_sinks': nrm(ks[12], (N_C, C_HEADS), 0.5),
        'c_w_out': nrm(ks[13], (N_C, C_Q, D_MODEL), C_Q ** -0.5 * out_scale),
        'ffn_w_up': nrm(ks[14], (DEPTH, D_MODEL, 2 * D_FF), D_MODEL ** -0.5),
        'ffn_conv_w': conv_center + nrm(ks[15], (DEPTH, CONV_WIDTH, 2 * D_FF), 0.3),
        'ffn_conv_b': nrm(ks[16], (DEPTH, 2 * D_FF), 0.02),
        'ffn_w_down': nrm(ks[17], (DEPTH, D_FF, D_MODEL), D_FF ** -0.5 * out_scale),
    }


def reference(x, attn_norm, ffn_norm, final_norm, a_w_in, a_idx_k_norm_g, a_idx_k_norm_b, a_w_out,
              b_w_in, b_f_bias, b_w_out, c_w_in, c_sinks, c_w_out,
              ffn_w_up, ffn_conv_w, ffn_conv_b, ffn_w_down):
    for layer in range(DEPTH):
        mixer, slot = layer % N_MIXERS, layer // N_MIXERS
        h = rmsnorm(x, attn_norm[layer])
        if mixer == 0:
            x = x + dsa_mixer(h, a_w_in[slot], a_idx_k_norm_g[slot], a_idx_k_norm_b[slot], a_w_out[slot])
        elif mixer == 1:
            x = x + fox_mixer(h, b_w_in[slot], b_f_bias[slot], b_w_out[slot])
        else:
            x = x + swa_sink_mixer(h, c_w_in[slot], c_sinks[slot], c_w_out[slot])
        h = rmsnorm(x, ffn_norm[layer])
        x = x + conv_glu_ffn(h, ffn_w_up[layer], ffn_conv_w[layer], ffn_conv_b[layer], ffn_w_down[layer])
    return rmsnorm(x, final_norm)
```

```python
import functools
import math

import jax
import jax.numpy as jnp
import numpy as np
from jax import lax
from jax.experimental import pallas as pl
from jax.experimental.pallas import tpu as pltpu

F32 = jnp.float32
BF16 = jnp.bfloat16

N_MIXERS = 3
ROPE_THETA = 500000.0
ROPE_FRACTION = 4
NORM_EPS = 1e-6
LN_EPS = 1e-6
NEG_INF = -1e30
HEAD = 128
A_HEADS, A_KV_HEADS, IDX_HEADS, INDEX_TOPK = 32, 8, 16, 256
B_HEADS = 32
C_HEADS, C_KV_HEADS, C_HEAD_DIM, WINDOW = 64, 8, 64, 128
CONV_WIDTH = 3

LANES = 128
SUBLANES = 8
VMEM_LIMIT_BYTES = 56 * 1024 * 1024
INT_MIN = -(2 ** 31)


def _params(*sem):
    return pltpu.CompilerParams(dimension_semantics=sem, vmem_limit_bytes=VMEM_LIMIT_BYTES)


def _tile(dim, target, quantum=LANES):
    if dim <= target:
        return dim
    best = None
    for t in range(quantum, target + 1, quantum):
        if dim % t == 0:
            best = t
    assert best is not None, (dim, target)
    return best


def _rmsnorm_kernel(x_ref, g_ref, o_ref):
    x = x_ref[...]
    y = x * lax.rsqrt(jnp.mean(x * x, axis=-1, keepdims=True) + NORM_EPS)
    o_ref[...] = (y * g_ref[...]).astype(o_ref.dtype)


def _rmsnorm(x, g, out_dtype):
    m, d = x.shape
    tm = _tile(m, 256, SUBLANES)
    return pl.pallas_call(
        _rmsnorm_kernel,
        grid=(m // tm,),
        in_specs=[pl.BlockSpec((tm, d), lambda i: (i, 0)), pl.BlockSpec((1, d), lambda i: (0, 0))],
        out_specs=pl.BlockSpec((tm, d), lambda i: (i, 0)),
        out_shape=jax.ShapeDtypeStruct((m, d), out_dtype),
        compiler_params=_params("parallel"),
        name="rmsnorm",
    )(x, g.reshape(1, d))


def _mm_kernel(*refs, nk, has_res):
    if has_res:
        a_ref, w_ref, r_ref, o_ref = refs[:4]
        scratch = refs[4:]
    else:
        a_ref, w_ref, o_ref = refs[:3]
        r_ref = None
        scratch = refs[3:]
    part = jnp.dot(a_ref[...], w_ref[...], preferred_element_type=F32)
    if nk == 1:
        if has_res:
            part = part + r_ref[...]
        o_ref[...] = part.astype(o_ref.dtype)
        return
    acc_ref = scratch[0]
    k = pl.program_id(2)

    @pl.when(k == 0)
    def _():
        acc_ref[...] = part

    @pl.when(k > 0)
    def _():
        acc_ref[...] += part

    @pl.when(k == nk - 1)
    def _():
        out = acc_ref[...]
        if has_res:
            out = out + r_ref[...]
        o_ref[...] = out.astype(o_ref.dtype)


def _matmul(a, w, *, res=None, out_dtype, tm=1024, tn=512, tk=None):
    m, kdim = a.shape
    n = w.shape[1]
    tm = _tile(m, tm, SUBLANES)
    tn = _tile(n, tn)
    tk = kdim if tk is None else _tile(kdim, tk)
    nk = kdim // tk
    in_specs = [pl.BlockSpec((tm, tk), lambda i, j, k: (i, k)), pl.BlockSpec((tk, tn), lambda i, j, k: (k, j))]
    args = [a, w]
    if res is not None:
        in_specs.append(pl.BlockSpec((tm, tn), lambda i, j, k: (i, j)))
        args.append(res)
    return pl.pallas_call(
        functools.partial(_mm_kernel, nk=nk, has_res=res is not None),
        grid=(m // tm, n // tn, nk),
        in_specs=in_specs,
        out_specs=pl.BlockSpec((tm, tn), lambda i, j, k: (i, j)),
        out_shape=jax.ShapeDtypeStruct((m, n), out_dtype),
        scratch_shapes=[pltpu.VMEM((tm, tn), F32)] if nk > 1 else [],
        compiler_params=_params("parallel", "parallel", "arbitrary"),
        name="matmul",
    )(*args)


def _rope_tables(seq, head_dim, lanes_per_head):
    rot = head_dim // ROPE_FRACTION
    half = rot // 2
    inv_freq = jnp.power(jnp.float32(ROPE_THETA), -jnp.arange(half, dtype=F32) * (2.0 / rot))
    ang = jnp.arange(seq, dtype=F32)[:, None] * inv_freq[None, :]
    cos, sin = jnp.cos(ang), jnp.sin(ang)
    ones = jnp.ones((seq, lanes_per_head - rot), F32)
    zeros_h = jnp.zeros((seq, half), F32)
    zeros_r = jnp.zeros((seq, lanes_per_head - rot), F32)
    c = jnp.concatenate([cos, cos, ones], axis=1)
    a = jnp.concatenate([-sin, zeros_h, zeros_r], axis=1)
    b = jnp.concatenate([zeros_h, sin, zeros_r], axis=1)
    reps = LANES // lanes_per_head
    return tuple(jnp.tile(t, (1, reps)) for t in (c, a, b)), half


def _rope_apply(x, c, a, b, half):
    up = pltpu.roll(x, LANES - half, axis=1)
    dn = pltpu.roll(x, half, axis=1)
    return x * c + up * a + dn * b


def _rope_kernel(x_ref, c_ref, a_ref, b_ref, o_ref, *, half, n_scaled, scale):
    j = pl.program_id(1)
    s = jnp.where(j < n_scaled, jnp.float32(scale), jnp.float32(1.0))
    c, a, b = c_ref[...], a_ref[...], b_ref[...]
    for blk in range(x_ref.shape[1] // LANES):
        sl = slice(blk * LANES, (blk + 1) * LANES)
        x = x_ref[:, sl].astype(F32)
        o_ref[:, sl] = (_rope_apply(x, c, a, b, half) * s).astype(o_ref.dtype)


def _rope(p, seq, tables, half, *, n_cols, n_scaled_cols, scale, tc=512):
    m = p.shape[0]
    tm = _tile(seq, 512, SUBLANES)
    tc = _tile(math.gcd(n_cols, n_scaled_cols) if n_scaled_cols else n_cols, tc)
    sb = seq // tm
    tab_spec = pl.BlockSpec((tm, LANES), lambda i, j: (i % sb, 0))
    return pl.pallas_call(
        functools.partial(_rope_kernel, half=half, n_scaled=n_scaled_cols // tc, scale=scale),
        grid=(m // tm, n_cols // tc),
        in_specs=[pl.BlockSpec((tm, tc), lambda i, j: (i, j)), tab_spec, tab_spec, tab_spec],
        out_specs=pl.BlockSpec((tm, tc), lambda i, j: (i, j)),
        out_shape=jax.ShapeDtypeStruct((m, n_cols), p.dtype),
        compiler_params=_params("parallel", "parallel"),
        name="rope",
    )(p, *tables)


def _idx_prep_kernel(p_ref, g_ref, beta_ref, c_ref, a_ref, b_ref, ik_ref, iw_ref, *, half, iw_scale):
    x = p_ref[:, :HEAD]
    mu = jnp.mean(x, axis=-1, keepdims=True)
    var = jnp.mean(jnp.square(x - mu), axis=-1, keepdims=True)
    y = (x - mu) * lax.rsqrt(var + LN_EPS)
    y = y * g_ref[...] + beta_ref[...]
    ik_ref[...] = _rope_apply(y, c_ref[...], a_ref[...], b_ref[...], half).astype(ik_ref.dtype)
    iw_ref[...] = p_ref[:, HEAD:] * jnp.float32(iw_scale)


def _idx_prep(p2, seq, ik_g, ik_b, tables, half):
    m = p2.shape[0]
    tm = _tile(seq, 512, SUBLANES)
    sb = seq // tm
    tab_spec = pl.BlockSpec((tm, LANES), lambda i: (i % sb, 0))
    vec_spec = pl.BlockSpec((1, HEAD), lambda i: (0, 0))
    blk = pl.BlockSpec((tm, HEAD), lambda i: (i, 0))
    return pl.pallas_call(
        functools.partial(_idx_prep_kernel, half=half, iw_scale=IDX_HEADS ** -0.5 * HEAD ** -0.5),
        grid=(m // tm,),
        in_specs=[pl.BlockSpec((tm, 2 * HEAD), lambda i: (i, 0)), vec_spec, vec_spec, tab_spec, tab_spec, tab_spec],
        out_specs=[blk, blk],
        out_shape=[jax.ShapeDtypeStruct((m, HEAD), BF16), jax.ShapeDtypeStruct((m, HEAD), F32)],
        compiler_params=_params("parallel"),
        name="idx_prep",
    )(p2, ik_g.reshape(1, HEAD), ik_b.reshape(1, HEAD), *tables)


DSA_TQ = 128
DSA_KC = 512


def _indexer_kernel(iq_ref, ik_ref, iw_ref, mask_ref, key_ref, *, n_sel, n_chunks):
    qb = pl.program_id(1)
    tq, kc = DSA_TQ, DSA_KC
    nch = (qb * tq) // kc + 1
    idx_bits = (n_chunks * kc - 1).bit_length()
    iw = iw_ref[0]
    q_pos = qb * tq + lax.broadcasted_iota(jnp.int32, (tq, 1), 0)
    lane = lax.broadcasted_iota(jnp.int32, (1, kc), 1)

    def score_chunk(c, _):
        ikc = ik_ref[0, pl.ds(pl.multiple_of(c * kc, kc), kc), :]
        score = jnp.zeros((tq, kc), F32)
        for h in range(IDX_HEADS):
            rel = lax.dot_general(iq_ref[0, :, h * HEAD:(h + 1) * HEAD], ikc, (((1,), (1,)), ((), ())),
                                  preferred_element_type=F32)
            score = score + jnp.maximum(rel, 0.0) * iw[:, h:h + 1]
        bits = pltpu.bitcast(score, jnp.int32)
        key = jnp.where(bits < 0, bits ^ jnp.int32(0x7FFFFFFF), bits)
        key = jnp.where(bits == jnp.int32(INT_MIN), jnp.int32(0), key)
        key = jnp.where(c * kc + lane <= q_pos, key, jnp.int32(INT_MIN))
        key_ref[c] = key
        return 0

    lax.fori_loop(0, nch, score_chunk, 0)

    def count(pred):
        def body(c, acc):
            hit = jnp.where(pred(key_ref[c], c), jnp.int32(1), jnp.int32(0))
            for b in range(kc // LANES):
                acc = acc + hit[:, b * LANES:(b + 1) * LANES]
            return acc
        acc = lax.fori_loop(0, nch, body, jnp.zeros((tq, LANES), jnp.int32))
        return jnp.sum(acc, axis=1, keepdims=True)

    cnt0 = count(lambda key, c: key >= 0)
    t0 = jnp.where(cnt0 >= n_sel, jnp.int32(0), jnp.int32(INT_MIN))

    def bit_step(i, t):
        cand = t | jnp.left_shift(jnp.int32(1), jnp.int32(30) - i)
        cnt = count(lambda key, c: key >= cand)
        return jnp.where(cnt >= n_sel, cand, t)

    thr = lax.fori_loop(0, 31, bit_step, t0)
    n_ge = count(lambda key, c: key >= thr)
    real = thr != jnp.int32(INT_MIN)
    excess = jnp.max(jnp.where(jnp.logical_and(real, n_ge > n_sel), 1, 0)) > 0

    def zero_tail():
        def body(c, _):
            mask_ref[0, 0, c] = jnp.zeros((tq, kc), mask_ref.dtype)
            return 0
        lax.fori_loop(nch, n_chunks, body, 0)

    @pl.when(jnp.logical_not(excess))
    def _():
        thr_eff = jnp.maximum(thr, jnp.int32(INT_MIN + 1))

        def body(c, _):
            mask_ref[0, 0, c] = jnp.where(key_ref[c] >= thr_eff, 1.0, 0.0).astype(mask_ref.dtype)
            return 0
        lax.fori_loop(0, nch, body, 0)
        zero_tail()

    @pl.when(excess)
    def _():
        need = n_sel - count(lambda key, c: key > thr)

        def idx_step(i, j):
            cand = j | jnp.left_shift(jnp.int32(1), jnp.int32(idx_bits - 1) - i)
            cnt = count(lambda key, c: jnp.logical_and(key == thr, c * kc + lane < cand))
            return jnp.where(cnt < need, cand, j)

        last = lax.fori_loop(0, idx_bits, idx_step, jnp.zeros((tq, 1), jnp.int32))

        def body(c, _):
            key = key_ref[c]
            tie = jnp.logical_and(jnp.logical_and(key == thr, c * kc + lane <= last), real)
            sel = jnp.logical_or(key > thr, tie)
            mask_ref[0, 0, c] = jnp.where(sel, 1.0, 0.0).astype(mask_ref.dtype)
            return 0
        lax.fori_loop(0, nch, body, 0)
        zero_tail()


def _indexer(iq, ik, iw, n_sel):
    bsz, seq = ik.shape[0], ik.shape[1]
    n_chunks = seq // DSA_KC
    nq = seq // DSA_TQ
    iq_arr, iq_blk = iq
    return pl.pallas_call(
        functools.partial(_indexer_kernel, n_sel=n_sel, n_chunks=n_chunks),
        grid=(bsz, nq),
        in_specs=[pl.BlockSpec((1, DSA_TQ, IDX_HEADS * HEAD), lambda b, q: (b, q, iq_blk)),
                  pl.BlockSpec((1, seq, HEAD), lambda b, q: (b, 0, 0)),
                  pl.BlockSpec((1, DSA_TQ, HEAD), lambda b, q: (b, q, 0))],
        out_specs=pl.BlockSpec((1, 1, n_chunks, DSA_TQ, DSA_KC), lambda b, q: (b, q, 0, 0, 0)),
        out_shape=jax.ShapeDtypeStruct((bsz, nq, n_chunks, DSA_TQ, DSA_KC), BF16),
        scratch_shapes=[pltpu.VMEM((n_chunks, DSA_TQ, DSA_KC), jnp.int32)],
        compiler_params=_params("parallel", "arbitrary"),
        name="dsa_indexer",
    )(iq_arr, ik, iw)


def _online_softmax_step(logits, v_c, m, l, acc):
    m_new = jnp.maximum(m, jnp.max(logits, axis=1, keepdims=True))
    alpha = jnp.exp(m - m_new)
    p = jnp.exp(logits - m_new)
    l = alpha * l + jnp.sum(p, axis=1, keepdims=True)
    acc = alpha * acc + jnp.dot(p.astype(v_c.dtype), v_c, preferred_element_type=F32)
    return m_new, l, acc


def _dsa_attn_kernel(q_ref, k_ref, v_ref, mask_ref, o_ref, *, groups):
    qb = pl.program_id(2)
    tq, kc = DSA_TQ, DSA_KC
    nch = (qb * tq) // kc + 1

    def body(c, carry):
        start = pl.multiple_of(c * kc, kc)
        k_c = k_ref[0, pl.ds(start, kc), :]
        v_c = v_ref[0, pl.ds(start, kc), :]
        sel = mask_ref[0, 0, c].astype(F32) > 0.5
        out = []
        for r in range(groups):
            m, l, acc = carry[r]
            s = lax.dot_general(q_ref[0, :, r * HEAD:(r + 1) * HEAD], k_c, (((1,), (1,)), ((), ())),
                                preferred_element_type=F32)
            out.append(_online_softmax_step(jnp.where(sel, s, NEG_INF), v_c, m, l, acc))
        return tuple(out)

    init = tuple((jnp.full((tq, 1), -jnp.inf, F32), jnp.zeros((tq, 1), F32), jnp.zeros((tq, HEAD), F32))
                 for _ in range(groups))
    res = lax.fori_loop(0, nch, body, init)
    for r in range(groups):
        m, l, acc = res[r]
        o_ref[0, :, r * HEAD:(r + 1) * HEAD] = (acc / l).astype(o_ref.dtype)


def _dsa_attention(q, k, v, mask, bsz, seq):
    groups = A_HEADS // A_KV_HEADS
    (q_arr, q0), (k_arr, k0), (v_arr, v0) = q, k, v
    nq = seq // DSA_TQ
    n_chunks = seq // DSA_KC
    qw = groups * HEAD
    return pl.pallas_call(
        functools.partial(_dsa_attn_kernel, groups=groups),
        grid=(bsz, A_KV_HEADS, nq),
        in_specs=[pl.BlockSpec((1, DSA_TQ, qw), lambda b, g, i: (b, i, q0 // groups + g)),
                  pl.BlockSpec((1, seq, HEAD), lambda b, g, i: (b, 0, k0 + g)),
                  pl.BlockSpec((1, seq, HEAD), lambda b, g, i: (b, 0, v0 + g)),
                  pl.BlockSpec((1, 1, n_chunks, DSA_TQ, DSA_KC), lambda b, g, i: (b, i, 0, 0, 0))],
        out_specs=pl.BlockSpec((1, DSA_TQ, qw), lambda b, g, i: (b, i, g)),
        out_shape=jax.ShapeDtypeStruct((bsz, seq, A_HEADS * HEAD), BF16),
        compiler_params=_params("parallel", "parallel", "arbitrary"),
        name="dsa_attention",
    )(q_arr, k_arr, v_arr, mask)


def _fox_cum_kernel(f_ref, bias_ref, cum_t_ref):
    z = f_ref[0] + bias_ref[...]
    x = jnp.minimum(z, 0.0) - jnp.log1p(jnp.exp(-jnp.abs(z)))
    seq = x.shape[0]
    row = lax.broadcasted_iota(jnp.int32, (seq, 1), 0)
    d = 1
    while d < seq:
        x = x + jnp.where(row >= d, pltpu.roll(x, d, axis=0), 0.0)
        d *= 2
    cum_t_ref[0] = x.T


def _fox_cum(f, bias):
    bsz, seq, _ = f.shape
    return pl.pallas_call(
        _fox_cum_kernel,
        grid=(bsz,),
        in_specs=[pl.BlockSpec((1, seq, HEAD), lambda b: (b, 0, 0)), pl.BlockSpec((1, HEAD), lambda b: (0, 0))],
        out_specs=pl.BlockSpec((1, HEAD, seq), lambda b: (b, 0, 0)),
        out_shape=jax.ShapeDtypeStruct((bsz, HEAD, seq), F32),
        compiler_params=_params("parallel"),
        name="fox_cum",
    )(f, bias)


FOX_TQ = 256
FOX_KC = 512


def _fox_attn_kernel(q_ref, k_ref, v_ref, cq_ref, ck_ref, o_ref, *, scale):
    qb = pl.program_id(2)
    tq, kc = q_ref.shape[1], FOX_KC
    nch = (qb * tq + tq - 1) // kc + 1
    q = (q_ref[0].astype(F32) * jnp.float32(scale)).astype(q_ref.dtype)
    cq = jnp.broadcast_to(cq_ref[0, 0, 0], (LANES, tq)).T[:, 0:1]
    q_pos = qb * tq + lax.broadcasted_iota(jnp.int32, (tq, 1), 0)
    lane = lax.broadcasted_iota(jnp.int32, (1, kc), 1)

    def body(c, carry):
        start = pl.multiple_of(c * kc, kc)
        k_c = k_ref[0, pl.ds(start, kc), :]
        v_c = v_ref[0, pl.ds(start, kc), :]
        s = lax.dot_general(q, k_c, (((1,), (1,)), ((), ())), preferred_element_type=F32)
        logits = s + cq - ck_ref[0, 0, c]
        logits = jnp.where(c * kc + lane <= q_pos, logits, NEG_INF)
        return _online_softmax_step(logits, v_c, *carry)

    init = (jnp.full((tq, 1), -jnp.inf, F32), jnp.zeros((tq, 1), F32), jnp.zeros((tq, HEAD), F32))
    m, l, acc = lax.fori_loop(0, nch, body, init)
    o_ref[0] = (acc / l).astype(o_ref.dtype)


def _fox_attention(p, cum_rows, bsz, seq):
    tq = _tile(seq, FOX_TQ, SUBLANES)
    n_chunks = seq // FOX_KC
    cq_rows = cum_rows.reshape(bsz, B_HEADS, 1, 1, seq)
    return pl.pallas_call(
        functools.partial(_fox_attn_kernel, scale=HEAD ** -0.5),
        grid=(bsz, B_HEADS, seq // tq),
        in_specs=[pl.BlockSpec((1, tq, HEAD), lambda b, h, i: (b, i, h)),
                  pl.BlockSpec((1, seq, HEAD), lambda b, h, i: (b, 0, B_HEADS + h)),
                  pl.BlockSpec((1, seq, HEAD), lambda b, h, i: (b, 0, 2 * B_HEADS + h)),
                  pl.BlockSpec((1, 1, 1, 1, tq), lambda b, h, i: (b, h, 0, 0, i)),
                  pl.BlockSpec((1, 1, n_chunks, 1, FOX_KC), lambda b, h, i: (b, h, 0, 0, 0))],
        out_specs=pl.BlockSpec((1, tq, HEAD), lambda b, h, i: (b, i, h)),
        out_shape=jax.ShapeDtypeStruct((bsz, seq, B_HEADS * HEAD), BF16),
        compiler_params=_params("parallel", "parallel", "arbitrary"),
        name="fox_attention",
    )(p, p, p, cq_rows, cum_rows)


def _swa_kernel(sinks_ref, q_ref, kp_ref, kc_ref, vp_ref, vc_ref, o_ref):
    n = pl.program_id(1)
    pair = pl.program_id(2)
    w, dh = WINDOW, C_HEAD_DIM
    groups = C_HEADS // C_KV_HEADS
    kv_per_blk = LANES // dh
    i = lax.broadcasted_iota(jnp.int32, (w, 1), 0)
    j = lax.broadcasted_iota(jnp.int32, (1, 2 * w), 1)
    valid = jnp.logical_and(jnp.logical_and(j > i, j <= i + w), n * w + j >= w)
    kk = jnp.concatenate([kp_ref[0], kc_ref[0]], axis=0)
    vv = jnp.concatenate([vp_ref[0], vc_ref[0]], axis=0)
    for g in range(kv_per_blk):
        k_g = kk[:, g * dh:(g + 1) * dh]
        v_g = vv[:, g * dh:(g + 1) * dh]
        for r in range(groups):
            col = (g * groups + r) * dh
            q = q_ref[0, :, col:col + dh]
            s = lax.dot_general(q, k_g, (((1,), (1,)), ((), ())), preferred_element_type=F32)
            logits = jnp.where(valid, s, NEG_INF)
            sink = sinks_ref[(pair * kv_per_blk + g) * groups + r]
            m = jnp.maximum(jnp.max(logits, axis=1, keepdims=True), sink)
            p = jnp.exp(logits - m)
            denom = jnp.sum(p, axis=1, keepdims=True) + jnp.exp(sink - m)
            o = jnp.dot((p / denom).astype(v_g.dtype), v_g, preferred_element_type=F32)
            o_ref[0, :, col:col + dh] = o.astype(o_ref.dtype)


def _swa_attention(pr, p, sinks, bsz, seq):
    w = WINDOW
    nb = seq // w
    kv_per_blk = LANES // C_HEAD_DIM
    pairs = C_KV_HEADS // kv_per_blk
    qw = kv_per_blk * (C_HEADS // C_KV_HEADS) * C_HEAD_DIM
    k0 = C_HEADS * C_HEAD_DIM // LANES
    v0 = k0 + C_KV_HEADS * C_HEAD_DIM // LANES
    prev = lambda col0: (lambda b, n, g: (b, jnp.maximum(n - 1, 0), col0 + g))
    cur = lambda col0: (lambda b, n, g: (b, n, col0 + g))
    blk = (1, w, LANES)
    return pl.pallas_call(
        _swa_kernel,
        grid=(bsz, nb, pairs),
        in_specs=[pl.BlockSpec(memory_space=pltpu.SMEM),
                  pl.BlockSpec((1, w, qw), lambda b, n, g: (b, n, g)),
                  pl.BlockSpec(blk, prev(k0)), pl.BlockSpec(blk, cur(k0)),
                  pl.BlockSpec(blk, prev(v0)), pl.BlockSpec(blk, cur(v0))],
        out_specs=pl.BlockSpec((1, w, qw), lambda b, n, g: (b, n, g)),
        out_shape=jax.ShapeDtypeStruct((bsz, seq, C_HEADS * C_HEAD_DIM), BF16),
        compiler_params=_params("parallel", "parallel", "parallel"),
        name="swa_attention",
    )(sinks, pr, pr, pr, p, p)


def _ffn_up_kernel(h_ref, wg_ref, wv_ref, cwg_ref, cwv_ref, cbg_ref, cbv_ref, o_ref, carry_g, carry_v, *, tiles_per_seq):
    mi = pl.program_id(0)
    ni = pl.program_id(1)
    tm = h_ref.shape[0]
    h = h_ref[...]
    seq_start = (mi % tiles_per_seq) == 0
    row = lax.broadcasted_iota(jnp.int32, (tm, 1), 0)

    def conv(u, carry_ref, cw_ref, cb_ref):
        @pl.when(seq_start)
        def _():
            carry_ref[ni] = jnp.zeros(carry_ref.shape[1:], F32)

        tail = carry_ref[ni]
        carry_ref[ni] = u[tm - SUBLANES:, :]
        p1 = jnp.where(row == 0, tail[SUBLANES - 1:SUBLANES, :], pltpu.roll(u, 1, axis=0))
        p2 = jnp.where(row == 0, tail[SUBLANES - 2:SUBLANES - 1, :],
                       jnp.where(row == 1, tail[SUBLANES - 1:SUBLANES, :], pltpu.roll(u, 2, axis=0)))
        return cb_ref[...] + (p2 * cw_ref[0:1, :] + p1 * cw_ref[1:2, :] + u * cw_ref[2:3, :])

    gate = conv(jnp.dot(h, wg_ref[...], preferred_element_type=F32), carry_g, cwg_ref, cbg_ref)
    val = conv(jnp.dot(h, wv_ref[...], preferred_element_type=F32), carry_v, cwv_ref, cbv_ref)
    o_ref[...] = (gate * jax.nn.sigmoid(gate) * val).astype(o_ref.dtype)


def _ffn_up(h, w_up, conv_w, conv_b, seq, *, tm=1024, tn=256):
    m, d = h.shape
    dff = w_up.shape[1] // 2
    tm = _tile(seq, tm, SUBLANES)
    tn = _tile(dff, tn)
    nt = dff // tn
    assert CONV_WIDTH == 3 and tm >= SUBLANES
    conv_b = conv_b.reshape(1, 2 * dff)
    return pl.pallas_call(
        functools.partial(_ffn_up_kernel, tiles_per_seq=seq // tm),
        grid=(m // tm, nt),
        in_specs=[pl.BlockSpec((tm, d), lambda i, j: (i, 0)),
                  pl.BlockSpec((d, tn), lambda i, j: (0, j)),
                  pl.BlockSpec((d, tn), lambda i, j: (0, j + nt)),
                  pl.BlockSpec((CONV_WIDTH, tn), lambda i, j: (0, j)),
                  pl.BlockSpec((CONV_WIDTH, tn), lambda i, j: (0, j + nt)),
                  pl.BlockSpec((1, tn), lambda i, j: (0, j)),
                  pl.BlockSpec((1, tn), lambda i, j: (0, j + nt))],
        out_specs=pl.BlockSpec((tm, tn), lambda i, j: (i, j)),
        out_shape=jax.ShapeDtypeStruct((m, dff), BF16),
        scratch_shapes=[pltpu.VMEM((nt, SUBLANES, tn), F32), pltpu.VMEM((nt, SUBLANES, tn), F32)],
        compiler_params=_params("arbitrary", "arbitrary"),
        name="ffn_up",
    )(h, w_up, w_up, conv_w, conv_w, conv_b, conv_b)


def _pad_cols(w, n):
    return jnp.pad(w, ((0, 0), (0, n - w.shape[1])))


def _dsa_mixer(h, x, w_in, ik_g, ik_b, w_out, bsz, seq):
    d = h.shape[1]
    a_q, a_kv, a_iq = A_HEADS * HEAD, A_KV_HEADS * HEAD, IDX_HEADS * HEAD
    wq, wk, wv, wiq, wik, wiw = jnp.split(w_in, np.cumsum([a_q, a_kv, a_kv, a_iq, HEAD]).tolist(), axis=1)
    w_main = jnp.concatenate([wq, wiq, wk, wv], axis=1).astype(BF16)
    w_small = _pad_cols(jnp.concatenate([wik, wiw], axis=1), 2 * HEAD).astype(BF16)
    p = _matmul(h, w_main, out_dtype=BF16)
    p2 = _matmul(h, w_small, out_dtype=F32)
    tables, half = _rope_tables(seq, HEAD, HEAD)
    n_rope = a_q + a_iq + a_kv
    pr = _rope(p, seq, tables, half, n_cols=n_rope, n_scaled_cols=a_q, scale=HEAD ** -0.5)
    ik, iw = _idx_prep(p2, seq, ik_g, ik_b, tables, half)
    pr3 = pr.reshape(bsz, seq, n_rope)
    p3 = p.reshape(bsz, seq, p.shape[1])
    n_sel = min(INDEX_TOPK, seq // 4)
    mask = _indexer((pr3, a_q // a_iq), ik.reshape(bsz, seq, HEAD), iw.reshape(bsz, seq, HEAD), n_sel)
    o = _dsa_attention((pr3, 0), (pr3, (a_q + a_iq) // HEAD), (p3, n_rope // HEAD), mask, bsz, seq)
    return _matmul(o.reshape(bsz * seq, a_q), w_out.astype(BF16), res=x, out_dtype=F32)


def _fox_mixer(h, x, w_in, f_bias, w_out, bsz, seq):
    d = h.shape[1]
    w_main = w_in[:, :3 * d].astype(BF16)
    w_f = _pad_cols(w_in[:, 3 * d:], HEAD).astype(BF16)
    p = _matmul(h, w_main, out_dtype=BF16)
    f = _matmul(h, w_f, out_dtype=F32, tn=HEAD)
    bias = jnp.pad(f_bias.astype(F32), (0, HEAD - B_HEADS)).reshape(1, HEAD)
    cum_t = _fox_cum(f.reshape(bsz, seq, HEAD), bias)
    cum_rows = cum_t[:, :B_HEADS].reshape(bsz, B_HEADS, seq // FOX_KC, 1, FOX_KC)
    o = _fox_attention(p.reshape(bsz, seq, 3 * d), cum_rows, bsz, seq)
    return _matmul(o.reshape(bsz * seq, d), w_out.astype(BF16), res=x, out_dtype=F32)


def _swa_mixer(h, x, w_in, sinks, w_out, bsz, seq):
    c_q, c_kv = C_HEADS * C_HEAD_DIM, C_KV_HEADS * C_HEAD_DIM
    p = _matmul(h, w_in.astype(BF16), out_dtype=BF16)
    tables, half = _rope_tables(seq, C_HEAD_DIM, C_HEAD_DIM)
    pr = _rope(p, seq, tables, half, n_cols=c_q + c_kv, n_scaled_cols=c_q, scale=C_HEAD_DIM ** -0.5)
    o = _swa_attention(pr.reshape(bsz, seq, c_q + c_kv), p.reshape(bsz, seq, p.shape[1]),
                       sinks.astype(F32), bsz, seq)
    return _matmul(o.reshape(bsz * seq, c_q), w_out.astype(BF16), res=x, out_dtype=F32)


def _conv_glu_ffn(h, x, w_up, conv_w, conv_b, w_down, seq):
    g = _ffn_up(h, w_up.astype(BF16), conv_w, conv_b, seq)
    dff = g.shape[1]
    return _matmul(g, w_down.astype(BF16), res=x, out_dtype=F32, tm=512, tn=1024, tk=dff // 2)


def kernel(x, attn_norm, ffn_norm, final_norm, a_w_in, a_idx_k_norm_g, a_idx_k_norm_b, a_w_out,
           b_w_in, b_f_bias, b_w_out, c_w_in, c_sinks, c_w_out,
           ffn_w_up, ffn_conv_w, ffn_conv_b, ffn_w_down):
    bsz, seq, d = x.shape
    depth = attn_norm.shape[0]
    x = x.reshape(bsz * seq, d)
    for layer in range(depth):
        mixer, slot = layer % N_MIXERS, layer // N_MIXERS
        h = _rmsnorm(x, attn_norm[layer], BF16)
        if mixer == 0:
            x = _dsa_mixer(h, x, a_w_in[slot], a_idx_k_norm_g[slot], a_idx_k_norm_b[slot], a_w_out[slot], bsz, seq)
        elif mixer == 1:
            x = _fox_mixer(h, x, b_w_in[slot], b_f_bias[slot], b_w_out[slot], bsz, seq)
        else:
            x = _swa_mixer(h, x, c_w_in[slot], c_sinks[slot], c_w_out[slot], bsz, seq)
        h = _rmsnorm(x, ffn_norm[layer], BF16)
        x = _conv_glu_ffn(h, x, ffn_w_up[layer], ffn_conv_w[layer], ffn_conv_b[layer], ffn_w_down[layer], seq)
    return _rmsnorm(x, final_norm, F32).reshape(bsz, seq, d)
```

```python
import functools
import math

import jax
import jax.numpy as jnp
import numpy as np
from jax import lax
from jax.experimental import pallas as pl
from jax.experimental.pallas import tpu as pltpu

F32 = jnp.float32
BF16 = jnp.bfloat16

N_MIXERS = 3
ROPE_THETA = 500000.0
ROPE_FRACTION = 4
NORM_EPS = 1e-6
LN_EPS = 1e-6
NEG_INF = -1e30
LOG2E = math.log2(math.e)
HEAD = 128
A_HEADS, A_KV_HEADS, IDX_HEADS, INDEX_TOPK = 32, 8, 16, 256
B_HEADS = 32
C_HEADS, C_KV_HEADS, C_HEAD_DIM, WINDOW = 64, 8, 64, 128
CONV_WIDTH = 3

LANES = 128
SUBLANES = 8
VMEM_LIMIT_BYTES = 56 * 1024 * 1024
INT_MIN = -(2 ** 31)


def _params(*sem):
    return pltpu.CompilerParams(dimension_semantics=sem, vmem_limit_bytes=VMEM_LIMIT_BYTES)


def _tile(dim, target, quantum=LANES):
    if dim <= target:
        return dim
    best = None
    for t in range(quantum, target + 1, quantum):
        if dim % t == 0:
            best = t
    assert best is not None, (dim, target)
    return best


def _rmsnorm_kernel(x_ref, g_ref, o_ref):
    x = x_ref[...]
    y = x * lax.rsqrt(jnp.mean(x * x, axis=-1, keepdims=True) + NORM_EPS)
    o_ref[...] = (y * g_ref[...]).astype(o_ref.dtype)


def _rmsnorm(x, g, out_dtype):
    m, d = x.shape
    tm = _tile(m, 256, SUBLANES)
    return pl.pallas_call(
        _rmsnorm_kernel,
        grid=(m // tm,),
        in_specs=[pl.BlockSpec((tm, d), lambda i: (i, 0)), pl.BlockSpec((1, d), lambda i: (0, 0))],
        out_specs=pl.BlockSpec((tm, d), lambda i: (i, 0)),
        out_shape=jax.ShapeDtypeStruct((m, d), out_dtype),
        compiler_params=_params("parallel"),
        name="rmsnorm",
    )(x, g.reshape(1, d))


def _mm_kernel(*refs, nk, has_res):
    if has_res:
        a_ref, w_ref, r_ref, o_ref = refs[:4]
        scratch = refs[4:]
    else:
        a_ref, w_ref, o_ref = refs[:3]
        r_ref = None
        scratch = refs[3:]
    part = jnp.dot(a_ref[...], w_ref[...], preferred_element_type=F32)
    if nk == 1:
        if has_res:
            part = part + r_ref[...]
        o_ref[...] = part.astype(o_ref.dtype)
        return
    acc_ref = scratch[0]
    k = pl.program_id(2)

    @pl.when(k == 0)
    def _():
        acc_ref[...] = part

    @pl.when(k > 0)
    def _():
        acc_ref[...] += part

    @pl.when(k == nk - 1)
    def _():
        out = acc_ref[...]
        if has_res:
            out = out + r_ref[...]
        o_ref[...] = out.astype(o_ref.dtype)


def _matmul(a, w, *, res=None, out_dtype, tm=1024, tn=512, tk=None):
    m, kdim = a.shape
    n = w.shape[1]
    tm = _tile(m, tm, SUBLANES)
    tn = _tile(n, tn)
    tk = kdim if tk is None else _tile(kdim, tk)
    nk = kdim // tk
    in_specs = [pl.BlockSpec((tm, tk), lambda i, j, k: (i, k)), pl.BlockSpec((tk, tn), lambda i, j, k: (k, j))]
    args = [a, w]
    if res is not None:
        in_specs.append(pl.BlockSpec((tm, tn), lambda i, j, k: (i, j)))
        args.append(res)
    return pl.pallas_call(
        functools.partial(_mm_kernel, nk=nk, has_res=res is not None),
        grid=(m // tm, n // tn, nk),
        in_specs=in_specs,
        out_specs=pl.BlockSpec((tm, tn), lambda i, j, k: (i, j)),
        out_shape=jax.ShapeDtypeStruct((m, n), out_dtype),
        scratch_shapes=[pltpu.VMEM((tm, tn), F32)] if nk > 1 else [],
        compiler_params=_params("parallel", "parallel", "arbitrary"),
        name="matmul",
    )(*args)


def _rope_tables(seq, head_dim, lanes_per_head):
    rot = head_dim // ROPE_FRACTION
    half = rot // 2
    inv_freq = jnp.power(jnp.float32(ROPE_THETA), -jnp.arange(half, dtype=F32) * (2.0 / rot))
    ang = jnp.arange(seq, dtype=F32)[:, None] * inv_freq[None, :]
    cos, sin = jnp.cos(ang), jnp.sin(ang)
    ones = jnp.ones((seq, lanes_per_head - rot), F32)
    zeros_h = jnp.zeros((seq, half), F32)
    zeros_r = jnp.zeros((seq, lanes_per_head - rot), F32)
    c = jnp.concatenate([cos, cos, ones], axis=1)
    a = jnp.concatenate([-sin, zeros_h, zeros_r], axis=1)
    b = jnp.concatenate([zeros_h, sin, zeros_r], axis=1)
    reps = LANES // lanes_per_head
    return tuple(jnp.tile(t, (1, reps)) for t in (c, a, b)), half


def _rope_apply(x, c, a, b, half):
    up = pltpu.roll(x, LANES - half, axis=1)
    dn = pltpu.roll(x, half, axis=1)
    return x * c + up * a + dn * b


def _rope_kernel(x_ref, c_ref, a_ref, b_ref, o_ref, *, half, n_scaled, scale):
    j = pl.program_id(1)
    s = jnp.where(j < n_scaled, jnp.float32(scale), jnp.float32(1.0))
    c, a, b = c_ref[...], a_ref[...], b_ref[...]
    for blk in range(x_ref.shape[1] // LANES):
        sl = slice(blk * LANES, (blk + 1) * LANES)
        x = x_ref[:, sl].astype(F32)
        o_ref[:, sl] = (_rope_apply(x, c, a, b, half) * s).astype(o_ref.dtype)


def _rope(p, seq, tables, half, *, n_cols, n_scaled_cols, scale, tc=512):
    m = p.shape[0]
    tm = _tile(seq, 512, SUBLANES)
    tc = _tile(math.gcd(n_cols, n_scaled_cols) if n_scaled_cols else n_cols, tc)
    sb = seq // tm
    tab_spec = pl.BlockSpec((tm, LANES), lambda i, j: (i % sb, 0))
    return pl.pallas_call(
        functools.partial(_rope_kernel, half=half, n_scaled=n_scaled_cols // tc, scale=scale),
        grid=(m // tm, n_cols // tc),
        in_specs=[pl.BlockSpec((tm, tc), lambda i, j: (i, j)), tab_spec, tab_spec, tab_spec],
        out_specs=pl.BlockSpec((tm, tc), lambda i, j: (i, j)),
        out_shape=jax.ShapeDtypeStruct((m, n_cols), p.dtype),
        compiler_params=_params("parallel", "parallel"),
        name="rope",
    )(p, *tables)


def _idx_prep_kernel(p_ref, g_ref, beta_ref, c_ref, a_ref, b_ref, ik_ref, iw_ref, *, half, iw_scale):
    x = p_ref[:, :HEAD]
    mu = jnp.mean(x, axis=-1, keepdims=True)
    var = jnp.mean(jnp.square(x - mu), axis=-1, keepdims=True)
    y = (x - mu) * lax.rsqrt(var + LN_EPS)
    y = y * g_ref[...] + beta_ref[...]
    ik_ref[...] = _rope_apply(y, c_ref[...], a_ref[...], b_ref[...], half).astype(ik_ref.dtype)
    iw_ref[...] = p_ref[:, HEAD:] * jnp.float32(iw_scale)


def _idx_prep(p2, seq, ik_g, ik_b, tables, half):
    m = p2.shape[0]
    tm = _tile(seq, 512, SUBLANES)
    sb = seq // tm
    tab_spec = pl.BlockSpec((tm, LANES), lambda i: (i % sb, 0))
    vec_spec = pl.BlockSpec((1, HEAD), lambda i: (0, 0))
    blk = pl.BlockSpec((tm, HEAD), lambda i: (i, 0))
    return pl.pallas_call(
        functools.partial(_idx_prep_kernel, half=half, iw_scale=IDX_HEADS ** -0.5 * HEAD ** -0.5),
        grid=(m // tm,),
        in_specs=[pl.BlockSpec((tm, 2 * HEAD), lambda i: (i, 0)), vec_spec, vec_spec, tab_spec, tab_spec, tab_spec],
        out_specs=[blk, blk],
        out_shape=[jax.ShapeDtypeStruct((m, HEAD), BF16), jax.ShapeDtypeStruct((m, HEAD), F32)],
        compiler_params=_params("parallel"),
        name="idx_prep",
    )(p2, ik_g.reshape(1, HEAD), ik_b.reshape(1, HEAD), *tables)


DSA_TQ = 128
DSA_KC = 512


def _indexer_kernel(iq_ref, ik_ref, iw_ref, mask_ref, key_ref, *, n_sel, n_chunks):
    qb = pl.program_id(1)
    tq, kc = DSA_TQ, DSA_KC
    nch = (qb * tq) // kc + 1
    idx_bits = (n_chunks * kc - 1).bit_length()
    iw = iw_ref[0]
    q_pos = qb * tq + lax.broadcasted_iota(jnp.int32, (tq, 1), 0)
    lane = lax.broadcasted_iota(jnp.int32, (1, kc), 1)

    def score_chunk(c, _):
        ikc = ik_ref[0, pl.ds(pl.multiple_of(c * kc, kc), kc), :]
        score = jnp.zeros((tq, kc), F32)
        for h in range(IDX_HEADS):
            rel = lax.dot_general(iq_ref[0, :, h * HEAD:(h + 1) * HEAD], ikc, (((1,), (1,)), ((), ())),
                                  preferred_element_type=F32)
            score = score + jnp.maximum(rel, 0.0) * iw[:, h:h + 1]
        bits = pltpu.bitcast(score, jnp.int32)
        key = jnp.where(bits < 0, bits ^ jnp.int32(0x7FFFFFFF), bits)
        key = jnp.where(bits == jnp.int32(INT_MIN), jnp.int32(0), key)
        key = jnp.where(c * kc + lane <= q_pos, key, jnp.int32(INT_MIN))
        key_ref[c] = key
        return 0

    lax.fori_loop(0, nch, score_chunk, 0)

    lane128 = lax.broadcasted_iota(jnp.int32, (1, LANES), 1)

    def for_blocks(c, fn):
        key = key_ref[c]
        return [fn(key[:, b * LANES:(b + 1) * LANES], c * kc + b * LANES + lane128) for b in range(kc // LANES)]

    def count(pred):
        def body(c, acc):
            for hit in for_blocks(c, lambda key, pos: jnp.where(pred(key, pos), jnp.int32(1), jnp.int32(0))):
                acc = acc + hit
            return acc
        acc = lax.fori_loop(0, nch, body, jnp.zeros((tq, LANES), jnp.int32))
        return jnp.broadcast_to(jnp.sum(acc, axis=1, keepdims=True), (tq, LANES))

    def write_mask(sel_fn):
        def body(c, _):
            blocks = for_blocks(c, lambda key, pos: jnp.where(sel_fn(key, pos), 1.0, 0.0))
            mask_ref[0, 0, c] = jnp.concatenate(blocks, axis=1).astype(mask_ref.dtype)
            return 0
        lax.fori_loop(0, nch, body, 0)

        def tail(c, _):
            mask_ref[0, 0, c] = jnp.zeros((tq, kc), mask_ref.dtype)
            return 0
        lax.fori_loop(nch, n_chunks, tail, 0)

    cnt0 = count(lambda key, pos: key >= 0)
    t0 = jnp.where(cnt0 >= n_sel, jnp.int32(0), jnp.int32(INT_MIN))

    def bit_step(i, t):
        cand = t | jnp.left_shift(jnp.int32(1), jnp.int32(30) - i)
        cnt = count(lambda key, pos: key >= cand)
        return jnp.where(cnt >= n_sel, cand, t)

    thr = lax.fori_loop(0, 31, bit_step, t0)
    n_ge = count(lambda key, pos: key >= thr)
    real = thr != jnp.int32(INT_MIN)
    excess = jnp.max(jnp.where(jnp.logical_and(real, n_ge > n_sel), 1, 0)) > 0

    @pl.when(jnp.logical_not(excess))
    def _():
        thr_eff = jnp.maximum(thr, jnp.int32(INT_MIN + 1))
        write_mask(lambda key, pos: key >= thr_eff)

    @pl.when(excess)
    def _():
        need = n_sel - count(lambda key, pos: key > thr)

        def idx_step(i, j):
            cand = j | jnp.left_shift(jnp.int32(1), jnp.int32(idx_bits - 1) - i)
            cnt = count(lambda key, pos: jnp.logical_and(key == thr, pos < cand))
            return jnp.where(cnt < need, cand, j)

        last = lax.fori_loop(0, idx_bits, idx_step, jnp.zeros((tq, LANES), jnp.int32))
        write_mask(lambda key, pos: jnp.logical_or(
            key > thr, jnp.logical_and(jnp.logical_and(key == thr, pos <= last), real)))


def _indexer(iq, ik, iw, n_sel):
    bsz, seq = ik.shape[0], ik.shape[1]
    n_chunks = seq // DSA_KC
    nq = seq // DSA_TQ
    iq_arr, iq_blk = iq
    return pl.pallas_call(
        functools.partial(_indexer_kernel, n_sel=n_sel, n_chunks=n_chunks),
        grid=(bsz, nq),
        in_specs=[pl.BlockSpec((1, DSA_TQ, IDX_HEADS * HEAD), lambda b, q: (b, q, iq_blk)),
                  pl.BlockSpec((1, seq, HEAD), lambda b, q: (b, 0, 0)),
                  pl.BlockSpec((1, DSA_TQ, HEAD), lambda b, q: (b, q, 0))],
        out_specs=pl.BlockSpec((1, 1, n_chunks, DSA_TQ, DSA_KC), lambda b, q: (b, q, 0, 0, 0)),
        out_shape=jax.ShapeDtypeStruct((bsz, nq, n_chunks, DSA_TQ, DSA_KC), BF16),
        scratch_shapes=[pltpu.VMEM((n_chunks, DSA_TQ, DSA_KC), jnp.int32)],
        compiler_params=_params("parallel", "arbitrary"),
        name="dsa_indexer",
    )(iq_arr, ik, iw)


def _lane_blocks(x):
    return [x[:, j * LANES:(j + 1) * LANES] for j in range(x.shape[1] // LANES)]


STREAMS = 2


def _two_sweep_attention(nch, rows, logits_fn, pv_fn, s_ref, mx_ref, ls_ref, acc_ref):
    mx_ref[...] = jnp.full(mx_ref.shape, -jnp.inf, F32)

    def sweep1_chunk(c, last):
        for i, logits2 in enumerate(logits_fn(c, last)):
            sl = slice(i * rows, (i + 1) * rows)
            s_ref[c, sl, :] = logits2
            mx = mx_ref[sl, :]
            for blk in _lane_blocks(logits2):
                mx = jnp.maximum(mx, blk)
            mx_ref[sl, :] = mx

    def sweep1(c, _):
        sweep1_chunk(c, False)
        return 0

    lax.fori_loop(0, nch - 1, sweep1, 0)
    sweep1_chunk(nch - 1, True)

    row_max = jnp.max(mx_ref[...], axis=1, keepdims=True)
    mx_ref[...] = jnp.broadcast_to(row_max, mx_ref.shape)
    ls_ref[...] = jnp.zeros(ls_ref.shape, F32)
    acc_ref[...] = jnp.zeros(acc_ref.shape, F32)

    def sweep2(c, _):
        for i in range(STREAMS):
            sl = slice(i * rows, (i + 1) * rows)
            m = mx_ref[sl, :]
            ps = [jnp.exp2(blk - m) for blk in _lane_blocks(s_ref[c, sl, :])]
            ls_ref[sl, :] += sum(ps[1:], ps[0])
            acc_ref[sl, :] += pv_fn(c, i, jnp.concatenate(ps, axis=1).astype(BF16))
        return 0

    lax.fori_loop(0, nch, sweep2, 0)
    return acc_ref[...] / jnp.sum(ls_ref[...], axis=1, keepdims=True)


def _softmax_scratch(rows, n_chunks, kc):
    return [pltpu.VMEM((n_chunks, rows, kc), F32),
            pltpu.VMEM((rows, LANES), F32),
            pltpu.VMEM((rows, LANES), F32),
            pltpu.VMEM((rows, HEAD), F32)]


def _dsa_attn_kernel(q_ref, k_ref, v_ref, mask_ref, o_ref, s_ref, mx_ref, ls_ref, acc_ref, *, groups):
    qb = pl.program_id(2)
    tq, kc = DSA_TQ, DSA_KC
    nch = (qb * tq) // kc + 1
    rows = groups * tq
    qs = [jnp.concatenate([q_ref[0, :, (i * groups + r) * HEAD:(i * groups + r + 1) * HEAD] for r in range(groups)],
                          axis=0) for i in range(STREAMS)]

    def logits_fn(c, last):
        start = pl.multiple_of(c * kc, kc)
        sel = (mask_ref[0, 0, c].astype(F32) > 0.5)[None]
        out = []
        for i in range(STREAMS):
            k_c = k_ref[0, pl.ds(start, kc), i * HEAD:(i + 1) * HEAD]
            raw = lax.dot_general(qs[i], k_c, (((1,), (1,)), ((), ())), preferred_element_type=F32)
            out.append(jnp.where(sel, raw.reshape(groups, tq, kc), NEG_INF).reshape(rows, kc))
        return out

    def pv_fn(c, i, p):
        v_c = v_ref[0, pl.ds(pl.multiple_of(c * kc, kc), kc), i * HEAD:(i + 1) * HEAD]
        return jnp.dot(p, v_c, preferred_element_type=F32)

    out = _two_sweep_attention(nch, rows, logits_fn, pv_fn, s_ref, mx_ref, ls_ref, acc_ref)
    for r in range(STREAMS * groups):
        o_ref[0, :, r * HEAD:(r + 1) * HEAD] = out[r * tq:(r + 1) * tq].astype(o_ref.dtype)


def _dsa_attention(q, k, v, mask, bsz, seq):
    groups = A_HEADS // A_KV_HEADS
    (q_arr, q0), (k_arr, k0), (v_arr, v0) = q, k, v
    nq = seq // DSA_TQ
    n_chunks = seq // DSA_KC
    qw = STREAMS * groups * HEAD
    kw = STREAMS * HEAD
    assert q0 % (STREAMS * groups) == 0 and k0 % STREAMS == 0 and v0 % STREAMS == 0
    return pl.pallas_call(
        functools.partial(_dsa_attn_kernel, groups=groups),
        grid=(bsz, A_KV_HEADS // STREAMS, nq),
        in_specs=[pl.BlockSpec((1, DSA_TQ, qw), lambda b, g, i: (b, i, q0 // (STREAMS * groups) + g)),
                  pl.BlockSpec((1, seq, kw), lambda b, g, i: (b, 0, k0 // STREAMS + g)),
                  pl.BlockSpec((1, seq, kw), lambda b, g, i: (b, 0, v0 // STREAMS + g)),
                  pl.BlockSpec((1, 1, n_chunks, DSA_TQ, DSA_KC), lambda b, g, i: (b, i, 0, 0, 0))],
        out_specs=pl.BlockSpec((1, DSA_TQ, qw), lambda b, g, i: (b, i, g)),
        out_shape=jax.ShapeDtypeStruct((bsz, seq, A_HEADS * HEAD), BF16),
        scratch_shapes=_softmax_scratch(STREAMS * groups * DSA_TQ, n_chunks, DSA_KC),
        compiler_params=_params("parallel", "parallel", "arbitrary"),
        name="dsa_attention",
    )(q_arr, k_arr, v_arr, mask)


def _fox_cum_kernel(f_ref, bias_ref, cum_t_ref):
    z = f_ref[0] + bias_ref[...]
    x = jnp.minimum(z, 0.0) - jnp.log1p(jnp.exp(-jnp.abs(z)))
    seq = x.shape[0]
    row = lax.broadcasted_iota(jnp.int32, (seq, 1), 0)
    d = 1
    while d < seq:
        x = x + jnp.where(row >= d, pltpu.roll(x, d, axis=0), 0.0)
        d *= 2
    cum_t_ref[0] = (x * jnp.float32(LOG2E)).T


def _fox_cum(f, bias):
    bsz, seq, _ = f.shape
    return pl.pallas_call(
        _fox_cum_kernel,
        grid=(bsz,),
        in_specs=[pl.BlockSpec((1, seq, HEAD), lambda b: (b, 0, 0)), pl.BlockSpec((1, HEAD), lambda b: (0, 0))],
        out_specs=pl.BlockSpec((1, HEAD, seq), lambda b: (b, 0, 0)),
        out_shape=jax.ShapeDtypeStruct((bsz, HEAD, seq), F32),
        compiler_params=_params("parallel"),
        name="fox_cum",
    )(f, bias)


FOX_TQ = 512
FOX_KC = 512


def _fox_attn_kernel(q_ref, k_ref, v_ref, cq_ref, ck_ref, o_ref, s_ref, mx_ref, ls_ref, acc_ref, *, scale2):
    qb = pl.program_id(2)
    tq, kc = FOX_TQ, FOX_KC
    heads = [slice(i * HEAD, (i + 1) * HEAD) for i in range(STREAMS)]
    qs = [(q_ref[0, :, h].astype(F32) * jnp.float32(scale2)).astype(q_ref.dtype) for h in heads]
    cqs = [jnp.broadcast_to(cq_ref[0, i, 0], (LANES, tq)).T for i in range(STREAMS)]

    def logits_fn(c, last):
        start = pl.multiple_of(c * kc, kc)
        out = []
        for i in range(STREAMS):
            raw = lax.dot_general(qs[i], k_ref[0, pl.ds(start, kc), heads[i]], (((1,), (1,)), ((), ())),
                                  preferred_element_type=F32)
            ck = ck_ref[0, i, c]
            blocks = [blk + cqs[i] - ck[:, j * LANES:(j + 1) * LANES] for j, blk in enumerate(_lane_blocks(raw))]
            logits2 = jnp.concatenate(blocks, axis=1)
            if last:
                row = lax.broadcasted_iota(jnp.int32, (tq, 1), 0)
                lane = lax.broadcasted_iota(jnp.int32, (1, kc), 1)
                logits2 = jnp.where(lane <= row, logits2, NEG_INF)
            out.append(logits2)
        return out

    def pv_fn(c, i, p):
        v_c = v_ref[0, pl.ds(pl.multiple_of(c * kc, kc), kc), heads[i]]
        return jnp.dot(p, v_c, preferred_element_type=F32)

    out = _two_sweep_attention(qb + 1, tq, logits_fn, pv_fn, s_ref, mx_ref, ls_ref, acc_ref)
    for i in range(STREAMS):
        o_ref[0, :, heads[i]] = out[i * tq:(i + 1) * tq].astype(o_ref.dtype)


def _fox_attention(p, cum_rows, bsz, seq):
    tq = FOX_TQ
    assert FOX_TQ == FOX_KC and seq % tq == 0
    n_chunks = seq // FOX_KC
    hw = STREAMS * HEAD
    cq_rows = cum_rows.reshape(bsz, B_HEADS, 1, 1, seq)
    return pl.pallas_call(
        functools.partial(_fox_attn_kernel, scale2=HEAD ** -0.5 * LOG2E),
        grid=(bsz, B_HEADS // STREAMS, seq // tq),
        in_specs=[pl.BlockSpec((1, tq, hw), lambda b, h, i: (b, i, h)),
                  pl.BlockSpec((1, seq, hw), lambda b, h, i: (b, 0, B_HEADS // STREAMS + h)),
                  pl.BlockSpec((1, seq, hw), lambda b, h, i: (b, 0, 2 * B_HEADS // STREAMS + h)),
                  pl.BlockSpec((1, STREAMS, 1, 1, tq), lambda b, h, i: (b, h, 0, 0, i)),
                  pl.BlockSpec((1, STREAMS, n_chunks, 1, FOX_KC), lambda b, h, i: (b, h, 0, 0, 0))],
        out_specs=pl.BlockSpec((1, tq, hw), lambda b, h, i: (b, i, h)),
        out_shape=jax.ShapeDtypeStruct((bsz, seq, B_HEADS * HEAD), BF16),
        scratch_shapes=_softmax_scratch(STREAMS * tq, n_chunks, FOX_KC),
        compiler_params=_params("parallel", "parallel", "arbitrary"),
        name="fox_attention",
    )(p, p, p, cq_rows, cum_rows)


def _swa_kernel(sinks_ref, q_ref, kp_ref, kc_ref, vp_ref, vc_ref, o_ref):
    n = pl.program_id(1)
    pair = pl.program_id(2)
    w, dh = WINDOW, C_HEAD_DIM
    groups = C_HEADS // C_KV_HEADS
    kv_per_blk = LANES // dh
    i = lax.broadcasted_iota(jnp.int32, (w, 1), 0)
    j = lax.broadcasted_iota(jnp.int32, (1, 2 * w), 1)
    valid = jnp.logical_and(jnp.logical_and(j > i, j <= i + w), n * w + j >= w)
    kk = jnp.concatenate([kp_ref[0], kc_ref[0]], axis=0)
    vv = jnp.concatenate([vp_ref[0], vc_ref[0]], axis=0)
    for g in range(kv_per_blk):
        k_g = kk[:, g * dh:(g + 1) * dh]
        v_g = vv[:, g * dh:(g + 1) * dh]
        for r in range(groups):
            col = (g * groups + r) * dh
            q = q_ref[0, :, col:col + dh]
            s = lax.dot_general(q, k_g, (((1,), (1,)), ((), ())), preferred_element_type=F32)
            logits = jnp.where(valid, s, NEG_INF)
            sink = sinks_ref[(pair * kv_per_blk + g) * groups + r]
            m = jnp.maximum(jnp.max(logits, axis=1, keepdims=True), sink)
            p = jnp.exp(logits - m)
            denom = jnp.sum(p, axis=1, keepdims=True) + jnp.exp(sink - m)
            o = jnp.dot((p / denom).astype(v_g.dtype), v_g, preferred_element_type=F32)
            o_ref[0, :, col:col + dh] = o.astype(o_ref.dtype)


def _swa_attention(pr, p, sinks, bsz, seq):
    w = WINDOW
    nb = seq // w
    kv_per_blk = LANES // C_HEAD_DIM
    pairs = C_KV_HEADS // kv_per_blk
    qw = kv_per_blk * (C_HEADS // C_KV_HEADS) * C_HEAD_DIM
    k0 = C_HEADS * C_HEAD_DIM // LANES
    v0 = k0 + C_KV_HEADS * C_HEAD_DIM // LANES
    prev = lambda col0: (lambda b, n, g: (b, jnp.maximum(n - 1, 0), col0 + g))
    cur = lambda col0: (lambda b, n, g: (b, n, col0 + g))
    blk = (1, w, LANES)
    return pl.pallas_call(
        _swa_kernel,
        grid=(bsz, nb, pairs),
        in_specs=[pl.BlockSpec(memory_space=pltpu.SMEM),
                  pl.BlockSpec((1, w, qw), lambda b, n, g: (b, n, g)),
                  pl.BlockSpec(blk, prev(k0)), pl.BlockSpec(blk, cur(k0)),
                  pl.BlockSpec(blk, prev(v0)), pl.BlockSpec(blk, cur(v0))],
        out_specs=pl.BlockSpec((1, w, qw), lambda b, n, g: (b, n, g)),
        out_shape=jax.ShapeDtypeStruct((bsz, seq, C_HEADS * C_HEAD_DIM), BF16),
        compiler_params=_params("parallel", "parallel", "parallel"),
        name="swa_attention",
    )(sinks, pr, pr, pr, p, p)


def _ffn_up_kernel(h_ref, wg_ref, wv_ref, cwg_ref, cwv_ref, cbg_ref, cbv_ref, o_ref, w_scr, u_scr, carry,
                   *, tiles_per_seq):
    mi = pl.program_id(0)
    ni = pl.program_id(1)
    tm = h_ref.shape[0]
    tn = wg_ref.shape[1]
    w_scr[:, :tn] = wg_ref[...].astype(w_scr.dtype)
    w_scr[:, tn:] = wv_ref[...].astype(w_scr.dtype)

    @pl.when((mi % tiles_per_seq) == 0)
    def _():
        carry[ni] = jnp.zeros(carry.shape[1:], F32)

    u_scr[:SUBLANES, :] = carry[ni]
    u_scr[SUBLANES:, :] = jnp.dot(h_ref[...], w_scr[...], preferred_element_type=F32)
    carry[ni] = u_scr[tm:, :]

    def conv(cols, cw_ref, cb_ref):
        taps = [u_scr[pl.ds(SUBLANES - (CONV_WIDTH - 1) + tap, tm), cols] * cw_ref[tap:tap + 1, :]
                for tap in range(CONV_WIDTH)]
        return cb_ref[...] + sum(taps[1:], taps[0])

    gate = conv(slice(0, tn), cwg_ref, cbg_ref)
    val = conv(slice(tn, 2 * tn), cwv_ref, cbv_ref)
    o_ref[...] = (gate * jax.nn.sigmoid(gate) * val).astype(o_ref.dtype)


def _ffn_up(h, w_up, conv_w, conv_b, seq, *, tm=1024, tn=256):
    m, d = h.shape
    dff = w_up.shape[1] // 2
    tm = _tile(seq, tm, SUBLANES)
    tn = _tile(dff, tn)
    nt = dff // tn
    assert CONV_WIDTH - 1 <= SUBLANES <= tm
    conv_b = conv_b.reshape(1, 2 * dff)
    return pl.pallas_call(
        functools.partial(_ffn_up_kernel, tiles_per_seq=seq // tm),
        grid=(m // tm, nt),
        in_specs=[pl.BlockSpec((tm, d), lambda i, j: (i, 0)),
                  pl.BlockSpec((d, tn), lambda i, j: (0, j)),
                  pl.BlockSpec((d, tn), lambda i, j: (0, j + nt)),
                  pl.BlockSpec((CONV_WIDTH, tn), lambda i, j: (0, j)),
                  pl.BlockSpec((CONV_WIDTH, tn), lambda i, j: (0, j + nt)),
                  pl.BlockSpec((1, tn), lambda i, j: (0, j)),
                  pl.BlockSpec((1, tn), lambda i, j: (0, j + nt))],
        out_specs=pl.BlockSpec((tm, tn), lambda i, j: (i, j)),
        out_shape=jax.ShapeDtypeStruct((m, dff), BF16),
        scratch_shapes=[pltpu.VMEM((d, 2 * tn), BF16),
                        pltpu.VMEM((SUBLANES + tm, 2 * tn), F32),
                        pltpu.VMEM((nt, SUBLANES, 2 * tn), F32)],
        compiler_params=_params("arbitrary", "arbitrary"),
        name="ffn_up",
    )(h, w_up, w_up, conv_w, conv_w, conv_b, conv_b)


def _pad_cols(w, n):
    return jnp.pad(w, ((0, 0), (0, n - w.shape[1])))


def _dsa_mixer(h, x, w_in, ik_g, ik_b, w_out, bsz, seq):
    d = h.shape[1]
    a_q, a_kv, a_iq = A_HEADS * HEAD, A_KV_HEADS * HEAD, IDX_HEADS * HEAD
    wq, wk, wv, wiq, wik, wiw = jnp.split(w_in, np.cumsum([a_q, a_kv, a_kv, a_iq, HEAD]).tolist(), axis=1)
    w_main = jnp.concatenate([wq, wiq, wk, wv], axis=1).astype(BF16)
    w_small = _pad_cols(jnp.concatenate([wik, wiw], axis=1), 2 * HEAD).astype(BF16)
    p = _matmul(h, w_main, out_dtype=BF16)
    p2 = _matmul(h, w_small, out_dtype=F32)
    tables, half = _rope_tables(seq, HEAD, HEAD)
    n_rope = a_q + a_iq + a_kv
    pr = _rope(p, seq, tables, half, n_cols=n_rope, n_scaled_cols=a_q, scale=HEAD ** -0.5 * LOG2E)
    ik, iw = _idx_prep(p2, seq, ik_g, ik_b, tables, half)
    pr3 = pr.reshape(bsz, seq, n_rope)
    p3 = p.reshape(bsz, seq, p.shape[1])
    n_sel = min(INDEX_TOPK, seq // 4)
    mask = _indexer((pr3, a_q // a_iq), ik.reshape(bsz, seq, HEAD), iw.reshape(bsz, seq, HEAD), n_sel)
    o = _dsa_attention((pr3, 0), (pr3, (a_q + a_iq) // HEAD), (p3, n_rope // HEAD), mask, bsz, seq)
    return _matmul(o.reshape(bsz * seq, a_q), w_out.astype(BF16), res=x, out_dtype=F32)


def _fox_mixer(h, x, w_in, f_bias, w_out, bsz, seq):
    d = h.shape[1]
    w_main = w_in[:, :3 * d].astype(BF16)
    w_f = _pad_cols(w_in[:, 3 * d:], HEAD).astype(BF16)
    p = _matmul(h, w_main, out_dtype=BF16)
    f = _matmul(h, w_f, out_dtype=F32, tn=HEAD)
    bias = jnp.pad(f_bias.astype(F32), (0, HEAD - B_HEADS)).reshape(1, HEAD)
    cum_t = _fox_cum(f.reshape(bsz, seq, HEAD), bias)
    cum_rows = cum_t[:, :B_HEADS].reshape(bsz, B_HEADS, seq // FOX_KC, 1, FOX_KC)
    o = _fox_attention(p.reshape(bsz, seq, 3 * d), cum_rows, bsz, seq)
    return _matmul(o.reshape(bsz * seq, d), w_out.astype(BF16), res=x, out_dtype=F32)


def _swa_mixer(h, x, w_in, sinks, w_out, bsz, seq):
    c_q, c_kv = C_HEADS * C_HEAD_DIM, C_KV_HEADS * C_HEAD_DIM
    p = _matmul(h, w_in.astype(BF16), out_dtype=BF16)
    tables, half = _rope_tables(seq, C_HEAD_DIM, C_HEAD_DIM)
    pr = _rope(p, seq, tables, half, n_cols=c_q + c_kv, n_scaled_cols=c_q, scale=C_HEAD_DIM ** -0.5)
    o = _swa_attention(pr.reshape(bsz, seq, c_q + c_kv), p.reshape(bsz, seq, p.shape[1]),
                       sinks.astype(F32), bsz, seq)
    return _matmul(o.reshape(bsz * seq, c_q), w_out.astype(BF16), res=x, out_dtype=F32)


def _conv_glu_ffn(h, x, w_up, conv_w, conv_b, w_down, seq):
    g = _ffn_up(h, w_up, conv_w, conv_b, seq)
    dff = g.shape[1]
    return _matmul(g, w_down.astype(BF16), res=x, out_dtype=F32, tm=512, tn=1024, tk=dff // 2)


def kernel(x, attn_norm, ffn_norm, final_norm, a_w_in, a_idx_k_norm_g, a_idx_k_norm_b, a_w_out,
           b_w_in, b_f_bias, b_w_out, c_w_in, c_sinks, c_w_out,
           ffn_w_up, ffn_conv_w, ffn_conv_b, ffn_w_down):
    bsz, seq, d = x.shape
    depth = attn_norm.shape[0]
    x = x.reshape(bsz * seq, d)
    for layer in range(depth):
        mixer, slot = layer % N_MIXERS, layer // N_MIXERS
        h = _rmsnorm(x, attn_norm[layer], BF16)
        if mixer == 0:
            x = _dsa_mixer(h, x, a_w_in[slot], a_idx_k_norm_g[slot], a_idx_k_norm_b[slot], a_w_out[slot], bsz, seq)
        elif mixer == 1:
            x = _fox_mixer(h, x, b_w_in[slot], b_f_bias[slot], b_w_out[slot], bsz, seq)
        else:
            x = _swa_mixer(h, x, c_w_in[slot], c_sinks[slot], c_w_out[slot], bsz, seq)
        h = _rmsnorm(x, ffn_norm[layer], BF16)
        x = _conv_glu_ffn(h, x, ffn_w_up[layer], ffn_conv_w[layer], ffn_conv_b[layer], ffn_w_down[layer], seq)
    return _rmsnorm(x, final_norm, F32).reshape(bsz, seq, d)
```

```python
import functools
import math

import jax
import jax.numpy as jnp
import numpy as np
from jax import lax
from jax.experimental import pallas as pl
from jax.experimental.pallas import tpu as pltpu

F32 = jnp.float32
BF16 = jnp.bfloat16

N_MIXERS = 3
ROPE_THETA = 500000.0
ROPE_FRACTION = 4
NORM_EPS = 1e-6
LN_EPS = 1e-6
NEG_INF = -1e30
LOG2E = math.log2(math.e)
HEAD = 128
A_HEADS, A_KV_HEADS, IDX_HEADS, INDEX_TOPK = 32, 8, 16, 256
B_HEADS = 32
C_HEADS, C_KV_HEADS, C_HEAD_DIM, WINDOW = 64, 8, 64, 128
CONV_WIDTH = 3

LANES = 128
SUBLANES = 8
VMEM_LIMIT_BYTES = 56 * 1024 * 1024
INT_MIN = -(2 ** 31)


def _params(*sem):
    return pltpu.CompilerParams(dimension_semantics=sem, vmem_limit_bytes=VMEM_LIMIT_BYTES)


def _tile(dim, target, quantum=LANES):
    if dim <= target:
        return dim
    best = None
    for t in range(quantum, target + 1, quantum):
        if dim % t == 0:
            best = t
    assert best is not None, (dim, target)
    return best


def _rmsnorm_kernel(x_ref, g_ref, o_ref):
    x = x_ref[...]
    y = x * lax.rsqrt(jnp.mean(x * x, axis=-1, keepdims=True) + NORM_EPS)
    o_ref[...] = (y * g_ref[...]).astype(o_ref.dtype)


def _rmsnorm(x, g, out_dtype):
    m, d = x.shape
    tm = _tile(m, 256, SUBLANES)
    return pl.pallas_call(
        _rmsnorm_kernel,
        grid=(m // tm,),
        in_specs=[pl.BlockSpec((tm, d), lambda i: (i, 0)), pl.BlockSpec((1, d), lambda i: (0, 0))],
        out_specs=pl.BlockSpec((tm, d), lambda i: (i, 0)),
        out_shape=jax.ShapeDtypeStruct((m, d), out_dtype),
        compiler_params=_params("parallel"),
        name="rmsnorm",
    )(x, g.reshape(1, d))


def _mm_kernel(*refs, nk, has_res):
    if has_res:
        a_ref, w_ref, r_ref, o_ref = refs[:4]
        scratch = refs[4:]
    else:
        a_ref, w_ref, o_ref = refs[:3]
        r_ref = None
        scratch = refs[3:]
    part = jnp.dot(a_ref[...], w_ref[...], preferred_element_type=F32)
    if nk == 1:
        if has_res:
            part = part + r_ref[...]
        o_ref[...] = part.astype(o_ref.dtype)
        return
    acc_ref = scratch[0]
    k = pl.program_id(2)

    @pl.when(k == 0)
    def _():
        acc_ref[...] = part

    @pl.when(k > 0)
    def _():
        acc_ref[...] += part

    @pl.when(k == nk - 1)
    def _():
        out = acc_ref[...]
        if has_res:
            out = out + r_ref[...]
        o_ref[...] = out.astype(o_ref.dtype)


def _matmul(a, w, *, res=None, out_dtype, n=None, w_layer=None, col_block=None, tm=1024, tn=512, tk=None):
    m, kdim = a.shape
    n = w.shape[-1] if n is None else n
    tm = _tile(m, tm, SUBLANES)
    tn = _tile(n, tn)
    tk = kdim if tk is None else _tile(kdim, tk)
    nk = kdim // tk
    col = (lambda j: j) if col_block is None else functools.partial(col_block, tn=tn)
    if w.ndim == 3:
        w_spec = pl.BlockSpec((None, tk, tn), lambda i, j, k: (w_layer, k, col(j)))
    else:
        w_spec = pl.BlockSpec((tk, tn), lambda i, j, k: (k, col(j)))
    in_specs = [pl.BlockSpec((tm, tk), lambda i, j, k: (i, k)), w_spec]
    args = [a, w]
    if res is not None:
        in_specs.append(pl.BlockSpec((tm, tn), lambda i, j, k: (i, j)))
        args.append(res)
    return pl.pallas_call(
        functools.partial(_mm_kernel, nk=nk, has_res=res is not None),
        grid=(m // tm, n // tn, nk),
        in_specs=in_specs,
        out_specs=pl.BlockSpec((tm, tn), lambda i, j, k: (i, j)),
        out_shape=jax.ShapeDtypeStruct((m, n), out_dtype),
        scratch_shapes=[pltpu.VMEM((tm, tn), F32)] if nk > 1 else [],
        compiler_params=_params("parallel", "parallel", "arbitrary"),
        name="matmul",
    )(*args)


def _rope_tables(seq, head_dim, lanes_per_head):
    rot = head_dim // ROPE_FRACTION
    half = rot // 2
    inv_freq = jnp.power(jnp.float32(ROPE_THETA), -jnp.arange(half, dtype=F32) * (2.0 / rot))
    ang = jnp.arange(seq, dtype=F32)[:, None] * inv_freq[None, :]
    cos, sin = jnp.cos(ang), jnp.sin(ang)
    ones = jnp.ones((seq, lanes_per_head - rot), F32)
    zeros_h = jnp.zeros((seq, half), F32)
    zeros_r = jnp.zeros((seq, lanes_per_head - rot), F32)
    c = jnp.concatenate([cos, cos, ones], axis=1)
    a = jnp.concatenate([-sin, zeros_h, zeros_r], axis=1)
    b = jnp.concatenate([zeros_h, sin, zeros_r], axis=1)
    reps = LANES // lanes_per_head
    return tuple(jnp.tile(t, (1, reps)) for t in (c, a, b)), half


def _rope_apply(x, c, a, b, half):
    up = pltpu.roll(x, LANES - half, axis=1)
    dn = pltpu.roll(x, half, axis=1)
    return x * c + up * a + dn * b


def _rope_kernel(x_ref, c_ref, a_ref, b_ref, o_ref, *, half, n_scaled, scale):
    j = pl.program_id(1)
    s = jnp.where(j < n_scaled, jnp.float32(scale), jnp.float32(1.0))
    c, a, b = c_ref[...], a_ref[...], b_ref[...]
    for blk in range(x_ref.shape[1] // LANES):
        sl = slice(blk * LANES, (blk + 1) * LANES)
        x = x_ref[:, sl].astype(F32)
        o_ref[:, sl] = (_rope_apply(x, c, a, b, half) * s).astype(o_ref.dtype)


def _rope(p, seq, tables, half, *, n_cols, n_scaled_cols, scale, tc=512):
    m = p.shape[0]
    tm = _tile(seq, 512, SUBLANES)
    tc = _tile(math.gcd(n_cols, n_scaled_cols) if n_scaled_cols else n_cols, tc)
    sb = seq // tm
    tab_spec = pl.BlockSpec((tm, LANES), lambda i, j: (i % sb, 0))
    return pl.pallas_call(
        functools.partial(_rope_kernel, half=half, n_scaled=n_scaled_cols // tc, scale=scale),
        grid=(m // tm, n_cols // tc),
        in_specs=[pl.BlockSpec((tm, tc), lambda i, j: (i, j)), tab_spec, tab_spec, tab_spec],
        out_specs=pl.BlockSpec((tm, tc), lambda i, j: (i, j)),
        out_shape=jax.ShapeDtypeStruct((m, n_cols), p.dtype),
        compiler_params=_params("parallel", "parallel"),
        name="rope",
    )(p, *tables)


def _idx_prep_kernel(p_ref, g_ref, beta_ref, c_ref, a_ref, b_ref, ik_ref, iw_ref, *, half, iw_scale):
    x = p_ref[:, :HEAD]
    mu = jnp.mean(x, axis=-1, keepdims=True)
    var = jnp.mean(jnp.square(x - mu), axis=-1, keepdims=True)
    y = (x - mu) * lax.rsqrt(var + LN_EPS)
    y = y * g_ref[...] + beta_ref[...]
    ik_ref[...] = _rope_apply(y, c_ref[...], a_ref[...], b_ref[...], half).astype(ik_ref.dtype)
    iw_ref[...] = p_ref[:, HEAD:] * jnp.float32(iw_scale)


def _idx_prep(p2, seq, ik_g, ik_b, tables, half):
    m = p2.shape[0]
    tm = _tile(seq, 512, SUBLANES)
    sb = seq // tm
    tab_spec = pl.BlockSpec((tm, LANES), lambda i: (i % sb, 0))
    vec_spec = pl.BlockSpec((1, HEAD), lambda i: (0, 0))
    blk = pl.BlockSpec((tm, HEAD), lambda i: (i, 0))
    return pl.pallas_call(
        functools.partial(_idx_prep_kernel, half=half, iw_scale=IDX_HEADS ** -0.5 * HEAD ** -0.5),
        grid=(m // tm,),
        in_specs=[pl.BlockSpec((tm, 2 * HEAD), lambda i: (i, 0)), vec_spec, vec_spec, tab_spec, tab_spec, tab_spec],
        out_specs=[blk, blk],
        out_shape=[jax.ShapeDtypeStruct((m, HEAD), BF16), jax.ShapeDtypeStruct((m, HEAD), F32)],
        compiler_params=_params("parallel"),
        name="idx_prep",
    )(p2, ik_g.reshape(1, HEAD), ik_b.reshape(1, HEAD), *tables)


DSA_TQ = 128
DSA_KC = 512


def _indexer_kernel(iq_ref, ik_ref, iw_ref, mask_ref, key_ref, *, n_sel, n_chunks):
    qb = pl.program_id(1)
    tq, kc = DSA_TQ, DSA_KC
    nch = (qb * tq) // kc + 1
    idx_bits = (n_chunks * kc - 1).bit_length()
    iw = iw_ref[0]
    q_pos = qb * tq + lax.broadcasted_iota(jnp.int32, (tq, 1), 0)
    lane = lax.broadcasted_iota(jnp.int32, (1, kc), 1)

    def score_chunk(c, _):
        ikc = ik_ref[0, pl.ds(pl.multiple_of(c * kc, kc), kc), :]
        score = jnp.zeros((tq, kc), F32)
        for h in range(IDX_HEADS):
            rel = lax.dot_general(iq_ref[0, :, h * HEAD:(h + 1) * HEAD], ikc, (((1,), (1,)), ((), ())),
                                  preferred_element_type=F32)
            score = score + jnp.maximum(rel, 0.0) * iw[:, h:h + 1]
        bits = pltpu.bitcast(score, jnp.int32)
        key = jnp.where(bits < 0, bits ^ jnp.int32(0x7FFFFFFF), bits)
        key = jnp.where(bits == jnp.int32(INT_MIN), jnp.int32(0), key)
        key = jnp.where(c * kc + lane <= q_pos, key, jnp.int32(INT_MIN))
        key_ref[c] = key
        return 0

    lax.fori_loop(0, nch, score_chunk, 0)

    lane128 = lax.broadcasted_iota(jnp.int32, (1, LANES), 1)

    def for_blocks(c, fn):
        key = key_ref[c]
        return [fn(key[:, b * LANES:(b + 1) * LANES], c * kc + b * LANES + lane128) for b in range(kc // LANES)]

    def count(pred):
        def body(c, acc):
            for hit in for_blocks(c, lambda key, pos: jnp.where(pred(key, pos), jnp.int32(1), jnp.int32(0))):
                acc = acc + hit
            return acc
        acc = lax.fori_loop(0, nch, body, jnp.zeros((tq, LANES), jnp.int32))
        return jnp.broadcast_to(jnp.sum(acc, axis=1, keepdims=True), (tq, LANES))

    def write_mask(sel_fn):
        def body(c, _):
            blocks = for_blocks(c, lambda key, pos: jnp.where(sel_fn(key, pos), 1.0, 0.0))
            mask_ref[0, 0, c] = jnp.concatenate(blocks, axis=1).astype(mask_ref.dtype)
            return 0
        lax.fori_loop(0, nch, body, 0)

        def tail(c, _):
            mask_ref[0, 0, c] = jnp.zeros((tq, kc), mask_ref.dtype)
            return 0
        lax.fori_loop(nch, n_chunks, tail, 0)

    cnt0 = count(lambda key, pos: key >= 0)
    t0 = jnp.where(cnt0 >= n_sel, jnp.int32(0), jnp.int32(INT_MIN))

    def bit_step(i, t):
        cand = t | jnp.left_shift(jnp.int32(1), jnp.int32(30) - i)
        cnt = count(lambda key, pos: key >= cand)
        return jnp.where(cnt >= n_sel, cand, t)

    thr = lax.fori_loop(0, 31, bit_step, t0)
    n_ge = count(lambda key, pos: key >= thr)
    real = thr != jnp.int32(INT_MIN)
    excess = jnp.max(jnp.where(jnp.logical_and(real, n_ge > n_sel), 1, 0)) > 0

    @pl.when(jnp.logical_not(excess))
    def _():
        thr_eff = jnp.maximum(thr, jnp.int32(INT_MIN + 1))
        write_mask(lambda key, pos: key >= thr_eff)

    @pl.when(excess)
    def _():
        need = n_sel - count(lambda key, pos: key > thr)

        def idx_step(i, j):
            cand = j | jnp.left_shift(jnp.int32(1), jnp.int32(idx_bits - 1) - i)
            cnt = count(lambda key, pos: jnp.logical_and(key == thr, pos < cand))
            return jnp.where(cnt < need, cand, j)

        last = lax.fori_loop(0, idx_bits, idx_step, jnp.zeros((tq, LANES), jnp.int32))
        write_mask(lambda key, pos: jnp.logical_or(
            key > thr, jnp.logical_and(jnp.logical_and(key == thr, pos <= last), real)))


def _indexer(iq, ik, iw, n_sel):
    bsz, seq = ik.shape[0], ik.shape[1]
    n_chunks = seq // DSA_KC
    nq = seq // DSA_TQ
    iq_arr, iq_blk = iq
    return pl.pallas_call(
        functools.partial(_indexer_kernel, n_sel=n_sel, n_chunks=n_chunks),
        grid=(bsz, nq),
        in_specs=[pl.BlockSpec((1, DSA_TQ, IDX_HEADS * HEAD), lambda b, q: (b, q, iq_blk)),
                  pl.BlockSpec((1, seq, HEAD), lambda b, q: (b, 0, 0)),
                  pl.BlockSpec((1, DSA_TQ, HEAD), lambda b, q: (b, q, 0))],
        out_specs=pl.BlockSpec((1, 1, n_chunks, DSA_TQ, DSA_KC), lambda b, q: (b, q, 0, 0, 0)),
        out_shape=jax.ShapeDtypeStruct((bsz, nq, n_chunks, DSA_TQ, DSA_KC), BF16),
        scratch_shapes=[pltpu.VMEM((n_chunks, DSA_TQ, DSA_KC), jnp.int32)],
        compiler_params=_params("parallel", "arbitrary"),
        name="dsa_indexer",
    )(iq_arr, ik, iw)


def _lane_blocks(x):
    return [x[:, j * LANES:(j + 1) * LANES] for j in range(x.shape[1] // LANES)]


STREAMS = 2


def _two_sweep_attention(nch, rows, logits_fn, pv_fn, s_ref, mx_ref, ls_ref, acc_ref):
    mx_ref[...] = jnp.full(mx_ref.shape, -jnp.inf, F32)

    def sweep1_chunk(c, last):
        for i, logits2 in enumerate(logits_fn(c, last)):
            sl = slice(i * rows, (i + 1) * rows)
            s_ref[c, sl, :] = logits2
            mx = mx_ref[sl, :]
            for blk in _lane_blocks(logits2):
                mx = jnp.maximum(mx, blk)
            mx_ref[sl, :] = mx

    def sweep1(c, _):
        sweep1_chunk(c, False)
        return 0

    lax.fori_loop(0, nch - 1, sweep1, 0)
    sweep1_chunk(nch - 1, True)

    row_max = jnp.max(mx_ref[...], axis=1, keepdims=True)
    mx_ref[...] = jnp.broadcast_to(row_max, mx_ref.shape)
    ls_ref[...] = jnp.zeros(ls_ref.shape, F32)
    acc_ref[...] = jnp.zeros(acc_ref.shape, F32)

    def sweep2(c, _):
        for i in range(STREAMS):
            sl = slice(i * rows, (i + 1) * rows)
            m = mx_ref[sl, :]
            ps = [jnp.exp2(blk - m) for blk in _lane_blocks(s_ref[c, sl, :])]
            ls_ref[sl, :] += sum(ps[1:], ps[0])
            acc_ref[sl, :] += pv_fn(c, i, jnp.concatenate(ps, axis=1).astype(BF16))
        return 0

    lax.fori_loop(0, nch, sweep2, 0)
    return acc_ref[...] / jnp.sum(ls_ref[...], axis=1, keepdims=True)


def _softmax_scratch(rows, n_chunks, kc):
    return [pltpu.VMEM((n_chunks, rows, kc), F32),
            pltpu.VMEM((rows, LANES), F32),
            pltpu.VMEM((rows, LANES), F32),
            pltpu.VMEM((rows, HEAD), F32)]


def _dsa_attn_kernel(q_ref, k_ref, v_ref, mask_ref, o_ref, s_ref, mx_ref, ls_ref, acc_ref, *, groups):
    qb = pl.program_id(2)
    tq, kc = DSA_TQ, DSA_KC
    nch = (qb * tq) // kc + 1
    rows = groups * tq
    qs = [jnp.concatenate([q_ref[0, :, (i * groups + r) * HEAD:(i * groups + r + 1) * HEAD] for r in range(groups)],
                          axis=0) for i in range(STREAMS)]

    def logits_fn(c, last):
        start = pl.multiple_of(c * kc, kc)
        sel = (mask_ref[0, 0, c].astype(F32) > 0.5)[None]
        out = []
        for i in range(STREAMS):
            k_c = k_ref[0, pl.ds(start, kc), i * HEAD:(i + 1) * HEAD]
            raw = lax.dot_general(qs[i], k_c, (((1,), (1,)), ((), ())), preferred_element_type=F32)
            out.append(jnp.where(sel, raw.reshape(groups, tq, kc), NEG_INF).reshape(rows, kc))
        return out

    def pv_fn(c, i, p):
        v_c = v_ref[0, pl.ds(pl.multiple_of(c * kc, kc), kc), i * HEAD:(i + 1) * HEAD]
        return jnp.dot(p, v_c, preferred_element_type=F32)

    out = _two_sweep_attention(nch, rows, logits_fn, pv_fn, s_ref, mx_ref, ls_ref, acc_ref)
    for r in range(STREAMS * groups):
        o_ref[0, :, r * HEAD:(r + 1) * HEAD] = out[r * tq:(r + 1) * tq].astype(o_ref.dtype)


def _dsa_attention(q, k, v, mask, bsz, seq):
    groups = A_HEADS // A_KV_HEADS
    (q_arr, q0), (k_arr, k0), (v_arr, v0) = q, k, v
    nq = seq // DSA_TQ
    n_chunks = seq // DSA_KC
    qw = STREAMS * groups * HEAD
    kw = STREAMS * HEAD
    assert q0 % (STREAMS * groups) == 0 and k0 % STREAMS == 0 and v0 % STREAMS == 0
    return pl.pallas_call(
        functools.partial(_dsa_attn_kernel, groups=groups),
        grid=(bsz, A_KV_HEADS // STREAMS, nq),
        in_specs=[pl.BlockSpec((1, DSA_TQ, qw), lambda b, g, i: (b, i, q0 // (STREAMS * groups) + g)),
                  pl.BlockSpec((1, seq, kw), lambda b, g, i: (b, 0, k0 // STREAMS + g)),
                  pl.BlockSpec((1, seq, kw), lambda b, g, i: (b, 0, v0 // STREAMS + g)),
                  pl.BlockSpec((1, 1, n_chunks, DSA_TQ, DSA_KC), lambda b, g, i: (b, i, 0, 0, 0))],
        out_specs=pl.BlockSpec((1, DSA_TQ, qw), lambda b, g, i: (b, i, g)),
        out_shape=jax.ShapeDtypeStruct((bsz, seq, A_HEADS * HEAD), BF16),
        scratch_shapes=_softmax_scratch(STREAMS * groups * DSA_TQ, n_chunks, DSA_KC),
        compiler_params=_params("parallel", "parallel", "arbitrary"),
        name="dsa_attention",
    )(q_arr, k_arr, v_arr, mask)


def _fox_cum_kernel(f_ref, bias_ref, cum_t_ref):
    z = f_ref[0] + bias_ref[...]
    x = jnp.minimum(z, 0.0) - jnp.log1p(jnp.exp(-jnp.abs(z)))
    seq = x.shape[0]
    row = lax.broadcasted_iota(jnp.int32, (seq, 1), 0)
    d = 1
    while d < seq:
        x = x + jnp.where(row >= d, pltpu.roll(x, d, axis=0), 0.0)
        d *= 2
    cum_t_ref[0] = (x * jnp.float32(LOG2E)).T


def _fox_cum(f, bias):
    bsz, seq, _ = f.shape
    return pl.pallas_call(
        _fox_cum_kernel,
        grid=(bsz,),
        in_specs=[pl.BlockSpec((1, seq, HEAD), lambda b: (b, 0, 0)), pl.BlockSpec((1, HEAD), lambda b: (0, 0))],
        out_specs=pl.BlockSpec((1, HEAD, seq), lambda b: (b, 0, 0)),
        out_shape=jax.ShapeDtypeStruct((bsz, HEAD, seq), F32),
        compiler_params=_params("parallel"),
        name="fox_cum",
    )(f, bias)


FOX_TQ = 512
FOX_KC = 512


def _fox_attn_kernel(q_ref, k_ref, v_ref, cq_ref, ck_ref, o_ref, s_ref, mx_ref, ls_ref, acc_ref, *, scale2):
    qb = pl.program_id(2)
    tq, kc = FOX_TQ, FOX_KC
    heads = [slice(i * HEAD, (i + 1) * HEAD) for i in range(STREAMS)]
    qs = [(q_ref[0, :, h].astype(F32) * jnp.float32(scale2)).astype(q_ref.dtype) for h in heads]
    cqs = [jnp.broadcast_to(cq_ref[0, i, 0], (LANES, tq)).T for i in range(STREAMS)]

    def logits_fn(c, last):
        start = pl.multiple_of(c * kc, kc)
        out = []
        for i in range(STREAMS):
            raw = lax.dot_general(qs[i], k_ref[0, pl.ds(start, kc), heads[i]], (((1,), (1,)), ((), ())),
                                  preferred_element_type=F32)
            ck = ck_ref[0, i, c]
            blocks = [blk + cqs[i] - ck[:, j * LANES:(j + 1) * LANES] for j, blk in enumerate(_lane_blocks(raw))]
            logits2 = jnp.concatenate(blocks, axis=1)
            if last:
                row = lax.broadcasted_iota(jnp.int32, (tq, 1), 0)
                lane = lax.broadcasted_iota(jnp.int32, (1, kc), 1)
                logits2 = jnp.where(lane <= row, logits2, NEG_INF)
            out.append(logits2)
        return out

    def pv_fn(c, i, p):
        v_c = v_ref[0, pl.ds(pl.multiple_of(c * kc, kc), kc), heads[i]]
        return jnp.dot(p, v_c, preferred_element_type=F32)

    out = _two_sweep_attention(qb + 1, tq, logits_fn, pv_fn, s_ref, mx_ref, ls_ref, acc_ref)
    for i in range(STREAMS):
        o_ref[0, :, heads[i]] = out[i * tq:(i + 1) * tq].astype(o_ref.dtype)


def _fox_attention(p, cum_rows, bsz, seq):
    tq = FOX_TQ
    assert FOX_TQ == FOX_KC and seq % tq == 0
    n_chunks = seq // FOX_KC
    hw = STREAMS * HEAD
    cq_rows = cum_rows.reshape(bsz, B_HEADS, 1, 1, seq)
    return pl.pallas_call(
        functools.partial(_fox_attn_kernel, scale2=HEAD ** -0.5 * LOG2E),
        grid=(bsz, B_HEADS // STREAMS, seq // tq),
        in_specs=[pl.BlockSpec((1, tq, hw), lambda b, h, i: (b, i, h)),
                  pl.BlockSpec((1, seq, hw), lambda b, h, i: (b, 0, B_HEADS // STREAMS + h)),
                  pl.BlockSpec((1, seq, hw), lambda b, h, i: (b, 0, 2 * B_HEADS // STREAMS + h)),
                  pl.BlockSpec((1, STREAMS, 1, 1, tq), lambda b, h, i: (b, h, 0, 0, i)),
                  pl.BlockSpec((1, STREAMS, n_chunks, 1, FOX_KC), lambda b, h, i: (b, h, 0, 0, 0))],
        out_specs=pl.BlockSpec((1, tq, hw), lambda b, h, i: (b, i, h)),
        out_shape=jax.ShapeDtypeStruct((bsz, seq, B_HEADS * HEAD), BF16),
        scratch_shapes=_softmax_scratch(STREAMS * tq, n_chunks, FOX_KC),
        compiler_params=_params("parallel", "parallel", "arbitrary"),
        name="fox_attention",
    )(p, p, p, cq_rows, cum_rows)


def _swa_kernel(sinks_ref, q_ref, kp_ref, kc_ref, vp_ref, vc_ref, o_ref):
    n = pl.program_id(1)
    pair = pl.program_id(2)
    w, dh = WINDOW, C_HEAD_DIM
    groups = C_HEADS // C_KV_HEADS
    kv_per_blk = LANES // dh
    i = lax.broadcasted_iota(jnp.int32, (w, 1), 0)
    j = lax.broadcasted_iota(jnp.int32, (1, 2 * w), 1)
    valid = jnp.logical_and(jnp.logical_and(j > i, j <= i + w), n * w + j >= w)
    kk = jnp.concatenate([kp_ref[0], kc_ref[0]], axis=0)
    vv = jnp.concatenate([vp_ref[0], vc_ref[0]], axis=0)
    for g in range(kv_per_blk):
        k_g = kk[:, g * dh:(g + 1) * dh]
        v_g = vv[:, g * dh:(g + 1) * dh]
        for r in range(groups):
            col = (g * groups + r) * dh
            q = q_ref[0, :, col:col + dh]
            s = lax.dot_general(q, k_g, (((1,), (1,)), ((), ())), preferred_element_type=F32)
            logits = jnp.where(valid, s, NEG_INF)
            sink = sinks_ref[(pair * kv_per_blk + g) * groups + r]
            m = jnp.maximum(jnp.max(logits, axis=1, keepdims=True), sink)
            p = jnp.exp(logits - m)
            denom = jnp.sum(p, axis=1, keepdims=True) + jnp.exp(sink - m)
            o = jnp.dot((p / denom).astype(v_g.dtype), v_g, preferred_element_type=F32)
            o_ref[0, :, col:col + dh] = o.astype(o_ref.dtype)


def _swa_attention(pr, p, sinks, bsz, seq):
    w = WINDOW
    nb = seq // w
    kv_per_blk = LANES // C_HEAD_DIM
    pairs = C_KV_HEADS // kv_per_blk
    qw = kv_per_blk * (C_HEADS // C_KV_HEADS) * C_HEAD_DIM
    k0 = C_HEADS * C_HEAD_DIM // LANES
    v0 = k0 + C_KV_HEADS * C_HEAD_DIM // LANES
    prev = lambda col0: (lambda b, n, g: (b, jnp.maximum(n - 1, 0), col0 + g))
    cur = lambda col0: (lambda b, n, g: (b, n, col0 + g))
    blk = (1, w, LANES)
    return pl.pallas_call(
        _swa_kernel,
        grid=(bsz, nb, pairs),
        in_specs=[pl.BlockSpec(memory_space=pltpu.SMEM),
                  pl.BlockSpec((1, w, qw), lambda b, n, g: (b, n, g)),
                  pl.BlockSpec(blk, prev(k0)), pl.BlockSpec(blk, cur(k0)),
                  pl.BlockSpec(blk, prev(v0)), pl.BlockSpec(blk, cur(v0))],
        out_specs=pl.BlockSpec((1, w, qw), lambda b, n, g: (b, n, g)),
        out_shape=jax.ShapeDtypeStruct((bsz, seq, C_HEADS * C_HEAD_DIM), BF16),
        compiler_params=_params("parallel", "parallel", "parallel"),
        name="swa_attention",
    )(sinks, pr, pr, pr, p, p)


FFN_SUB_BLOCKS = 4


def _ffn_up_kernel(h_ref, wg_ref, wv_ref, cwg_ref, cwv_ref, cbg_ref, cbv_ref, o_ref, w_scr, u_scr, carry,
                   *, tiles_per_seq):
    mi = pl.program_id(0)
    ni = pl.program_id(1)
    tm = h_ref.shape[0]
    tn = wg_ref.shape[1]
    w_scr[:, :tn] = wg_ref[...].astype(w_scr.dtype)
    w_scr[:, tn:] = wv_ref[...].astype(w_scr.dtype)

    @pl.when((mi % tiles_per_seq) == 0)
    def _():
        carry[ni] = jnp.zeros(carry.shape[1:], F32)

    u_scr[:SUBLANES, :] = carry[ni]
    sub = tm // FFN_SUB_BLOCKS

    def matmul(s):
        rows = slice(s * sub, (s + 1) * sub)
        u_scr[SUBLANES + s * sub:SUBLANES + (s + 1) * sub, :] = jnp.dot(h_ref[rows, :], w_scr[...],
                                                                        preferred_element_type=F32)

    def conv(s, cols, cw_ref, cb_ref):
        taps = [u_scr[pl.ds(s * sub + SUBLANES - (CONV_WIDTH - 1) + tap, sub), cols] * cw_ref[tap:tap + 1, :]
                for tap in range(CONV_WIDTH)]
        return cb_ref[...] + sum(taps[1:], taps[0])

    def epilogue(s):
        gate = conv(s, slice(0, tn), cwg_ref, cbg_ref)
        val = conv(s, slice(tn, 2 * tn), cwv_ref, cbv_ref)
        o_ref[s * sub:(s + 1) * sub, :] = (gate * jax.nn.sigmoid(gate) * val).astype(o_ref.dtype)

    matmul(0)
    for s in range(1, FFN_SUB_BLOCKS):
        matmul(s)
        epilogue(s - 1)
    epilogue(FFN_SUB_BLOCKS - 1)
    carry[ni] = u_scr[tm:, :]


def _ffn_up(h, w_up, conv_w, conv_b, layer, seq, *, tm=2048, tn=256):
    m, d = h.shape
    dff = w_up.shape[2] // 2
    tm = _tile(seq, tm, SUBLANES)
    tn = _tile(dff, tn)
    nt = dff // tn
    assert CONV_WIDTH - 1 <= SUBLANES and tm % (FFN_SUB_BLOCKS * SUBLANES) == 0
    conv_b = conv_b.reshape(conv_b.shape[0], 1, 2 * dff)
    return pl.pallas_call(
        functools.partial(_ffn_up_kernel, tiles_per_seq=seq // tm),
        grid=(m // tm, nt),
        in_specs=[pl.BlockSpec((tm, d), lambda i, j: (i, 0), pipeline_mode=pl.Buffered(1)),
                  pl.BlockSpec((None, d, tn), lambda i, j: (layer, 0, j)),
                  pl.BlockSpec((None, d, tn), lambda i, j: (layer, 0, j + nt)),
                  pl.BlockSpec((None, CONV_WIDTH, tn), lambda i, j: (layer, 0, j)),
                  pl.BlockSpec((None, CONV_WIDTH, tn), lambda i, j: (layer, 0, j + nt)),
                  pl.BlockSpec((None, 1, tn), lambda i, j: (layer, 0, j)),
                  pl.BlockSpec((None, 1, tn), lambda i, j: (layer, 0, j + nt))],
        out_specs=pl.BlockSpec((tm, tn), lambda i, j: (i, j)),
        out_shape=jax.ShapeDtypeStruct((m, dff), BF16),
        scratch_shapes=[pltpu.VMEM((d, 2 * tn), BF16),
                        pltpu.VMEM((SUBLANES + tm, 2 * tn), F32),
                        pltpu.VMEM((nt, SUBLANES, 2 * tn), F32)],
        compiler_params=_params("arbitrary", "arbitrary"),
        name="ffn_up",
    )(h, w_up, w_up, conv_w, conv_w, conv_b, conv_b)


def _pad_cols(w, n):
    return jnp.pad(w, ((0, 0), (0, 0), (0, n - w.shape[2])))


def _dsa_weight_col(j, *, tn):
    nq, niq, nkv2 = A_HEADS * HEAD // tn, IDX_HEADS * HEAD // tn, 2 * A_KV_HEADS * HEAD // tn
    return jnp.where(j < nq, j, jnp.where(j < nq + niq, j + nkv2, j - niq))


def _dsa_mixer(h, x, w_in, w_small, ik_g, ik_b, w_out, slot, bsz, seq):
    a_q, a_kv, a_iq = A_HEADS * HEAD, A_KV_HEADS * HEAD, IDX_HEADS * HEAD
    p = _matmul(h, w_in, n=a_q + a_iq + 2 * a_kv, w_layer=slot, col_block=_dsa_weight_col, out_dtype=BF16)
    p2 = _matmul(h, w_small, w_layer=slot, out_dtype=F32)
    tables, half = _rope_tables(seq, HEAD, HEAD)
    n_rope = a_q + a_iq + a_kv
    pr = _rope(p, seq, tables, half, n_cols=n_rope, n_scaled_cols=a_q, scale=HEAD ** -0.5 * LOG2E)
    ik, iw = _idx_prep(p2, seq, ik_g, ik_b, tables, half)
    pr3 = pr.reshape(bsz, seq, n_rope)
    p3 = p.reshape(bsz, seq, p.shape[1])
    n_sel = min(INDEX_TOPK, seq // 4)
    mask = _indexer((pr3, a_q // a_iq), ik.reshape(bsz, seq, HEAD), iw.reshape(bsz, seq, HEAD), n_sel)
    o = _dsa_attention((pr3, 0), (pr3, (a_q + a_iq) // HEAD), (p3, n_rope // HEAD), mask, bsz, seq)
    return _matmul(o.reshape(bsz * seq, a_q), w_out, w_layer=slot, res=x, out_dtype=F32)


def _fox_mixer(h, x, w_in, w_f, f_bias, w_out, slot, bsz, seq):
    d = h.shape[1]
    p = _matmul(h, w_in, n=3 * d, w_layer=slot, out_dtype=BF16)
    f = _matmul(h, w_f, w_layer=slot, out_dtype=F32, tn=HEAD)
    bias = jnp.pad(f_bias.astype(F32), (0, HEAD - B_HEADS)).reshape(1, HEAD)
    cum_t = _fox_cum(f.reshape(bsz, seq, HEAD), bias)
    cum_rows = cum_t[:, :B_HEADS].reshape(bsz, B_HEADS, seq // FOX_KC, 1, FOX_KC)
    o = _fox_attention(p.reshape(bsz, seq, 3 * d), cum_rows, bsz, seq)
    return _matmul(o.reshape(bsz * seq, d), w_out, w_layer=slot, res=x, out_dtype=F32)


def _swa_mixer(h, x, w_in, sinks, w_out, slot, bsz, seq):
    c_q, c_kv = C_HEADS * C_HEAD_DIM, C_KV_HEADS * C_HEAD_DIM
    p = _matmul(h, w_in, w_layer=slot, out_dtype=BF16)
    tables, half = _rope_tables(seq, C_HEAD_DIM, C_HEAD_DIM)
    pr = _rope(p, seq, tables, half, n_cols=c_q + c_kv, n_scaled_cols=c_q, scale=C_HEAD_DIM ** -0.5)
    o = _swa_attention(pr.reshape(bsz, seq, c_q + c_kv), p.reshape(bsz, seq, p.shape[1]),
                       sinks.astype(F32), bsz, seq)
    return _matmul(o.reshape(bsz * seq, c_q), w_out, w_layer=slot, res=x, out_dtype=F32)


def _conv_glu_ffn(h, x, w_up, conv_w, conv_b, w_down, layer, seq):
    g = _ffn_up(h, w_up, conv_w, conv_b, layer, seq)
    dff = g.shape[1]
    return _matmul(g, w_down, w_layer=layer, res=x, out_dtype=F32, tm=512, tn=1024, tk=dff // 2)


def kernel(x, attn_norm, ffn_norm, final_norm, a_w_in, a_idx_k_norm_g, a_idx_k_norm_b, a_w_out,
           b_w_in, b_f_bias, b_w_out, c_w_in, c_sinks, c_w_out,
           ffn_w_up, ffn_conv_w, ffn_conv_b, ffn_w_down):
    bsz, seq, d = x.shape
    depth = attn_norm.shape[0]
    a_main = A_HEADS * HEAD + 2 * A_KV_HEADS * HEAD + IDX_HEADS * HEAD
    a_w_small = _pad_cols(a_w_in[:, :, a_main:], 2 * HEAD).astype(BF16)
    b_w_f = _pad_cols(b_w_in[:, :, 3 * d:], HEAD).astype(BF16)
    a_w_in, a_w_out, b_w_in, b_w_out, c_w_in, c_w_out, ffn_w_down = (
        w.astype(BF16) for w in (a_w_in, a_w_out, b_w_in, b_w_out, c_w_in, c_w_out, ffn_w_down))
    x = x.reshape(bsz * seq, d)
    for layer in range(depth):
        mixer, slot = layer % N_MIXERS, layer // N_MIXERS
        h = _rmsnorm(x, attn_norm[layer], BF16)
        if mixer == 0:
            x = _dsa_mixer(h, x, a_w_in, a_w_small, a_idx_k_norm_g[slot], a_idx_k_norm_b[slot], a_w_out, slot,
                           bsz, seq)
        elif mixer == 1:
            x = _fox_mixer(h, x, b_w_in, b_w_f, b_f_bias[slot], b_w_out, slot, bsz, seq)
        else:
            x = _swa_mixer(h, x, c_w_in, c_sinks[slot], c_w_out, slot, bsz, seq)
        h = _rmsnorm(x, ffn_norm[layer], BF16)
        x = _conv_glu_ffn(h, x, ffn_w_up, ffn_conv_w, ffn_conv_b, ffn_w_down, layer, seq)
    return _rmsnorm(x, final_norm, F32).reshape(bsz, seq, d)
```

```python
import functools
import math

import jax
import jax.numpy as jnp
import numpy as np
from jax import lax
from jax.experimental import pallas as pl
from jax.experimental.pallas import tpu as pltpu

F32 = jnp.float32
BF16 = jnp.bfloat16

N_MIXERS = 3
ROPE_THETA = 500000.0
ROPE_FRACTION = 4
NORM_EPS = 1e-6
LN_EPS = 1e-6
NEG_INF = -1e30
LOG2E = math.log2(math.e)
HEAD = 128
A_HEADS, A_KV_HEADS, IDX_HEADS, INDEX_TOPK = 32, 8, 16, 256
B_HEADS = 32
C_HEADS, C_KV_HEADS, C_HEAD_DIM, WINDOW = 64, 8, 64, 128
CONV_WIDTH = 3

LANES = 128
SUBLANES = 8
VMEM_LIMIT_BYTES = 56 * 1024 * 1024
INT_MIN = -(2 ** 31)


def _params(*sem):
    return pltpu.CompilerParams(dimension_semantics=sem, vmem_limit_bytes=VMEM_LIMIT_BYTES)


def _tile(dim, target, quantum=LANES):
    if dim <= target:
        return dim
    best = None
    for t in range(quantum, target + 1, quantum):
        if dim % t == 0:
            best = t
    assert best is not None, (dim, target)
    return best


def _rmsnorm_kernel(x_ref, g_ref, o_ref):
    x = x_ref[...]
    y = x * lax.rsqrt(jnp.mean(x * x, axis=-1, keepdims=True) + NORM_EPS)
    o_ref[...] = (y * g_ref[...]).astype(o_ref.dtype)


def _rmsnorm(x, g, out_dtype):
    m, d = x.shape
    tm = _tile(m, 256, SUBLANES)
    return pl.pallas_call(
        _rmsnorm_kernel,
        grid=(m // tm,),
        in_specs=[pl.BlockSpec((tm, d), lambda i: (i, 0)), pl.BlockSpec((1, d), lambda i: (0, 0))],
        out_specs=pl.BlockSpec((tm, d), lambda i: (i, 0)),
        out_shape=jax.ShapeDtypeStruct((m, d), out_dtype),
        compiler_params=_params("parallel"),
        name="rmsnorm",
    )(x, g.reshape(1, d))


def _mm_kernel(*refs, nk, has_res):
    if has_res:
        a_ref, w_ref, r_ref, o_ref = refs[:4]
        scratch = refs[4:]
    else:
        a_ref, w_ref, o_ref = refs[:3]
        r_ref = None
        scratch = refs[3:]
    part = jnp.dot(a_ref[...], w_ref[...], preferred_element_type=F32)
    if nk == 1:
        if has_res:
            part = part + r_ref[...]
        o_ref[...] = part.astype(o_ref.dtype)
        return
    acc_ref = scratch[0]
    k = pl.program_id(2)

    @pl.when(k == 0)
    def _():
        acc_ref[...] = part

    @pl.when(k > 0)
    def _():
        acc_ref[...] += part

    @pl.when(k == nk - 1)
    def _():
        out = acc_ref[...]
        if has_res:
            out = out + r_ref[...]
        o_ref[...] = out.astype(o_ref.dtype)


def _matmul(a, w, *, res=None, out_dtype, n=None, w_layer=None, col_block=None, tm=1024, tn=512, tk=None):
    m, kdim = a.shape
    n = w.shape[-1] if n is None else n
    tm = _tile(m, tm, SUBLANES)
    tn = _tile(n, tn)
    tk = kdim if tk is None else _tile(kdim, tk)
    nk = kdim // tk
    col = (lambda j: j) if col_block is None else functools.partial(col_block, tn=tn)
    if w.ndim == 3:
        w_spec = pl.BlockSpec((None, tk, tn), lambda i, j, k: (w_layer, k, col(j)))
    else:
        w_spec = pl.BlockSpec((tk, tn), lambda i, j, k: (k, col(j)))
    in_specs = [pl.BlockSpec((tm, tk), lambda i, j, k: (i, k)), w_spec]
    args = [a, w]
    if res is not None:
        in_specs.append(pl.BlockSpec((tm, tn), lambda i, j, k: (i, j)))
        args.append(res)
    return pl.pallas_call(
        functools.partial(_mm_kernel, nk=nk, has_res=res is not None),
        grid=(m // tm, n // tn, nk),
        in_specs=in_specs,
        out_specs=pl.BlockSpec((tm, tn), lambda i, j, k: (i, j)),
        out_shape=jax.ShapeDtypeStruct((m, n), out_dtype),
        scratch_shapes=[pltpu.VMEM((tm, tn), F32)] if nk > 1 else [],
        compiler_params=_params("parallel", "parallel", "arbitrary"),
        name="matmul",
    )(*args)


def _rope_tables(seq, head_dim, lanes_per_head):
    rot = head_dim // ROPE_FRACTION
    half = rot // 2
    inv_freq = jnp.power(jnp.float32(ROPE_THETA), -jnp.arange(half, dtype=F32) * (2.0 / rot))
    ang = jnp.arange(seq, dtype=F32)[:, None] * inv_freq[None, :]
    cos, sin = jnp.cos(ang), jnp.sin(ang)
    ones = jnp.ones((seq, lanes_per_head - rot), F32)
    zeros_h = jnp.zeros((seq, half), F32)
    zeros_r = jnp.zeros((seq, lanes_per_head - rot), F32)
    c = jnp.concatenate([cos, cos, ones], axis=1)
    a = jnp.concatenate([-sin, zeros_h, zeros_r], axis=1)
    b = jnp.concatenate([zeros_h, sin, zeros_r], axis=1)
    reps = LANES // lanes_per_head
    return tuple(jnp.tile(t, (1, reps)) for t in (c, a, b)), half


def _rope_apply(x, c, a, b, half):
    up = pltpu.roll(x, LANES - half, axis=1)
    dn = pltpu.roll(x, half, axis=1)
    return x * c + up * a + dn * b


def _rope_kernel(x_ref, c_ref, ab_ref, swap_ref, o_ref, *, n_scaled, scale):
    j = pl.program_id(1)
    s = jnp.where(j < n_scaled, jnp.float32(scale), jnp.float32(1.0))
    c, ab, swap = c_ref[...], ab_ref[...], swap_ref[...]
    for blk in range(x_ref.shape[1] // LANES):
        sl = slice(blk * LANES, (blk + 1) * LANES)
        x = x_ref[:, sl]
        partner = jnp.dot(x, swap, preferred_element_type=F32)
        o_ref[:, sl] = ((x.astype(F32) * c + partner * ab) * s).astype(o_ref.dtype)


def _rope(p, seq, tables, half, lanes_per_head, *, n_cols, n_scaled_cols, scale, tc=512):
    m = p.shape[0]
    tm = _tile(seq, 512, SUBLANES)
    tc = _tile(math.gcd(n_cols, n_scaled_cols) if n_scaled_cols else n_cols, tc)
    sb = seq // tm
    c, a, b = tables
    swap = np.zeros((LANES, LANES), np.float32)
    for head in range(0, LANES, lanes_per_head):
        for r in range(half):
            swap[head + r + half, head + r] = 1.0
            swap[head + r, head + r + half] = 1.0
    tab_spec = pl.BlockSpec((tm, LANES), lambda i, j: (i % sb, 0))
    return pl.pallas_call(
        functools.partial(_rope_kernel, n_scaled=n_scaled_cols // tc, scale=scale),
        grid=(m // tm, n_cols // tc),
        in_specs=[pl.BlockSpec((tm, tc), lambda i, j: (i, j)), tab_spec, tab_spec,
                  pl.BlockSpec((LANES, LANES), lambda i, j: (0, 0))],
        out_specs=pl.BlockSpec((tm, tc), lambda i, j: (i, j)),
        out_shape=jax.ShapeDtypeStruct((m, n_cols), p.dtype),
        compiler_params=_params("parallel", "parallel"),
        name="rope",
    )(p, c, a + b, jnp.asarray(swap, p.dtype))


def _idx_prep_kernel(p_ref, g_ref, beta_ref, c_ref, a_ref, b_ref, ik_ref, iw_ref, *, half, iw_scale):
    x = p_ref[:, :HEAD]
    mu = jnp.mean(x, axis=-1, keepdims=True)
    var = jnp.mean(jnp.square(x - mu), axis=-1, keepdims=True)
    y = (x - mu) * lax.rsqrt(var + LN_EPS)
    y = y * g_ref[...] + beta_ref[...]
    ik_ref[...] = _rope_apply(y, c_ref[...], a_ref[...], b_ref[...], half).astype(ik_ref.dtype)
    iw_ref[...] = p_ref[:, HEAD:] * jnp.float32(iw_scale)


def _idx_prep(p2, seq, ik_g, ik_b, tables, half):
    m = p2.shape[0]
    tm = _tile(seq, 512, SUBLANES)
    sb = seq // tm
    tab_spec = pl.BlockSpec((tm, LANES), lambda i: (i % sb, 0))
    vec_spec = pl.BlockSpec((1, HEAD), lambda i: (0, 0))
    blk = pl.BlockSpec((tm, HEAD), lambda i: (i, 0))
    return pl.pallas_call(
        functools.partial(_idx_prep_kernel, half=half, iw_scale=IDX_HEADS ** -0.5 * HEAD ** -0.5),
        grid=(m // tm,),
        in_specs=[pl.BlockSpec((tm, 2 * HEAD), lambda i: (i, 0)), vec_spec, vec_spec, tab_spec, tab_spec, tab_spec],
        out_specs=[blk, blk],
        out_shape=[jax.ShapeDtypeStruct((m, HEAD), BF16), jax.ShapeDtypeStruct((m, HEAD), F32)],
        compiler_params=_params("parallel"),
        name="idx_prep",
    )(p2, ik_g.reshape(1, HEAD), ik_b.reshape(1, HEAD), *tables)


DSA_TQ = 128
DSA_KC = 512
IDX_TQ = 128


def _indexer_kernel(iq_ref, ik_ref, iw_ref, mask_ref, key_ref, *, n_sel, n_chunks):
    qb = pl.program_id(1)
    tq, kc = IDX_TQ, DSA_KC
    nch = (qb * tq + tq - 1) // kc + 1
    idx_bits = (n_chunks * kc - 1).bit_length()
    iw = iw_ref[0]
    q_pos = qb * tq + lax.broadcasted_iota(jnp.int32, (tq, 1), 0)
    lane = lax.broadcasted_iota(jnp.int32, (1, kc), 1)

    def score_chunk(c, _):
        ikc = ik_ref[0, pl.ds(pl.multiple_of(c * kc, kc), kc), :]
        score = jnp.zeros((tq, kc), F32)
        for h in range(IDX_HEADS):
            rel = lax.dot_general(iq_ref[0, :, h * HEAD:(h + 1) * HEAD], ikc, (((1,), (1,)), ((), ())),
                                  preferred_element_type=F32)
            score = score + jnp.maximum(rel, 0.0) * iw[:, h:h + 1]
        bits = pltpu.bitcast(score, jnp.int32)
        key = jnp.where(bits < 0, bits ^ jnp.int32(0x7FFFFFFF), bits)
        key = jnp.where(bits == jnp.int32(INT_MIN), jnp.int32(0), key)
        key = jnp.where(c * kc + lane <= q_pos, key, jnp.int32(INT_MIN))
        key_ref[c] = key
        return 0

    lax.fori_loop(0, nch, score_chunk, 0)

    lane128 = lax.broadcasted_iota(jnp.int32, (1, LANES), 1)

    def for_blocks(c, fn):
        key = key_ref[c]
        return [fn(key[:, b * LANES:(b + 1) * LANES], c * kc + b * LANES + lane128) for b in range(kc // LANES)]

    def count(pred):
        def body(c, acc):
            for hit in for_blocks(c, lambda key, pos: jnp.where(pred(key, pos), jnp.int32(1), jnp.int32(0))):
                acc = acc + hit
            return acc
        acc = lax.fori_loop(0, nch, body, jnp.zeros((tq, LANES), jnp.int32))
        return jnp.broadcast_to(jnp.sum(acc, axis=1, keepdims=True), (tq, LANES))

    def write_mask(sel_fn):
        def store(c, tile):
            for r in range(tq // DSA_TQ):
                mask_ref[0, r, c] = tile[r * DSA_TQ:(r + 1) * DSA_TQ].astype(mask_ref.dtype)

        def body(c, _):
            store(c, jnp.concatenate(for_blocks(c, lambda key, pos: jnp.where(sel_fn(key, pos), 1.0, 0.0)), axis=1))
            return 0
        lax.fori_loop(0, nch, body, 0)

        def tail(c, _):
            store(c, jnp.zeros((tq, kc), F32))
            return 0
        lax.fori_loop(nch, n_chunks, tail, 0)

    cnt0 = count(lambda key, pos: key >= 0)
    t0 = jnp.where(cnt0 >= n_sel, jnp.int32(0), jnp.int32(INT_MIN))

    def bit_step(i, t):
        cand = t | jnp.left_shift(jnp.int32(1), jnp.int32(30) - i)
        cnt = count(lambda key, pos: key >= cand)
        return jnp.where(cnt >= n_sel, cand, t)

    thr = lax.fori_loop(0, 31, bit_step, t0)
    n_ge = count(lambda key, pos: key >= thr)
    real = thr != jnp.int32(INT_MIN)
    excess = jnp.max(jnp.where(jnp.logical_and(real, n_ge > n_sel), 1, 0)) > 0

    @pl.when(jnp.logical_not(excess))
    def _():
        thr_eff = jnp.maximum(thr, jnp.int32(INT_MIN + 1))
        write_mask(lambda key, pos: key >= thr_eff)

    @pl.when(excess)
    def _():
        need = n_sel - count(lambda key, pos: key > thr)

        def idx_step(i, j):
            cand = j | jnp.left_shift(jnp.int32(1), jnp.int32(idx_bits - 1) - i)
            cnt = count(lambda key, pos: jnp.logical_and(key == thr, pos < cand))
            return jnp.where(cnt < need, cand, j)

        last = lax.fori_loop(0, idx_bits, idx_step, jnp.zeros((tq, LANES), jnp.int32))
        write_mask(lambda key, pos: jnp.logical_or(
            key > thr, jnp.logical_and(jnp.logical_and(key == thr, pos <= last), real)))


def _indexer(iq, ik, iw, n_sel):
    bsz, seq = ik.shape[0], ik.shape[1]
    n_chunks = seq // DSA_KC
    nq = seq // DSA_TQ
    per_step = IDX_TQ // DSA_TQ
    assert seq % IDX_TQ == 0 and IDX_TQ % DSA_TQ == 0
    iq_arr, iq_blk = iq
    return pl.pallas_call(
        functools.partial(_indexer_kernel, n_sel=n_sel, n_chunks=n_chunks),
        grid=(bsz, seq // IDX_TQ),
        in_specs=[pl.BlockSpec((1, IDX_TQ, IDX_HEADS * HEAD), lambda b, q: (b, q, iq_blk)),
                  pl.BlockSpec((1, seq, HEAD), lambda b, q: (b, 0, 0)),
                  pl.BlockSpec((1, IDX_TQ, HEAD), lambda b, q: (b, q, 0))],
        out_specs=pl.BlockSpec((1, per_step, n_chunks, DSA_TQ, DSA_KC), lambda b, q: (b, q, 0, 0, 0)),
        out_shape=jax.ShapeDtypeStruct((bsz, nq, n_chunks, DSA_TQ, DSA_KC), BF16),
        scratch_shapes=[pltpu.VMEM((n_chunks, IDX_TQ, DSA_KC), jnp.int32)],
        compiler_params=_params("parallel", "arbitrary"),
        name="dsa_indexer",
    )(iq_arr, ik, iw)


def _lane_blocks(x):
    return [x[:, j * LANES:(j + 1) * LANES] for j in range(x.shape[1] // LANES)]


STREAMS = 2


def _two_sweep_attention(nch, rows, logits_fn, pv_fn, s_ref, mx_ref, ls_ref, acc_ref):
    mx_ref[...] = jnp.full(mx_ref.shape, -jnp.inf, F32)

    def sweep1_chunk(c, last):
        for i, logits2 in enumerate(logits_fn(c, last)):
            sl = slice(i * rows, (i + 1) * rows)
            s_ref[c, sl, :] = logits2
            mx = mx_ref[sl, :]
            for blk in _lane_blocks(logits2):
                mx = jnp.maximum(mx, blk)
            mx_ref[sl, :] = mx

    def pairs_then_rest(n, chunk_fn):
        def pair(c2, _):
            chunk_fn(2 * c2)
            chunk_fn(2 * c2 + 1)
            return 0

        def single(c, _):
            chunk_fn(c)
            return 0

        lax.fori_loop(0, n // 2, pair, 0)
        lax.fori_loop(2 * (n // 2), n, single, 0)

    pairs_then_rest(nch - 1, lambda c: sweep1_chunk(c, False))
    sweep1_chunk(nch - 1, True)

    row_max = jnp.max(mx_ref[...], axis=1, keepdims=True)
    mx_ref[...] = jnp.broadcast_to(row_max, mx_ref.shape)
    ls_ref[...] = jnp.zeros(ls_ref.shape, F32)
    acc_ref[...] = jnp.zeros(acc_ref.shape, F32)

    def sweep2_chunk(c):
        for i in range(STREAMS):
            sl = slice(i * rows, (i + 1) * rows)
            m = mx_ref[sl, :]
            ps = [jnp.exp2(blk - m) for blk in _lane_blocks(s_ref[c, sl, :])]
            ls_ref[sl, :] += sum(ps[1:], ps[0])
            acc_ref[sl, :] += pv_fn(c, i, jnp.concatenate(ps, axis=1).astype(BF16))

    pairs_then_rest(nch, sweep2_chunk)
    return acc_ref[...] / jnp.sum(ls_ref[...], axis=1, keepdims=True)


def _softmax_scratch(rows, n_chunks, kc):
    return [pltpu.VMEM((n_chunks, rows, kc), F32),
            pltpu.VMEM((rows, LANES), F32),
            pltpu.VMEM((rows, LANES), F32),
            pltpu.VMEM((rows, HEAD), F32)]


def _dsa_attn_kernel(q_ref, k_ref, v_ref, mask_ref, o_ref, s_ref, mx_ref, ls_ref, acc_ref, *, groups):
    qb = pl.program_id(2)
    tq, kc = DSA_TQ, DSA_KC
    nch = (qb * tq) // kc + 1
    rows = groups * tq
    qs = [jnp.concatenate([q_ref[0, :, (i * groups + r) * HEAD:(i * groups + r + 1) * HEAD] for r in range(groups)],
                          axis=0) for i in range(STREAMS)]

    def logits_fn(c, last):
        start = pl.multiple_of(c * kc, kc)
        sel = (mask_ref[0, 0, c].astype(F32) > 0.5)[None]
        out = []
        for i in range(STREAMS):
            k_c = k_ref[0, pl.ds(start, kc), i * HEAD:(i + 1) * HEAD]
            raw = lax.dot_general(qs[i], k_c, (((1,), (1,)), ((), ())), preferred_element_type=F32)
            out.append(jnp.where(sel, raw.reshape(groups, tq, kc), NEG_INF).reshape(rows, kc))
        return out

    def pv_fn(c, i, p):
        v_c = v_ref[0, pl.ds(pl.multiple_of(c * kc, kc), kc), i * HEAD:(i + 1) * HEAD]
        return jnp.dot(p, v_c, preferred_element_type=F32)

    out = _two_sweep_attention(nch, rows, logits_fn, pv_fn, s_ref, mx_ref, ls_ref, acc_ref)
    for r in range(STREAMS * groups):
        o_ref[0, :, r * HEAD:(r + 1) * HEAD] = out[r * tq:(r + 1) * tq].astype(o_ref.dtype)


def _dsa_attention(q, k, v, mask, bsz, seq):
    groups = A_HEADS // A_KV_HEADS
    (q_arr, q0), (k_arr, k0), (v_arr, v0) = q, k, v
    nq = seq // DSA_TQ
    n_chunks = seq // DSA_KC
    qw = STREAMS * groups * HEAD
    kw = STREAMS * HEAD
    assert q0 % (STREAMS * groups) == 0 and k0 % STREAMS == 0 and v0 % STREAMS == 0
    return pl.pallas_call(
        functools.partial(_dsa_attn_kernel, groups=groups),
        grid=(bsz, A_KV_HEADS // STREAMS, nq),
        in_specs=[pl.BlockSpec((1, DSA_TQ, qw), lambda b, g, i: (b, i, q0 // (STREAMS * groups) + g)),
                  pl.BlockSpec((1, seq, kw), lambda b, g, i: (b, 0, k0 // STREAMS + g)),
                  pl.BlockSpec((1, seq, kw), lambda b, g, i: (b, 0, v0 // STREAMS + g)),
                  pl.BlockSpec((1, 1, n_chunks, DSA_TQ, DSA_KC), lambda b, g, i: (b, i, 0, 0, 0))],
        out_specs=pl.BlockSpec((1, DSA_TQ, qw), lambda b, g, i: (b, i, g)),
        out_shape=jax.ShapeDtypeStruct((bsz, seq, A_HEADS * HEAD), BF16),
        scratch_shapes=_softmax_scratch(STREAMS * groups * DSA_TQ, n_chunks, DSA_KC),
        compiler_params=_params("parallel", "parallel", "arbitrary"),
        name="dsa_attention",
    )(q_arr, k_arr, v_arr, mask)


def _fox_cum_kernel(f_ref, bias_ref, cum_t_ref):
    z = f_ref[0] + bias_ref[...]
    x = jnp.minimum(z, 0.0) - jnp.log1p(jnp.exp(-jnp.abs(z)))
    seq = x.shape[0]
    row = lax.broadcasted_iota(jnp.int32, (seq, 1), 0)
    d = 1
    while d < seq:
        x = x + jnp.where(row >= d, pltpu.roll(x, d, axis=0), 0.0)
        d *= 2
    cum_t_ref[0] = (x * jnp.float32(LOG2E)).T


def _fox_cum(f, bias):
    bsz, seq, _ = f.shape
    return pl.pallas_call(
        _fox_cum_kernel,
        grid=(bsz,),
        in_specs=[pl.BlockSpec((1, seq, HEAD), lambda b: (b, 0, 0)), pl.BlockSpec((1, HEAD), lambda b: (0, 0))],
        out_specs=pl.BlockSpec((1, HEAD, seq), lambda b: (b, 0, 0)),
        out_shape=jax.ShapeDtypeStruct((bsz, HEAD, seq), F32),
        compiler_params=_params("parallel"),
        name="fox_cum",
    )(f, bias)


FOX_TQ = 512
FOX_KC = 512


def _fox_attn_kernel(q_ref, k_ref, v_ref, cq_ref, ck_ref, o_ref, s_ref, mx_ref, ls_ref, acc_ref, *, scale2):
    qb = pl.program_id(2)
    tq, kc = FOX_TQ, FOX_KC
    heads = [slice(i * HEAD, (i + 1) * HEAD) for i in range(STREAMS)]
    qs = [(q_ref[0, :, h].astype(F32) * jnp.float32(scale2)).astype(q_ref.dtype) for h in heads]
    cqs = [jnp.broadcast_to(cq_ref[0, i, 0], (LANES, tq)).T for i in range(STREAMS)]

    def logits_fn(c, last):
        start = pl.multiple_of(c * kc, kc)
        out = []
        for i in range(STREAMS):
            raw = lax.dot_general(qs[i], k_ref[0, pl.ds(start, kc), heads[i]], (((1,), (1,)), ((), ())),
                                  preferred_element_type=F32)
            ck = ck_ref[0, i, c]
            blocks = [blk + cqs[i] - ck[:, j * LANES:(j + 1) * LANES] for j, blk in enumerate(_lane_blocks(raw))]
            logits2 = jnp.concatenate(blocks, axis=1)
            if last:
                row = lax.broadcasted_iota(jnp.int32, (tq, 1), 0)
                lane = lax.broadcasted_iota(jnp.int32, (1, kc), 1)
                logits2 = jnp.where(lane <= row, logits2, NEG_INF)
            out.append(logits2)
        return out

    def pv_fn(c, i, p):
        v_c = v_ref[0, pl.ds(pl.multiple_of(c * kc, kc), kc), heads[i]]
        return jnp.dot(p, v_c, preferred_element_type=F32)

    out = _two_sweep_attention(qb + 1, tq, logits_fn, pv_fn, s_ref, mx_ref, ls_ref, acc_ref)
    for i in range(STREAMS):
        o_ref[0, :, heads[i]] = out[i * tq:(i + 1) * tq].astype(o_ref.dtype)


def _fox_attention(p, cum_rows, bsz, seq):
    tq = FOX_TQ
    assert FOX_TQ == FOX_KC and seq % tq == 0
    n_chunks = seq // FOX_KC
    hw = STREAMS * HEAD
    cq_rows = cum_rows.reshape(bsz, B_HEADS, 1, 1, seq)
    return pl.pallas_call(
        functools.partial(_fox_attn_kernel, scale2=HEAD ** -0.5 * LOG2E),
        grid=(bsz, B_HEADS // STREAMS, seq // tq),
        in_specs=[pl.BlockSpec((1, tq, hw), lambda b, h, i: (b, i, h)),
                  pl.BlockSpec((1, seq, hw), lambda b, h, i: (b, 0, B_HEADS // STREAMS + h)),
                  pl.BlockSpec((1, seq, hw), lambda b, h, i: (b, 0, 2 * B_HEADS // STREAMS + h)),
                  pl.BlockSpec((1, STREAMS, 1, 1, tq), lambda b, h, i: (b, h, 0, 0, i)),
                  pl.BlockSpec((1, STREAMS, n_chunks, 1, FOX_KC), lambda b, h, i: (b, h, 0, 0, 0))],
        out_specs=pl.BlockSpec((1, tq, hw), lambda b, h, i: (b, i, h)),
        out_shape=jax.ShapeDtypeStruct((bsz, seq, B_HEADS * HEAD), BF16),
        scratch_shapes=_softmax_scratch(STREAMS * tq, n_chunks, FOX_KC),
        compiler_params=_params("parallel", "parallel", "arbitrary"),
        name="fox_attention",
    )(p, p, p, cq_rows, cum_rows)


def _swa_kernel(sinks_ref, q_ref, kp_ref, kc_ref, vp_ref, vc_ref, o_ref):
    n = pl.program_id(1)
    pair = pl.program_id(2)
    w, dh = WINDOW, C_HEAD_DIM
    groups = C_HEADS // C_KV_HEADS
    kv_per_blk = LANES // dh
    i = lax.broadcasted_iota(jnp.int32, (w, 1), 0)
    j = lax.broadcasted_iota(jnp.int32, (1, 2 * w), 1)
    valid = jnp.logical_and(jnp.logical_and(j > i, j <= i + w), n * w + j >= w)
    kk = jnp.concatenate([kp_ref[0], kc_ref[0]], axis=0)
    vv = jnp.concatenate([vp_ref[0], vc_ref[0]], axis=0)
    for g in range(kv_per_blk):
        k_g = kk[:, g * dh:(g + 1) * dh]
        v_g = vv[:, g * dh:(g + 1) * dh]
        for r in range(groups):
            col = (g * groups + r) * dh
            q = q_ref[0, :, col:col + dh]
            s = lax.dot_general(q, k_g, (((1,), (1,)), ((), ())), preferred_element_type=F32)
            logits = jnp.where(valid, s, NEG_INF)
            sink = sinks_ref[(pair * kv_per_blk + g) * groups + r]
            m = jnp.maximum(jnp.max(logits, axis=1, keepdims=True), sink)
            p = jnp.exp(logits - m)
            denom = jnp.sum(p, axis=1, keepdims=True) + jnp.exp(sink - m)
            o = jnp.dot((p / denom).astype(v_g.dtype), v_g, preferred_element_type=F32)
            o_ref[0, :, col:col + dh] = o.astype(o_ref.dtype)


def _swa_attention(pr, p, sinks, bsz, seq):
    w = WINDOW
    nb = seq // w
    kv_per_blk = LANES // C_HEAD_DIM
    pairs = C_KV_HEADS // kv_per_blk
    qw = kv_per_blk * (C_HEADS // C_KV_HEADS) * C_HEAD_DIM
    k0 = C_HEADS * C_HEAD_DIM // LANES
    v0 = k0 + C_KV_HEADS * C_HEAD_DIM // LANES
    prev = lambda col0: (lambda b, n, g: (b, jnp.maximum(n - 1, 0), col0 + g))
    cur = lambda col0: (lambda b, n, g: (b, n, col0 + g))
    blk = (1, w, LANES)
    return pl.pallas_call(
        _swa_kernel,
        grid=(bsz, nb, pairs),
        in_specs=[pl.BlockSpec(memory_space=pltpu.SMEM),
                  pl.BlockSpec((1, w, qw), lambda b, n, g: (b, n, g)),
                  pl.BlockSpec(blk, prev(k0)), pl.BlockSpec(blk, cur(k0)),
                  pl.BlockSpec(blk, prev(v0)), pl.BlockSpec(blk, cur(v0))],
        out_specs=pl.BlockSpec((1, w, qw), lambda b, n, g: (b, n, g)),
        out_shape=jax.ShapeDtypeStruct((bsz, seq, C_HEADS * C_HEAD_DIM), BF16),
        compiler_params=_params("parallel", "parallel", "parallel"),
        name="swa_attention",
    )(sinks, pr, pr, pr, p, p)


FFN_SUB_BLOCKS = 4


def _ffn_up_kernel(h_ref, wg_ref, wv_ref, cwg_ref, cwv_ref, cbg_ref, cbv_ref, o_ref, w_scr, u_scr, carry,
                   *, tiles_per_seq):
    mi = pl.program_id(0)
    ni = pl.program_id(1)
    tm = h_ref.shape[0]
    tn = wg_ref.shape[1]
    w_scr[:, :tn] = wg_ref[...].astype(w_scr.dtype)
    w_scr[:, tn:] = wv_ref[...].astype(w_scr.dtype)

    @pl.when((mi % tiles_per_seq) == 0)
    def _():
        carry[ni] = jnp.zeros(carry.shape[1:], F32)

    u_scr[:SUBLANES, :] = carry[ni]
    sub = tm // FFN_SUB_BLOCKS

    def matmul(s):
        rows = slice(s * sub, (s + 1) * sub)
        u_scr[SUBLANES + s * sub:SUBLANES + (s + 1) * sub, :] = jnp.dot(h_ref[rows, :], w_scr[...],
                                                                        preferred_element_type=F32)

    def conv(s, cols, cw_ref, cb_ref):
        taps = [u_scr[pl.ds(s * sub + SUBLANES - (CONV_WIDTH - 1) + tap, sub), cols] * cw_ref[tap:tap + 1, :]
                for tap in range(CONV_WIDTH)]
        return cb_ref[...] + sum(taps[1:], taps[0])

    def epilogue(s):
        gate = conv(s, slice(0, tn), cwg_ref, cbg_ref)
        val = conv(s, slice(tn, 2 * tn), cwv_ref, cbv_ref)
        silu = gate * (0.5 * jnp.tanh(0.5 * gate) + 0.5)
        o_ref[s * sub:(s + 1) * sub, :] = (silu * val).astype(o_ref.dtype)

    matmul(0)
    for s in range(1, FFN_SUB_BLOCKS):
        matmul(s)
        epilogue(s - 1)
    epilogue(FFN_SUB_BLOCKS - 1)
    carry[ni] = u_scr[tm:, :]


def _ffn_up(h, w_up, conv_w, conv_b, layer, seq, *, tm=2048, tn=256):
    m, d = h.shape
    dff = w_up.shape[2] // 2
    tm = _tile(seq, tm, SUBLANES)
    tn = _tile(dff, tn)
    nt = dff // tn
    assert CONV_WIDTH - 1 <= SUBLANES and tm % (FFN_SUB_BLOCKS * SUBLANES) == 0
    conv_b = conv_b.reshape(conv_b.shape[0], 1, 2 * dff)
    return pl.pallas_call(
        functools.partial(_ffn_up_kernel, tiles_per_seq=seq // tm),
        grid=(m // tm, nt),
        in_specs=[pl.BlockSpec((tm, d), lambda i, j: (i, 0), pipeline_mode=pl.Buffered(1)),
                  pl.BlockSpec((None, d, tn), lambda i, j: (layer, 0, j)),
                  pl.BlockSpec((None, d, tn), lambda i, j: (layer, 0, j + nt)),
                  pl.BlockSpec((None, CONV_WIDTH, tn), lambda i, j: (layer, 0, j)),
                  pl.BlockSpec((None, CONV_WIDTH, tn), lambda i, j: (layer, 0, j + nt)),
                  pl.BlockSpec((None, 1, tn), lambda i, j: (layer, 0, j)),
                  pl.BlockSpec((None, 1, tn), lambda i, j: (layer, 0, j + nt))],
        out_specs=pl.BlockSpec((tm, tn), lambda i, j: (i, j)),
        out_shape=jax.ShapeDtypeStruct((m, dff), BF16),
        scratch_shapes=[pltpu.VMEM((d, 2 * tn), BF16),
                        pltpu.VMEM((SUBLANES + tm, 2 * tn), F32),
                        pltpu.VMEM((nt, SUBLANES, 2 * tn), F32)],
        compiler_params=_params("arbitrary", "arbitrary"),
        name="ffn_up",
    )(h, w_up, w_up, conv_w, conv_w, conv_b, conv_b)


def _pad_cols(w, n):
    return jnp.pad(w, ((0, 0), (0, 0), (0, n - w.shape[2])))


def _dsa_weight_col(j, *, tn):
    nq, niq, nkv2 = A_HEADS * HEAD // tn, IDX_HEADS * HEAD // tn, 2 * A_KV_HEADS * HEAD // tn
    return jnp.where(j < nq, j, jnp.where(j < nq + niq, j + nkv2, j - niq))


def _dsa_mixer(h, x, w_in, w_small, ik_g, ik_b, w_out, slot, bsz, seq):
    a_q, a_kv, a_iq = A_HEADS * HEAD, A_KV_HEADS * HEAD, IDX_HEADS * HEAD
    p = _matmul(h, w_in, n=a_q + a_iq + 2 * a_kv, w_layer=slot, col_block=_dsa_weight_col, out_dtype=BF16)
    p2 = _matmul(h, w_small, w_layer=slot, out_dtype=F32)
    tables, half = _rope_tables(seq, HEAD, HEAD)
    n_rope = a_q + a_iq + a_kv
    pr = _rope(p, seq, tables, half, HEAD, n_cols=n_rope, n_scaled_cols=a_q, scale=HEAD ** -0.5 * LOG2E)
    ik, iw = _idx_prep(p2, seq, ik_g, ik_b, tables, half)
    pr3 = pr.reshape(bsz, seq, n_rope)
    p3 = p.reshape(bsz, seq, p.shape[1])
    n_sel = min(INDEX_TOPK, seq // 4)
    mask = _indexer((pr3, a_q // a_iq), ik.reshape(bsz, seq, HEAD), iw.reshape(bsz, seq, HEAD), n_sel)
    o = _dsa_attention((pr3, 0), (pr3, (a_q + a_iq) // HEAD), (p3, n_rope // HEAD), mask, bsz, seq)
    return _matmul(o.reshape(bsz * seq, a_q), w_out, w_layer=slot, res=x, out_dtype=F32)


def _fox_mixer(h, x, w_in, w_f, f_bias, w_out, slot, bsz, seq):
    d = h.shape[1]
    p = _matmul(h, w_in, n=3 * d, w_layer=slot, out_dtype=BF16)
    f = _matmul(h, w_f, w_layer=slot, out_dtype=F32, tn=HEAD)
    bias = jnp.pad(f_bias.astype(F32), (0, HEAD - B_HEADS)).reshape(1, HEAD)
    cum_t = _fox_cum(f.reshape(bsz, seq, HEAD), bias)
    cum_rows = cum_t[:, :B_HEADS].reshape(bsz, B_HEADS, seq // FOX_KC, 1, FOX_KC)
    o = _fox_attention(p.reshape(bsz, seq, 3 * d), cum_rows, bsz, seq)
    return _matmul(o.reshape(bsz * seq, d), w_out, w_layer=slot, res=x, out_dtype=F32)


def _swa_mixer(h, x, w_in, sinks, w_out, slot, bsz, seq):
    c_q, c_kv = C_HEADS * C_HEAD_DIM, C_KV_HEADS * C_HEAD_DIM
    p = _matmul(h, w_in, w_layer=slot, out_dtype=BF16)
    tables, half = _rope_tables(seq, C_HEAD_DIM, C_HEAD_DIM)
    pr = _rope(p, seq, tables, half, C_HEAD_DIM, n_cols=c_q + c_kv, n_scaled_cols=c_q, scale=C_HEAD_DIM ** -0.5)
    o = _swa_attention(pr.reshape(bsz, seq, c_q + c_kv), p.reshape(bsz, seq, p.shape[1]),
                       sinks.astype(F32), bsz, seq)
    return _matmul(o.reshape(bsz * seq, c_q), w_out, w_layer=slot, res=x, out_dtype=F32)


def _conv_glu_ffn(h, x, w_up, conv_w, conv_b, w_down, layer, seq):
    g = _ffn_up(h, w_up, conv_w, conv_b, layer, seq)
    dff = g.shape[1]
    return _matmul(g, w_down, w_layer=layer, res=x, out_dtype=F32, tm=512, tn=512)


def kernel(x, attn_norm, ffn_norm, final_norm, a_w_in, a_idx_k_norm_g, a_idx_k_norm_b, a_w_out,
           b_w_in, b_f_bias, b_w_out, c_w_in, c_sinks, c_w_out,
           ffn_w_up, ffn_conv_w, ffn_conv_b, ffn_w_down):
    bsz, seq, d = x.shape
    depth = attn_norm.shape[0]
    a_main = A_HEADS * HEAD + 2 * A_KV_HEADS * HEAD + IDX_HEADS * HEAD
    a_w_small = _pad_cols(a_w_in[:, :, a_main:], 2 * HEAD).astype(BF16)
    b_w_f = _pad_cols(b_w_in[:, :, 3 * d:], HEAD).astype(BF16)
    a_w_in, a_w_out, b_w_in, b_w_out, c_w_in, c_w_out, ffn_w_down = (
        w.astype(BF16) for w in (a_w_in, a_w_out, b_w_in, b_w_out, c_w_in, c_w_out, ffn_w_down))
    x = x.reshape(bsz * seq, d)
    for layer in range(depth):
        mixer, slot = layer % N_MIXERS, layer // N_MIXERS
        h = _rmsnorm(x, attn_norm[layer], BF16)
        if mixer == 0:
            x = _dsa_mixer(h, x, a_w_in, a_w_small, a_idx_k_norm_g[slot], a_idx_k_norm_b[slot], a_w_out, slot,
                           bsz, seq)
        elif mixer == 1:
            x = _fox_mixer(h, x, b_w_in, b_w_f, b_f_bias[slot], b_w_out, slot, bsz, seq)
        else:
            x = _swa_mixer(h, x, c_w_in, c_sinks[slot], c_w_out, slot, bsz, seq)
        h = _rmsnorm(x, ffn_norm[layer], BF16)
        x = _conv_glu_ffn(h, x, ffn_w_up, ffn_conv_w, ffn_conv_b, ffn_w_down, layer, seq)
    return _rmsnorm(x, final_norm, F32).reshape(bsz, seq, d)
```

```python
import functools
import math

import jax
import jax.numpy as jnp
import numpy as np
from jax import lax
from jax.experimental import pallas as pl
from jax.experimental.pallas import tpu as pltpu

F32 = jnp.float32
BF16 = jnp.bfloat16

N_MIXERS = 3
ROPE_THETA = 500000.0
ROPE_FRACTION = 4
NORM_EPS = 1e-6
LN_EPS = 1e-6
NEG_INF = -1e30
LOG2E = math.log2(math.e)
HEAD = 128
A_HEADS, A_KV_HEADS, IDX_HEADS, INDEX_TOPK = 32, 8, 16, 256
B_HEADS = 32
C_HEADS, C_KV_HEADS, C_HEAD_DIM, WINDOW = 64, 8, 64, 128
CONV_WIDTH = 3

LANES = 128
SUBLANES = 8
VMEM_LIMIT_BYTES = 56 * 1024 * 1024
INT_MIN = -(2 ** 31)


def _params(*sem, flags=None):
    return pltpu.CompilerParams(dimension_semantics=sem, vmem_limit_bytes=VMEM_LIMIT_BYTES, flags=flags)


def _tile(dim, target, quantum=LANES):
    if dim <= target:
        return dim
    best = None
    for t in range(quantum, target + 1, quantum):
        if dim % t == 0:
            best = t
    assert best is not None, (dim, target)
    return best


def _rmsnorm_kernel(x_ref, g_ref, o_ref):
    x = x_ref[...]
    y = x * lax.rsqrt(jnp.mean(x * x, axis=-1, keepdims=True) + NORM_EPS)
    o_ref[...] = (y * g_ref[...]).astype(o_ref.dtype)


def _rmsnorm(x, g, out_dtype):
    m, d = x.shape
    tm = _tile(m, 256, SUBLANES)
    return pl.pallas_call(
        _rmsnorm_kernel,
        grid=(m // tm,),
        in_specs=[pl.BlockSpec((tm, d), lambda i: (i, 0)), pl.BlockSpec((1, d), lambda i: (0, 0))],
        out_specs=pl.BlockSpec((tm, d), lambda i: (i, 0)),
        out_shape=jax.ShapeDtypeStruct((m, d), out_dtype),
        compiler_params=_params("parallel"),
        name="rmsnorm",
    )(x, g.reshape(1, d))


def _mm_kernel(*refs, nk, has_res):
    if has_res:
        a_ref, w_ref, r_ref, o_ref = refs[:4]
        scratch = refs[4:]
    else:
        a_ref, w_ref, o_ref = refs[:3]
        r_ref = None
        scratch = refs[3:]
    part = jnp.dot(a_ref[...], w_ref[...], preferred_element_type=F32)
    if nk == 1:
        if has_res:
            part = part + r_ref[...]
        o_ref[...] = part.astype(o_ref.dtype)
        return
    acc_ref = scratch[0]
    k = pl.program_id(2)

    @pl.when(k == 0)
    def _():
        acc_ref[...] = part

    @pl.when(k > 0)
    def _():
        acc_ref[...] += part

    @pl.when(k == nk - 1)
    def _():
        out = acc_ref[...]
        if has_res:
            out = out + r_ref[...]
        o_ref[...] = out.astype(o_ref.dtype)


def _matmul(a, w, *, res=None, out_dtype, n=None, w_layer=None, col_block=None, tm=1024, tn=1024, tk=None):
    m, kdim = a.shape
    n = w.shape[-1] if n is None else n
    tm = _tile(m, tm, SUBLANES)
    tn = _tile(n, tn)
    tk = kdim if tk is None else _tile(kdim, tk)
    nk = kdim // tk
    col = (lambda j: j) if col_block is None else functools.partial(col_block, tn=tn)
    if w.ndim == 3:
        w_spec = pl.BlockSpec((None, tk, tn), lambda i, j, k: (w_layer, k, col(j)))
    else:
        w_spec = pl.BlockSpec((tk, tn), lambda i, j, k: (k, col(j)))
    in_specs = [pl.BlockSpec((tm, tk), lambda i, j, k: (i, k)), w_spec]
    args = [a, w]
    if res is not None:
        in_specs.append(pl.BlockSpec((tm, tn), lambda i, j, k: (i, j)))
        args.append(res)
    return pl.pallas_call(
        functools.partial(_mm_kernel, nk=nk, has_res=res is not None),
        grid=(m // tm, n // tn, nk),
        in_specs=in_specs,
        out_specs=pl.BlockSpec((tm, tn), lambda i, j, k: (i, j)),
        out_shape=jax.ShapeDtypeStruct((m, n), out_dtype),
        scratch_shapes=[pltpu.VMEM((tm, tn), F32)] if nk > 1 else [],
        compiler_params=_params("parallel", "parallel", "arbitrary"),
        name="matmul",
    )(*args)


def _rope_tables(seq, head_dim, lanes_per_head):
    rot = head_dim // ROPE_FRACTION
    half = rot // 2
    inv_freq = jnp.power(jnp.float32(ROPE_THETA), -jnp.arange(half, dtype=F32) * (2.0 / rot))
    ang = jnp.arange(seq, dtype=F32)[:, None] * inv_freq[None, :]
    cos, sin = jnp.cos(ang), jnp.sin(ang)
    ones = jnp.ones((seq, lanes_per_head - rot), F32)
    zeros_h = jnp.zeros((seq, half), F32)
    zeros_r = jnp.zeros((seq, lanes_per_head - rot), F32)
    c = jnp.concatenate([cos, cos, ones], axis=1)
    a = jnp.concatenate([-sin, zeros_h, zeros_r], axis=1)
    b = jnp.concatenate([zeros_h, sin, zeros_r], axis=1)
    reps = LANES // lanes_per_head
    return tuple(jnp.tile(t, (1, reps)) for t in (c, a, b)), half


def _rope_apply(x, c, a, b, half):
    up = pltpu.roll(x, LANES - half, axis=1)
    dn = pltpu.roll(x, half, axis=1)
    return x * c + up * a + dn * b


def _rope_kernel(x_ref, c_ref, ab_ref, swap_ref, o_ref, *, n_scaled, scale):
    j = pl.program_id(1)
    s = jnp.where(j < n_scaled, jnp.float32(scale), jnp.float32(1.0))
    c, ab, swap = c_ref[...], ab_ref[...], swap_ref[...]
    for blk in range(x_ref.shape[1] // LANES):
        sl = slice(blk * LANES, (blk + 1) * LANES)
        x = x_ref[:, sl]
        partner = jnp.dot(x, swap, preferred_element_type=F32)
        o_ref[:, sl] = ((x.astype(F32) * c + partner * ab) * s).astype(o_ref.dtype)


def _rope(p, seq, tables, half, lanes_per_head, *, n_cols, n_scaled_cols, scale, tc=1024):
    m = p.shape[0]
    tm = _tile(seq, 1024, SUBLANES)
    tc = _tile(math.gcd(n_cols, n_scaled_cols) if n_scaled_cols else n_cols, tc)
    sb = seq // tm
    c, a, b = tables
    swap = np.zeros((LANES, LANES), np.float32)
    for head in range(0, LANES, lanes_per_head):
        for r in range(half):
            swap[head + r + half, head + r] = 1.0
            swap[head + r, head + r + half] = 1.0
    tab_spec = pl.BlockSpec((tm, LANES), lambda i, j: (i % sb, 0))
    return pl.pallas_call(
        functools.partial(_rope_kernel, n_scaled=n_scaled_cols // tc, scale=scale),
        grid=(m // tm, n_cols // tc),
        in_specs=[pl.BlockSpec((tm, tc), lambda i, j: (i, j)), tab_spec, tab_spec,
                  pl.BlockSpec((LANES, LANES), lambda i, j: (0, 0))],
        out_specs=pl.BlockSpec((tm, tc), lambda i, j: (i, j)),
        out_shape=jax.ShapeDtypeStruct((m, n_cols), p.dtype),
        compiler_params=_params("parallel", "parallel"),
        name="rope",
    )(p, c, a + b, jnp.asarray(swap, p.dtype))


def _idx_prep_kernel(p_ref, g_ref, beta_ref, c_ref, a_ref, b_ref, ik_ref, iw_ref, *, half, iw_scale):
    x = p_ref[:, :HEAD]
    mu = jnp.mean(x, axis=-1, keepdims=True)
    var = jnp.mean(jnp.square(x - mu), axis=-1, keepdims=True)
    y = (x - mu) * lax.rsqrt(var + LN_EPS)
    y = y * g_ref[...] + beta_ref[...]
    ik_ref[...] = _rope_apply(y, c_ref[...], a_ref[...], b_ref[...], half).astype(ik_ref.dtype)
    iw_ref[...] = p_ref[:, HEAD:] * jnp.float32(iw_scale)


def _idx_prep(p2, seq, ik_g, ik_b, tables, half):
    m = p2.shape[0]
    tm = _tile(seq, 512, SUBLANES)
    sb = seq // tm
    tab_spec = pl.BlockSpec((tm, LANES), lambda i: (i % sb, 0))
    vec_spec = pl.BlockSpec((1, HEAD), lambda i: (0, 0))
    blk = pl.BlockSpec((tm, HEAD), lambda i: (i, 0))
    return pl.pallas_call(
        functools.partial(_idx_prep_kernel, half=half, iw_scale=IDX_HEADS ** -0.5 * HEAD ** -0.5),
        grid=(m // tm,),
        in_specs=[pl.BlockSpec((tm, 2 * HEAD), lambda i: (i, 0)), vec_spec, vec_spec, tab_spec, tab_spec, tab_spec],
        out_specs=[blk, blk],
        out_shape=[jax.ShapeDtypeStruct((m, HEAD), BF16), jax.ShapeDtypeStruct((m, HEAD), F32)],
        compiler_params=_params("parallel"),
        name="idx_prep",
    )(p2, ik_g.reshape(1, HEAD), ik_b.reshape(1, HEAD), *tables)


DSA_TQ = 128
DSA_KC = 512
IDX_TQ = 128


def _indexer_kernel(iq_ref, ik_ref, iw_ref, mask_ref, key_ref, *, n_sel, n_chunks):
    qb = pl.program_id(1)
    tq, kc = IDX_TQ, DSA_KC
    nch = (qb * tq + tq - 1) // kc + 1
    idx_bits = (n_chunks * kc - 1).bit_length()
    iw = iw_ref[0]
    q_pos = qb * tq + lax.broadcasted_iota(jnp.int32, (tq, 1), 0)
    lane = lax.broadcasted_iota(jnp.int32, (1, kc), 1)

    def score_chunk(c, _):
        ikc = ik_ref[0, pl.ds(pl.multiple_of(c * kc, kc), kc), :]
        score = jnp.zeros((tq, kc), F32)
        for h in range(IDX_HEADS):
            rel = lax.dot_general(iq_ref[0, :, h * HEAD:(h + 1) * HEAD], ikc, (((1,), (1,)), ((), ())),
                                  preferred_element_type=F32)
            score = score + jnp.maximum(rel, 0.0) * iw[:, h:h + 1]
        bits = pltpu.bitcast(score, jnp.int32)
        key = jnp.where(bits < 0, bits ^ jnp.int32(0x7FFFFFFF), bits)
        key = jnp.where(bits == jnp.int32(INT_MIN), jnp.int32(0), key)
        key = jnp.where(c * kc + lane <= q_pos, key, jnp.int32(INT_MIN))
        key_ref[c] = key
        return 0

    lax.fori_loop(0, nch, score_chunk, 0)

    lane128 = lax.broadcasted_iota(jnp.int32, (1, LANES), 1)

    def for_blocks(c, fn):
        key = key_ref[c]
        return [fn(key[:, b * LANES:(b + 1) * LANES], c * kc + b * LANES + lane128) for b in range(kc // LANES)]

    def count(pred):
        def body(c, acc):
            for hit in for_blocks(c, lambda key, pos: jnp.where(pred(key, pos), jnp.int32(1), jnp.int32(0))):
                acc = acc + hit
            return acc
        acc = lax.fori_loop(0, nch, body, jnp.zeros((tq, LANES), jnp.int32))
        return jnp.broadcast_to(jnp.sum(acc, axis=1, keepdims=True), (tq, LANES))

    def write_mask(sel_fn):
        def store(c, tile):
            for r in range(tq // DSA_TQ):
                mask_ref[0, r, c] = tile[r * DSA_TQ:(r + 1) * DSA_TQ].astype(mask_ref.dtype)

        def body(c, _):
            store(c, jnp.concatenate(for_blocks(c, lambda key, pos: jnp.where(sel_fn(key, pos), 1.0, 0.0)), axis=1))
            return 0
        lax.fori_loop(0, nch, body, 0)

        def tail(c, _):
            store(c, jnp.zeros((tq, kc), F32))
            return 0
        lax.fori_loop(nch, n_chunks, tail, 0)

    cnt0 = count(lambda key, pos: key >= 0)
    nonneg = cnt0 >= n_sel
    t0 = jnp.where(nonneg, jnp.int32(0), jnp.int32(INT_MIN))
    n0 = jnp.where(nonneg, cnt0, jnp.broadcast_to(q_pos + 1, (tq, LANES)))
    n_bits = 31
    bits_per_check = 4

    def bit_step(i, t, n_ge):
        bit = jnp.where(i < n_bits, jnp.left_shift(jnp.int32(1), jnp.maximum(jnp.int32(n_bits - 1) - i, 0)), 0)
        cand = t | bit
        cnt = count(lambda key, pos: key >= cand)
        take = cnt >= n_sel
        return jnp.where(take, cand, t), jnp.where(take, cnt, n_ge)

    def search_cond(state):
        i, _, n_ge = state
        return jnp.logical_and(i < n_bits, jnp.max(n_ge) > n_sel)

    def search_body(state):
        i, t, n_ge = state
        for b in range(bits_per_check):
            t, n_ge = bit_step(i + b, t, n_ge)
        return i + bits_per_check, t, n_ge

    _, thr, n_ge = lax.while_loop(search_cond, search_body, (jnp.int32(0), t0, n0))
    real = thr != jnp.int32(INT_MIN)
    excess = jnp.max(jnp.where(jnp.logical_and(real, n_ge > n_sel), 1, 0)) > 0

    @pl.when(jnp.logical_not(excess))
    def _():
        thr_eff = jnp.maximum(thr, jnp.int32(INT_MIN + 1))
        write_mask(lambda key, pos: key >= thr_eff)

    @pl.when(excess)
    def _():
        need = n_sel - count(lambda key, pos: key > thr)

        def idx_step(i, j):
            cand = j | jnp.left_shift(jnp.int32(1), jnp.int32(idx_bits - 1) - i)
            cnt = count(lambda key, pos: jnp.logical_and(key == thr, pos < cand))
            return jnp.where(cnt < need, cand, j)

        last = lax.fori_loop(0, idx_bits, idx_step, jnp.zeros((tq, LANES), jnp.int32))
        write_mask(lambda key, pos: jnp.logical_or(
            key > thr, jnp.logical_and(jnp.logical_and(key == thr, pos <= last), real)))


def _indexer(iq, ik, iw, n_sel):
    bsz, seq = ik.shape[0], ik.shape[1]
    n_chunks = seq // DSA_KC
    nq = seq // DSA_TQ
    per_step = IDX_TQ // DSA_TQ
    assert seq % IDX_TQ == 0 and IDX_TQ % DSA_TQ == 0
    iq_arr, iq_blk = iq
    return pl.pallas_call(
        functools.partial(_indexer_kernel, n_sel=n_sel, n_chunks=n_chunks),
        grid=(bsz, seq // IDX_TQ),
        in_specs=[pl.BlockSpec((1, IDX_TQ, IDX_HEADS * HEAD), lambda b, q: (b, q, iq_blk)),
                  pl.BlockSpec((1, seq, HEAD), lambda b, q: (b, 0, 0)),
                  pl.BlockSpec((1, IDX_TQ, HEAD), lambda b, q: (b, q, 0))],
        out_specs=pl.BlockSpec((1, per_step, n_chunks, DSA_TQ, DSA_KC), lambda b, q: (b, q, 0, 0, 0)),
        out_shape=jax.ShapeDtypeStruct((bsz, nq, n_chunks, DSA_TQ, DSA_KC), BF16),
        scratch_shapes=[pltpu.VMEM((n_chunks, IDX_TQ, DSA_KC), jnp.int32)],
        compiler_params=_params("parallel", "arbitrary"),
        name="dsa_indexer",
    )(iq_arr, ik, iw)


def _lane_blocks(x):
    return [x[:, j * LANES:(j + 1) * LANES] for j in range(x.shape[1] // LANES)]


STREAMS = 2


def _two_sweep_attention(nch, rows, logits_fn, pv_fn, s_ref, mx_ref, ls_ref, acc_ref):
    mx_ref[...] = jnp.full(mx_ref.shape, -jnp.inf, F32)

    def sweep1_chunk(c, last):
        for i, logits2 in enumerate(logits_fn(c, last)):
            sl = slice(i * rows, (i + 1) * rows)
            s_ref[c, sl, :] = logits2
            mx = mx_ref[sl, :]
            for blk in _lane_blocks(logits2):
                mx = jnp.maximum(mx, blk)
            mx_ref[sl, :] = mx

    def pairs_then_rest(n, chunk_fn):
        def pair(c2, _):
            chunk_fn(2 * c2)
            chunk_fn(2 * c2 + 1)
            return 0

        def single(c, _):
            chunk_fn(c)
            return 0

        lax.fori_loop(0, n // 2, pair, 0)
        lax.fori_loop(2 * (n // 2), n, single, 0)

    pairs_then_rest(nch - 1, lambda c: sweep1_chunk(c, False))
    sweep1_chunk(nch - 1, True)

    row_max = jnp.max(mx_ref[...], axis=1, keepdims=True)
    mx_ref[...] = jnp.broadcast_to(row_max, mx_ref.shape)
    ls_ref[...] = jnp.zeros(ls_ref.shape, F32)
    acc_ref[...] = jnp.zeros(acc_ref.shape, F32)

    def sweep2_chunk(c):
        for i in range(STREAMS):
            sl = slice(i * rows, (i + 1) * rows)
            m = mx_ref[sl, :]
            ps = [jnp.exp2(blk - m) for blk in _lane_blocks(s_ref[c, sl, :])]
            ls_ref[sl, :] += sum(ps[1:], ps[0])
            acc_ref[sl, :] += pv_fn(c, i, jnp.concatenate(ps, axis=1).astype(BF16))

    pairs_then_rest(nch, sweep2_chunk)
    return acc_ref[...] / jnp.sum(ls_ref[...], axis=1, keepdims=True)


def _softmax_scratch(rows, n_chunks, kc):
    return [pltpu.VMEM((n_chunks, rows, kc), F32),
            pltpu.VMEM((rows, LANES), F32),
            pltpu.VMEM((rows, LANES), F32),
            pltpu.VMEM((rows, HEAD), F32)]


def _dsa_attn_kernel(q_ref, k_ref, v_ref, mask_ref, o_ref, s_ref, mx_ref, ls_ref, acc_ref, *, groups):
    qb = pl.program_id(2)
    tq, kc = DSA_TQ, DSA_KC
    nch = (qb * tq) // kc + 1
    rows = groups * tq
    qs = [jnp.concatenate([q_ref[0, :, (i * groups + r) * HEAD:(i * groups + r + 1) * HEAD] for r in range(groups)],
                          axis=0) for i in range(STREAMS)]

    def logits_fn(c, last):
        start = pl.multiple_of(c * kc, kc)
        sel = (mask_ref[0, 0, c].astype(F32) > 0.5)[None]
        out = []
        for i in range(STREAMS):
            k_c = k_ref[0, pl.ds(start, kc), i * HEAD:(i + 1) * HEAD]
            raw = lax.dot_general(qs[i], k_c, (((1,), (1,)), ((), ())), preferred_element_type=F32)
            out.append(jnp.where(sel, raw.reshape(groups, tq, kc), NEG_INF).reshape(rows, kc))
        return out

    def pv_fn(c, i, p):
        v_c = v_ref[0, pl.ds(pl.multiple_of(c * kc, kc), kc), i * HEAD:(i + 1) * HEAD]
        return jnp.dot(p, v_c, preferred_element_type=F32)

    out = _two_sweep_attention(nch, rows, logits_fn, pv_fn, s_ref, mx_ref, ls_ref, acc_ref)
    for r in range(STREAMS * groups):
        o_ref[0, :, r * HEAD:(r + 1) * HEAD] = out[r * tq:(r + 1) * tq].astype(o_ref.dtype)


def _dsa_attention(q, k, v, mask, bsz, seq):
    groups = A_HEADS // A_KV_HEADS
    (q_arr, q0), (k_arr, k0), (v_arr, v0) = q, k, v
    nq = seq // DSA_TQ
    n_chunks = seq // DSA_KC
    qw = STREAMS * groups * HEAD
    kw = STREAMS * HEAD
    assert q0 % (STREAMS * groups) == 0 and k0 % STREAMS == 0 and v0 % STREAMS == 0
    return pl.pallas_call(
        functools.partial(_dsa_attn_kernel, groups=groups),
        grid=(bsz, A_KV_HEADS // STREAMS, nq),
        in_specs=[pl.BlockSpec((1, DSA_TQ, qw), lambda b, g, i: (b, i, q0 // (STREAMS * groups) + g)),
                  pl.BlockSpec((1, seq, kw), lambda b, g, i: (b, 0, k0 // STREAMS + g)),
                  pl.BlockSpec((1, seq, kw), lambda b, g, i: (b, 0, v0 // STREAMS + g)),
                  pl.BlockSpec((1, 1, n_chunks, DSA_TQ, DSA_KC), lambda b, g, i: (b, i, 0, 0, 0))],
        out_specs=pl.BlockSpec((1, DSA_TQ, qw), lambda b, g, i: (b, i, g)),
        out_shape=jax.ShapeDtypeStruct((bsz, seq, A_HEADS * HEAD), BF16),
        scratch_shapes=_softmax_scratch(STREAMS * groups * DSA_TQ, n_chunks, DSA_KC),
        compiler_params=_params("parallel", "parallel", "arbitrary"),
        name="dsa_attention",
    )(q_arr, k_arr, v_arr, mask)


def _fox_cum_kernel(f_ref, bias_ref, cum_t_ref):
    z = f_ref[0] + bias_ref[...]
    x = jnp.minimum(z, 0.0) - jnp.log1p(jnp.exp(-jnp.abs(z)))
    seq = x.shape[0]
    row = lax.broadcasted_iota(jnp.int32, (seq, 1), 0)
    d = 1
    while d < seq:
        x = x + jnp.where(row >= d, pltpu.roll(x, d, axis=0), 0.0)
        d *= 2
    cum_t_ref[0] = (x * jnp.float32(LOG2E)).T


def _fox_cum(f, bias):
    bsz, seq, _ = f.shape
    return pl.pallas_call(
        _fox_cum_kernel,
        grid=(bsz,),
        in_specs=[pl.BlockSpec((1, seq, HEAD), lambda b: (b, 0, 0)), pl.BlockSpec((1, HEAD), lambda b: (0, 0))],
        out_specs=pl.BlockSpec((1, HEAD, seq), lambda b: (b, 0, 0)),
        out_shape=jax.ShapeDtypeStruct((bsz, HEAD, seq), F32),
        compiler_params=_params("parallel"),
        name="fox_cum",
    )(f, bias)


FOX_TQ = 512
FOX_KC = 512


def _fox_attn_kernel(q_ref, k_ref, v_ref, cq_ref, ck_ref, o_ref, s_ref, mx_ref, ls_ref, acc_ref, *, scale2):
    qb = pl.program_id(2)
    tq, kc = FOX_TQ, FOX_KC
    heads = [slice(i * HEAD, (i + 1) * HEAD) for i in range(STREAMS)]
    qs = [(q_ref[0, :, h].astype(F32) * jnp.float32(scale2)).astype(q_ref.dtype) for h in heads]
    cqs = [jnp.broadcast_to(cq_ref[0, i, 0], (LANES, tq)).T for i in range(STREAMS)]

    def logits_fn(c, last):
        start = pl.multiple_of(c * kc, kc)
        out = []
        for i in range(STREAMS):
            raw = lax.dot_general(qs[i], k_ref[0, pl.ds(start, kc), heads[i]], (((1,), (1,)), ((), ())),
                                  preferred_element_type=F32)
            ck = ck_ref[0, i, c]
            blocks = [blk + cqs[i] - ck[:, j * LANES:(j + 1) * LANES] for j, blk in enumerate(_lane_blocks(raw))]
            logits2 = jnp.concatenate(blocks, axis=1)
            if last:
                row = lax.broadcasted_iota(jnp.int32, (tq, 1), 0)
                lane = lax.broadcasted_iota(jnp.int32, (1, kc), 1)
                logits2 = jnp.where(lane <= row, logits2, NEG_INF)
            out.append(logits2)
        return out

    def pv_fn(c, i, p):
        v_c = v_ref[0, pl.ds(pl.multiple_of(c * kc, kc), kc), heads[i]]
        return jnp.dot(p, v_c, preferred_element_type=F32)

    out = _two_sweep_attention(qb + 1, tq, logits_fn, pv_fn, s_ref, mx_ref, ls_ref, acc_ref)
    for i in range(STREAMS):
        o_ref[0, :, heads[i]] = out[i * tq:(i + 1) * tq].astype(o_ref.dtype)


def _fox_attention(p, cum_rows, bsz, seq):
    tq = FOX_TQ
    assert FOX_TQ == FOX_KC and seq % tq == 0
    n_chunks = seq // FOX_KC
    hw = STREAMS * HEAD
    cq_rows = cum_rows.reshape(bsz, B_HEADS, 1, 1, seq)
    return pl.pallas_call(
        functools.partial(_fox_attn_kernel, scale2=HEAD ** -0.5 * LOG2E),
        grid=(bsz, B_HEADS // STREAMS, seq // tq),
        in_specs=[pl.BlockSpec((1, tq, hw), lambda b, h, i: (b, i, h)),
                  pl.BlockSpec((1, seq, hw), lambda b, h, i: (b, 0, B_HEADS // STREAMS + h)),
                  pl.BlockSpec((1, seq, hw), lambda b, h, i: (b, 0, 2 * B_HEADS // STREAMS + h)),
                  pl.BlockSpec((1, STREAMS, 1, 1, tq), lambda b, h, i: (b, h, 0, 0, i)),
                  pl.BlockSpec((1, STREAMS, n_chunks, 1, FOX_KC), lambda b, h, i: (b, h, 0, 0, 0))],
        out_specs=pl.BlockSpec((1, tq, hw), lambda b, h, i: (b, i, h)),
        out_shape=jax.ShapeDtypeStruct((bsz, seq, B_HEADS * HEAD), BF16),
        scratch_shapes=_softmax_scratch(STREAMS * tq, n_chunks, FOX_KC),
        compiler_params=_params("parallel", "parallel", "arbitrary"),
        name="fox_attention",
    )(p, p, p, cq_rows, cum_rows)


def _swa_kernel(sinks_ref, q_ref, kp_ref, kc_ref, vp_ref, vc_ref, o_ref):
    n = pl.program_id(1)
    pair = pl.program_id(2)
    w, dh = WINDOW, C_HEAD_DIM
    groups = C_HEADS // C_KV_HEADS
    kv_per_blk = LANES // dh
    i = lax.broadcasted_iota(jnp.int32, (w, 1), 0)
    j = lax.broadcasted_iota(jnp.int32, (1, 2 * w), 1)
    valid = jnp.logical_and(jnp.logical_and(j > i, j <= i + w), n * w + j >= w)
    kk = jnp.concatenate([kp_ref[0], kc_ref[0]], axis=0)
    vv = jnp.concatenate([vp_ref[0], vc_ref[0]], axis=0)
    for g in range(kv_per_blk):
        k_g = kk[:, g * dh:(g + 1) * dh]
        v_g = vv[:, g * dh:(g + 1) * dh]
        for r in range(groups):
            col = (g * groups + r) * dh
            q = q_ref[0, :, col:col + dh]
            s = lax.dot_general(q, k_g, (((1,), (1,)), ((), ())), preferred_element_type=F32)
            logits = jnp.where(valid, s, NEG_INF)
            sink = sinks_ref[(pair * kv_per_blk + g) * groups + r]
            m = jnp.maximum(jnp.max(logits, axis=1, keepdims=True), sink)
            p = jnp.exp(logits - m)
            denom = jnp.sum(p, axis=1, keepdims=True) + jnp.exp(sink - m)
            o = jnp.dot((p / denom).astype(v_g.dtype), v_g, preferred_element_type=F32)
            o_ref[0, :, col:col + dh] = o.astype(o_ref.dtype)


def _swa_attention(pr, p, sinks, bsz, seq):
    w = WINDOW
    nb = seq // w
    kv_per_blk = LANES // C_HEAD_DIM
    pairs = C_KV_HEADS // kv_per_blk
    qw = kv_per_blk * (C_HEADS // C_KV_HEADS) * C_HEAD_DIM
    k0 = C_HEADS * C_HEAD_DIM // LANES
    v0 = k0 + C_KV_HEADS * C_HEAD_DIM // LANES
    prev = lambda col0: (lambda b, n, g: (b, jnp.maximum(n - 1, 0), col0 + g))
    cur = lambda col0: (lambda b, n, g: (b, n, col0 + g))
    blk = (1, w, LANES)
    return pl.pallas_call(
        _swa_kernel,
        grid=(bsz, nb, pairs),
        in_specs=[pl.BlockSpec(memory_space=pltpu.SMEM),
                  pl.BlockSpec((1, w, qw), lambda b, n, g: (b, n, g)),
                  pl.BlockSpec(blk, prev(k0)), pl.BlockSpec(blk, cur(k0)),
                  pl.BlockSpec(blk, prev(v0)), pl.BlockSpec(blk, cur(v0))],
        out_specs=pl.BlockSpec((1, w, qw), lambda b, n, g: (b, n, g)),
        out_shape=jax.ShapeDtypeStruct((bsz, seq, C_HEADS * C_HEAD_DIM), BF16),
        compiler_params=_params("parallel", "parallel", "parallel"),
        name="swa_attention",
    )(sinks, pr, pr, pr, p, p)


FFN_SUB_BLOCKS = 4


def _ffn_up_kernel(h_ref, wg_ref, wv_ref, cwg_ref, cwv_ref, cbg_ref, cbv_ref, o_ref, w_scr, u_scr, carry,
                   *, tiles_per_seq):
    mi = pl.program_id(0)
    ni = pl.program_id(1)
    tm = h_ref.shape[0]
    tn = wg_ref.shape[1]
    kdim = h_ref.shape[1]
    khalf = kdim // 2

    def cast_weights(k0, k1):
        w_scr[k0:k1, :tn] = wg_ref[k0:k1, :].astype(w_scr.dtype)
        w_scr[k0:k1, tn:] = wv_ref[k0:k1, :].astype(w_scr.dtype)

    @pl.when((mi % tiles_per_seq) == 0)
    def _():
        carry[ni] = jnp.zeros(carry.shape[1:], F32)

    u_scr[:SUBLANES, :] = carry[ni]
    sub = tm // FFN_SUB_BLOCKS

    def matmul(s, k0=0, k1=None, accumulate=False):
        k1 = kdim if k1 is None else k1
        rows = slice(s * sub, (s + 1) * sub)
        dst = slice(SUBLANES + s * sub, SUBLANES + (s + 1) * sub)
        part = jnp.dot(h_ref[rows, k0:k1], w_scr[k0:k1, :], preferred_element_type=F32)
        u_scr[dst, :] = u_scr[dst, :] + part if accumulate else part

    def conv(s, cols, cw_ref, cb_ref):
        taps = [u_scr[pl.ds(s * sub + SUBLANES - (CONV_WIDTH - 1) + tap, sub), cols] * cw_ref[tap:tap + 1, :]
                for tap in range(CONV_WIDTH)]
        return cb_ref[...] + sum(taps[1:], taps[0])

    def epilogue(s):
        gate = conv(s, slice(0, tn), cwg_ref, cbg_ref)
        val = conv(s, slice(tn, 2 * tn), cwv_ref, cbv_ref)
        silu = gate * (0.5 * jnp.tanh(0.5 * gate) + 0.5)
        o_ref[s * sub:(s + 1) * sub, :] = (silu * val).astype(o_ref.dtype)

    cast_weights(0, khalf)
    matmul(0, 0, khalf)
    cast_weights(khalf, kdim)
    matmul(0, khalf, kdim, accumulate=True)
    for s in range(1, FFN_SUB_BLOCKS):
        matmul(s)
        epilogue(s - 1)
    epilogue(FFN_SUB_BLOCKS - 1)
    carry[ni] = u_scr[tm:, :]


def _ffn_up(h, w_up, conv_w, conv_b, layer, seq, *, tm=2048, tn=256):
    m, d = h.shape
    dff = w_up.shape[2] // 2
    tm = _tile(seq, tm, SUBLANES)
    tn = _tile(dff, tn)
    nt = dff // tn
    assert CONV_WIDTH - 1 <= SUBLANES and tm % (FFN_SUB_BLOCKS * SUBLANES) == 0
    conv_b = conv_b.reshape(conv_b.shape[0], 1, 2 * dff)
    return pl.pallas_call(
        functools.partial(_ffn_up_kernel, tiles_per_seq=seq // tm),
        grid=(m // tm, nt),
        in_specs=[pl.BlockSpec((tm, d), lambda i, j: (i, 0), pipeline_mode=pl.Buffered(1)),
                  pl.BlockSpec((None, d, tn), lambda i, j: (layer, 0, j)),
                  pl.BlockSpec((None, d, tn), lambda i, j: (layer, 0, j + nt)),
                  pl.BlockSpec((None, CONV_WIDTH, tn), lambda i, j: (layer, 0, j)),
                  pl.BlockSpec((None, CONV_WIDTH, tn), lambda i, j: (layer, 0, j + nt)),
                  pl.BlockSpec((None, 1, tn), lambda i, j: (layer, 0, j)),
                  pl.BlockSpec((None, 1, tn), lambda i, j: (layer, 0, j + nt))],
        out_specs=pl.BlockSpec((tm, tn), lambda i, j: (i, j)),
        out_shape=jax.ShapeDtypeStruct((m, dff), BF16),
        scratch_shapes=[pltpu.VMEM((d, 2 * tn), BF16),
                        pltpu.VMEM((SUBLANES + tm, 2 * tn), F32),
                        pltpu.VMEM((nt, SUBLANES, 2 * tn), F32)],
        compiler_params=_params("arbitrary", "arbitrary"),
        name="ffn_up",
    )(h, w_up, w_up, conv_w, conv_w, conv_b, conv_b)


OUT_PROJ_TN = 512


def _pad_cols(w, n):
    return jnp.pad(w, ((0, 0), (0, 0), (0, n - w.shape[2])))


def _dsa_weight_col(j, *, tn):
    nq, niq, nkv2 = A_HEADS * HEAD // tn, IDX_HEADS * HEAD // tn, 2 * A_KV_HEADS * HEAD // tn
    return jnp.where(j < nq, j, jnp.where(j < nq + niq, j + nkv2, j - niq))


def _dsa_mixer(h, x, w_in, w_small, ik_g, ik_b, w_out, slot, bsz, seq):
    a_q, a_kv, a_iq = A_HEADS * HEAD, A_KV_HEADS * HEAD, IDX_HEADS * HEAD
    p = _matmul(h, w_in, n=a_q + a_iq + 2 * a_kv, w_layer=slot, col_block=_dsa_weight_col, out_dtype=BF16)
    p2 = _matmul(h, w_small, w_layer=slot, out_dtype=F32)
    tables, half = _rope_tables(seq, HEAD, HEAD)
    n_rope = a_q + a_iq + a_kv
    pr = _rope(p, seq, tables, half, HEAD, n_cols=n_rope, n_scaled_cols=a_q, scale=HEAD ** -0.5 * LOG2E)
    ik, iw = _idx_prep(p2, seq, ik_g, ik_b, tables, half)
    pr3 = pr.reshape(bsz, seq, n_rope)
    p3 = p.reshape(bsz, seq, p.shape[1])
    n_sel = min(INDEX_TOPK, seq // 4)
    mask = _indexer((pr3, a_q // a_iq), ik.reshape(bsz, seq, HEAD), iw.reshape(bsz, seq, HEAD), n_sel)
    o = _dsa_attention((pr3, 0), (pr3, (a_q + a_iq) // HEAD), (p3, n_rope // HEAD), mask, bsz, seq)
    return _matmul(o.reshape(bsz * seq, a_q), w_out, w_layer=slot, res=x, out_dtype=F32, tn=OUT_PROJ_TN)


def _fox_mixer(h, x, w_in, w_f, f_bias, w_out, slot, bsz, seq):
    d = h.shape[1]
    p = _matmul(h, w_in, n=3 * d, w_layer=slot, out_dtype=BF16)
    f = _matmul(h, w_f, w_layer=slot, out_dtype=F32, tn=HEAD)
    bias = jnp.pad(f_bias.astype(F32), (0, HEAD - B_HEADS)).reshape(1, HEAD)
    cum_t = _fox_cum(f.reshape(bsz, seq, HEAD), bias)
    cum_rows = cum_t[:, :B_HEADS].reshape(bsz, B_HEADS, seq // FOX_KC, 1, FOX_KC)
    o = _fox_attention(p.reshape(bsz, seq, 3 * d), cum_rows, bsz, seq)
    return _matmul(o.reshape(bsz * seq, d), w_out, w_layer=slot, res=x, out_dtype=F32, tn=OUT_PROJ_TN)


def _swa_mixer(h, x, w_in, sinks, w_out, slot, bsz, seq):
    c_q, c_kv = C_HEADS * C_HEAD_DIM, C_KV_HEADS * C_HEAD_DIM
    p = _matmul(h, w_in, w_layer=slot, out_dtype=BF16)
    tables, half = _rope_tables(seq, C_HEAD_DIM, C_HEAD_DIM)
    pr = _rope(p, seq, tables, half, C_HEAD_DIM, n_cols=c_q + c_kv, n_scaled_cols=c_q, scale=C_HEAD_DIM ** -0.5)
    o = _swa_attention(pr.reshape(bsz, seq, c_q + c_kv), p.reshape(bsz, seq, p.shape[1]),
                       sinks.astype(F32), bsz, seq)
    return _matmul(o.reshape(bsz * seq, c_q), w_out, w_layer=slot, res=x, out_dtype=F32, tn=OUT_PROJ_TN)


def _conv_glu_ffn(h, x, w_up, conv_w, conv_b, w_down, layer, seq):
    g = _ffn_up(h, w_up, conv_w, conv_b, layer, seq)
    dff = g.shape[1]
    return _matmul(g, w_down, w_layer=layer, res=x, out_dtype=F32, tm=512, tn=512)


def kernel(x, attn_norm, ffn_norm, final_norm, a_w_in, a_idx_k_norm_g, a_idx_k_norm_b, a_w_out,
           b_w_in, b_f_bias, b_w_out, c_w_in, c_sinks, c_w_out,
           ffn_w_up, ffn_conv_w, ffn_conv_b, ffn_w_down):
    bsz, seq, d = x.shape
    depth = attn_norm.shape[0]
    a_main = A_HEADS * HEAD + 2 * A_KV_HEADS * HEAD + IDX_HEADS * HEAD
    a_w_small = _pad_cols(a_w_in[:, :, a_main:], 2 * HEAD).astype(BF16)
    b_w_f = _pad_cols(b_w_in[:, :, 3 * d:], HEAD).astype(BF16)
    a_w_in, a_w_out, b_w_in, b_w_out, c_w_in, c_w_out, ffn_w_down = (
        w.astype(BF16) for w in (a_w_in, a_w_out, b_w_in, b_w_out, c_w_in, c_w_out, ffn_w_down))
    x = x.reshape(bsz * seq, d)
    for layer in range(depth):
        mixer, slot = layer % N_MIXERS, layer // N_MIXERS
        h = _rmsnorm(x, attn_norm[layer], BF16)
        if mixer == 0:
            x = _dsa_mixer(h, x, a_w_in, a_w_small, a_idx_k_norm_g[slot], a_idx_k_norm_b[slot], a_w_out, slot,
                           bsz, seq)
        elif mixer == 1:
            x = _fox_mixer(h, x, b_w_in, b_w_f, b_f_bias[slot], b_w_out, slot, bsz, seq)
        else:
            x = _swa_mixer(h, x, c_w_in, c_sinks[slot], c_w_out, slot, bsz, seq)
        h = _rmsnorm(x, ffn_norm[layer], BF16)
        x = _conv_glu_ffn(h, x, ffn_w_up, ffn_conv_w, ffn_conv_b, ffn_w_down, layer, seq)
    return _rmsnorm(x, final_norm, F32).reshape(bsz, seq, d)
```

```python
import functools
import math

import jax
import jax.numpy as jnp
import numpy as np
from jax import lax
from jax.experimental import pallas as pl
from jax.experimental.pallas import tpu as pltpu

F32 = jnp.float32
BF16 = jnp.bfloat16

N_MIXERS = 3
ROPE_THETA = 500000.0
ROPE_FRACTION = 4
NORM_EPS = 1e-6
LN_EPS = 1e-6
NEG_INF = -1e30
LOG2E = math.log2(math.e)
HEAD = 128
A_HEADS, A_KV_HEADS, IDX_HEADS, INDEX_TOPK = 32, 8, 16, 256
B_HEADS = 32
C_HEADS, C_KV_HEADS, C_HEAD_DIM, WINDOW = 64, 8, 64, 128
CONV_WIDTH = 3

LANES = 128
SUBLANES = 8
VMEM_LIMIT_BYTES = 56 * 1024 * 1024
INT_MIN = -(2 ** 31)


def _params(*sem, flags=None):
    return pltpu.CompilerParams(dimension_semantics=sem, vmem_limit_bytes=VMEM_LIMIT_BYTES, flags=flags)


def _tile(dim, target, quantum=LANES):
    if dim <= target:
        return dim
    best = None
    for t in range(quantum, target + 1, quantum):
        if dim % t == 0:
            best = t
    assert best is not None, (dim, target)
    return best


def _rmsnorm_kernel(x_ref, g_ref, o_ref):
    x = x_ref[...]
    y = x * lax.rsqrt(jnp.mean(x * x, axis=-1, keepdims=True) + NORM_EPS)
    o_ref[...] = (y * g_ref[...]).astype(o_ref.dtype)


def _rmsnorm(x, g, out_dtype):
    m, d = x.shape
    tm = _tile(m, 512, SUBLANES)
    return pl.pallas_call(
        _rmsnorm_kernel,
        grid=(m // tm,),
        in_specs=[pl.BlockSpec((tm, d), lambda i: (i, 0)), pl.BlockSpec((1, d), lambda i: (0, 0))],
        out_specs=pl.BlockSpec((tm, d), lambda i: (i, 0)),
        out_shape=jax.ShapeDtypeStruct((m, d), out_dtype),
        compiler_params=_params("parallel"),
        name="rmsnorm",
    )(x, g.reshape(1, d))


def _mm_kernel(*refs, nk, has_res):
    if has_res:
        a_ref, w_ref, r_ref, o_ref = refs[:4]
        scratch = refs[4:]
    else:
        a_ref, w_ref, o_ref = refs[:3]
        r_ref = None
        scratch = refs[3:]
    part = jnp.dot(a_ref[...], w_ref[...], preferred_element_type=F32)
    if nk == 1:
        if has_res:
            part = part + r_ref[...]
        o_ref[...] = part.astype(o_ref.dtype)
        return
    acc_ref = scratch[0]
    k = pl.program_id(2)

    @pl.when(k == 0)
    def _():
        acc_ref[...] = part

    @pl.when(k > 0)
    def _():
        acc_ref[...] += part

    @pl.when(k == nk - 1)
    def _():
        out = acc_ref[...]
        if has_res:
            out = out + r_ref[...]
        o_ref[...] = out.astype(o_ref.dtype)


def _matmul(a, w, *, res=None, out_dtype, n=None, w_layer=None, col_block=None, tm=1024, tn=1024, tk=None):
    m, kdim = a.shape
    n = w.shape[-1] if n is None else n
    tm = _tile(m, tm, SUBLANES)
    tn = _tile(n, tn)
    tk = kdim if tk is None else _tile(kdim, tk)
    nk = kdim // tk
    col = (lambda j: j) if col_block is None else functools.partial(col_block, tn=tn)
    if w.ndim == 3:
        w_spec = pl.BlockSpec((None, tk, tn), lambda i, j, k: (w_layer, k, col(j)))
    else:
        w_spec = pl.BlockSpec((tk, tn), lambda i, j, k: (k, col(j)))
    in_specs = [pl.BlockSpec((tm, tk), lambda i, j, k: (i, k)), w_spec]
    args = [a, w]
    if res is not None:
        in_specs.append(pl.BlockSpec((tm, tn), lambda i, j, k: (i, j)))
        args.append(res)
    return pl.pallas_call(
        functools.partial(_mm_kernel, nk=nk, has_res=res is not None),
        grid=(m // tm, n // tn, nk),
        in_specs=in_specs,
        out_specs=pl.BlockSpec((tm, tn), lambda i, j, k: (i, j)),
        out_shape=jax.ShapeDtypeStruct((m, n), out_dtype),
        scratch_shapes=[pltpu.VMEM((tm, tn), F32)] if nk > 1 else [],
        compiler_params=_params("parallel", "parallel", "arbitrary"),
        name="matmul",
    )(*args)


def _rope_tables(seq, head_dim, lanes_per_head):
    rot = head_dim // ROPE_FRACTION
    half = rot // 2
    inv_freq = jnp.power(jnp.float32(ROPE_THETA), -jnp.arange(half, dtype=F32) * (2.0 / rot))
    ang = jnp.arange(seq, dtype=F32)[:, None] * inv_freq[None, :]
    cos, sin = jnp.cos(ang), jnp.sin(ang)
    ones = jnp.ones((seq, lanes_per_head - rot), F32)
    zeros_h = jnp.zeros((seq, half), F32)
    zeros_r = jnp.zeros((seq, lanes_per_head - rot), F32)
    c = jnp.concatenate([cos, cos, ones], axis=1)
    a = jnp.concatenate([-sin, zeros_h, zeros_r], axis=1)
    b = jnp.concatenate([zeros_h, sin, zeros_r], axis=1)
    reps = LANES // lanes_per_head
    return tuple(jnp.tile(t, (1, reps)) for t in (c, a, b)), half


def _rope_apply(x, c, a, b, half):
    up = pltpu.roll(x, LANES - half, axis=1)
    dn = pltpu.roll(x, half, axis=1)
    return x * c + up * a + dn * b


def _rope_kernel(x_ref, c_ref, ab_ref, swap_ref, o_ref, *, n_scaled, scale):
    j = pl.program_id(1)
    s = jnp.where(j < n_scaled, jnp.float32(scale), jnp.float32(1.0))
    c, ab, swap = c_ref[...], ab_ref[...], swap_ref[...]
    for blk in range(x_ref.shape[1] // LANES):
        sl = slice(blk * LANES, (blk + 1) * LANES)
        x = x_ref[:, sl]
        partner = jnp.dot(x, swap, preferred_element_type=F32)
        o_ref[:, sl] = ((x.astype(F32) * c + partner * ab) * s).astype(o_ref.dtype)


def _rope(p, seq, tables, half, lanes_per_head, *, n_cols, n_scaled_cols, scale, tc=1024):
    m = p.shape[0]
    tm = _tile(seq, 1024, SUBLANES)
    tc = _tile(math.gcd(n_cols, n_scaled_cols) if n_scaled_cols else n_cols, tc)
    sb = seq // tm
    c, a, b = tables
    swap = np.zeros((LANES, LANES), np.float32)
    for head in range(0, LANES, lanes_per_head):
        for r in range(half):
            swap[head + r + half, head + r] = 1.0
            swap[head + r, head + r + half] = 1.0
    tab_spec = pl.BlockSpec((tm, LANES), lambda i, j: (i % sb, 0))
    return pl.pallas_call(
        functools.partial(_rope_kernel, n_scaled=n_scaled_cols // tc, scale=scale),
        grid=(m // tm, n_cols // tc),
        in_specs=[pl.BlockSpec((tm, tc), lambda i, j: (i, j)), tab_spec, tab_spec,
                  pl.BlockSpec((LANES, LANES), lambda i, j: (0, 0))],
        out_specs=pl.BlockSpec((tm, tc), lambda i, j: (i, j)),
        out_shape=jax.ShapeDtypeStruct((m, n_cols), p.dtype),
        compiler_params=_params("parallel", "parallel"),
        name="rope",
    )(p, c, a + b, jnp.asarray(swap, p.dtype))


def _idx_prep_kernel(p_ref, g_ref, beta_ref, c_ref, a_ref, b_ref, ik_ref, iw_ref, *, half, iw_scale):
    x = p_ref[:, :HEAD]
    mu = jnp.mean(x, axis=-1, keepdims=True)
    var = jnp.mean(jnp.square(x - mu), axis=-1, keepdims=True)
    y = (x - mu) * lax.rsqrt(var + LN_EPS)
    y = y * g_ref[...] + beta_ref[...]
    ik_ref[...] = _rope_apply(y, c_ref[...], a_ref[...], b_ref[...], half).astype(ik_ref.dtype)
    iw_ref[...] = p_ref[:, HEAD:] * jnp.float32(iw_scale)


def _idx_prep(p2, seq, ik_g, ik_b, tables, half):
    m = p2.shape[0]
    tm = _tile(seq, 512, SUBLANES)
    sb = seq // tm
    tab_spec = pl.BlockSpec((tm, LANES), lambda i: (i % sb, 0))
    vec_spec = pl.BlockSpec((1, HEAD), lambda i: (0, 0))
    blk = pl.BlockSpec((tm, HEAD), lambda i: (i, 0))
    return pl.pallas_call(
        functools.partial(_idx_prep_kernel, half=half, iw_scale=IDX_HEADS ** -0.5 * HEAD ** -0.5),
        grid=(m // tm,),
        in_specs=[pl.BlockSpec((tm, 2 * HEAD), lambda i: (i, 0)), vec_spec, vec_spec, tab_spec, tab_spec, tab_spec],
        out_specs=[blk, blk],
        out_shape=[jax.ShapeDtypeStruct((m, HEAD), BF16), jax.ShapeDtypeStruct((m, HEAD), F32)],
        compiler_params=_params("parallel"),
        name="idx_prep",
    )(p2, ik_g.reshape(1, HEAD), ik_b.reshape(1, HEAD), *tables)


DSA_TQ = 128
DSA_KC = 512
IDX_TQ = 128


def _indexer_kernel(iq_ref, ik_ref, iw_ref, mask_ref, key_ref, *, n_sel, n_chunks):
    qb = pl.program_id(1)
    tq, kc = IDX_TQ, DSA_KC
    nch = (qb * tq + tq - 1) // kc + 1
    idx_bits = (n_chunks * kc - 1).bit_length()
    iw = iw_ref[0]
    q_pos = qb * tq + lax.broadcasted_iota(jnp.int32, (tq, 1), 0)
    lane = lax.broadcasted_iota(jnp.int32, (1, kc), 1)

    def score_chunk(c, _):
        ikc = ik_ref[0, pl.ds(pl.multiple_of(c * kc, kc), kc), :]
        score = jnp.zeros((tq, kc), F32)
        for h in range(IDX_HEADS):
            rel = lax.dot_general(iq_ref[0, :, h * HEAD:(h + 1) * HEAD], ikc, (((1,), (1,)), ((), ())),
                                  preferred_element_type=F32)
            score = score + jnp.maximum(rel, 0.0) * iw[:, h:h + 1]
        bits = pltpu.bitcast(score, jnp.int32)
        key = jnp.where(bits < 0, bits ^ jnp.int32(0x7FFFFFFF), bits)
        key = jnp.where(bits == jnp.int32(INT_MIN), jnp.int32(0), key)
        key = jnp.where(c * kc + lane <= q_pos, key, jnp.int32(INT_MIN))
        key_ref[c] = key
        return 0

    lax.fori_loop(0, nch, score_chunk, 0)

    lane128 = lax.broadcasted_iota(jnp.int32, (1, LANES), 1)

    def for_blocks(c, fn):
        key = key_ref[c]
        return [fn(key[:, b * LANES:(b + 1) * LANES], c * kc + b * LANES + lane128) for b in range(kc // LANES)]

    def count(pred):
        def body(c, acc):
            for hit in for_blocks(c, lambda key, pos: jnp.where(pred(key, pos), jnp.int32(1), jnp.int32(0))):
                acc = acc + hit
            return acc
        acc = lax.fori_loop(0, nch, body, jnp.zeros((tq, LANES), jnp.int32))
        return jnp.broadcast_to(jnp.sum(acc, axis=1, keepdims=True), (tq, LANES))

    def write_mask(sel_fn):
        def store(c, tile):
            for r in range(tq // DSA_TQ):
                mask_ref[0, r, c] = tile[r * DSA_TQ:(r + 1) * DSA_TQ].astype(mask_ref.dtype)

        def body(c, _):
            store(c, jnp.concatenate(for_blocks(c, lambda key, pos: jnp.where(sel_fn(key, pos), 1.0, 0.0)), axis=1))
            return 0
        lax.fori_loop(0, nch, body, 0)

        def tail(c, _):
            store(c, jnp.zeros((tq, kc), F32))
            return 0
        lax.fori_loop(nch, n_chunks, tail, 0)

    cnt0 = count(lambda key, pos: key >= 0)
    nonneg = cnt0 >= n_sel
    t0 = jnp.where(nonneg, jnp.int32(0), jnp.int32(INT_MIN))
    n0 = jnp.where(nonneg, cnt0, jnp.broadcast_to(q_pos + 1, (tq, LANES)))
    n_bits = 31
    bits_per_check = 4

    def bit_step(i, t, n_ge):
        bit = jnp.where(i < n_bits, jnp.left_shift(jnp.int32(1), jnp.maximum(jnp.int32(n_bits - 1) - i, 0)), 0)
        cand = t | bit
        cnt = count(lambda key, pos: key >= cand)
        take = cnt >= n_sel
        return jnp.where(take, cand, t), jnp.where(take, cnt, n_ge)

    def search_cond(state):
        i, _, n_ge = state
        return jnp.logical_and(i < n_bits, jnp.max(n_ge) > n_sel)

    def search_body(state):
        i, t, n_ge = state
        for b in range(bits_per_check):
            t, n_ge = bit_step(i + b, t, n_ge)
        return i + bits_per_check, t, n_ge

    _, thr, n_ge = lax.while_loop(search_cond, search_body, (jnp.int32(0), t0, n0))
    real = thr != jnp.int32(INT_MIN)
    excess = jnp.max(jnp.where(jnp.logical_and(real, n_ge > n_sel), 1, 0)) > 0

    @pl.when(jnp.logical_not(excess))
    def _():
        thr_eff = jnp.maximum(thr, jnp.int32(INT_MIN + 1))
        write_mask(lambda key, pos: key >= thr_eff)

    @pl.when(excess)
    def _():
        need = n_sel - count(lambda key, pos: key > thr)

        def idx_step(i, j):
            cand = j | jnp.left_shift(jnp.int32(1), jnp.int32(idx_bits - 1) - i)
            cnt = count(lambda key, pos: jnp.logical_and(key == thr, pos < cand))
            return jnp.where(cnt < need, cand, j)

        last = lax.fori_loop(0, idx_bits, idx_step, jnp.zeros((tq, LANES), jnp.int32))
        write_mask(lambda key, pos: jnp.logical_or(
            key > thr, jnp.logical_and(jnp.logical_and(key == thr, pos <= last), real)))


def _indexer(iq, ik, iw, n_sel):
    bsz, seq = ik.shape[0], ik.shape[1]
    n_chunks = seq // DSA_KC
    nq = seq // DSA_TQ
    per_step = IDX_TQ // DSA_TQ
    assert seq % IDX_TQ == 0 and IDX_TQ % DSA_TQ == 0
    iq_arr, iq_blk = iq
    return pl.pallas_call(
        functools.partial(_indexer_kernel, n_sel=n_sel, n_chunks=n_chunks),
        grid=(bsz, seq // IDX_TQ),
        in_specs=[pl.BlockSpec((1, IDX_TQ, IDX_HEADS * HEAD), lambda b, q: (b, q, iq_blk)),
                  pl.BlockSpec((1, seq, HEAD), lambda b, q: (b, 0, 0)),
                  pl.BlockSpec((1, IDX_TQ, HEAD), lambda b, q: (b, q, 0))],
        out_specs=pl.BlockSpec((1, per_step, n_chunks, DSA_TQ, DSA_KC), lambda b, q: (b, q, 0, 0, 0)),
        out_shape=jax.ShapeDtypeStruct((bsz, nq, n_chunks, DSA_TQ, DSA_KC), BF16),
        scratch_shapes=[pltpu.VMEM((n_chunks, IDX_TQ, DSA_KC), jnp.int32)],
        compiler_params=_params("parallel", "arbitrary"),
        name="dsa_indexer",
    )(iq_arr, ik, iw)


def _lane_blocks(x):
    return [x[:, j * LANES:(j + 1) * LANES] for j in range(x.shape[1] // LANES)]


STREAMS = 2


def _two_sweep_attention(nch, rows, logits_fn, pv_fn, s_ref, mx_ref, ls_ref, acc_ref):
    mx_ref[...] = jnp.full(mx_ref.shape, -jnp.inf, F32)

    def sweep1_chunk(c, last):
        for i, logits2 in enumerate(logits_fn(c, last)):
            sl = slice(i * rows, (i + 1) * rows)
            s_ref[c, sl, :] = logits2
            mx = mx_ref[sl, :]
            for blk in _lane_blocks(logits2):
                mx = jnp.maximum(mx, blk)
            mx_ref[sl, :] = mx

    def pairs_then_rest(n, chunk_fn):
        def run(first, count, width):
            def trip(t, _):
                for k in range(width):
                    chunk_fn(first + width * t + k)
                return 0
            lax.fori_loop(0, count, trip, 0)

        triples = n // 3
        rest = n - 3 * triples
        run(0, triples, 3)
        run(3 * triples, rest // 2, 2)
        run(3 * triples + 2 * (rest // 2), rest % 2, 1)

    pairs_then_rest(nch - 1, lambda c: sweep1_chunk(c, False))
    sweep1_chunk(nch - 1, True)

    row_max = jnp.max(mx_ref[...], axis=1, keepdims=True)
    mx_ref[...] = jnp.broadcast_to(row_max, mx_ref.shape)
    ls_ref[...] = jnp.zeros(ls_ref.shape, F32)
    acc_ref[...] = jnp.zeros(acc_ref.shape, F32)

    def sweep2_chunk(c):
        for i in range(STREAMS):
            sl = slice(i * rows, (i + 1) * rows)
            m = mx_ref[sl, :]
            ps = [jnp.exp2(blk - m) for blk in _lane_blocks(s_ref[c, sl, :])]
            ls_ref[sl, :] += sum(ps[1:], ps[0])
            acc_ref[sl, :] += pv_fn(c, i, jnp.concatenate(ps, axis=1).astype(BF16))

    pairs_then_rest(nch, sweep2_chunk)
    return acc_ref[...] / jnp.sum(ls_ref[...], axis=1, keepdims=True)


def _softmax_scratch(rows, n_chunks, kc):
    return [pltpu.VMEM((n_chunks, rows, kc), F32),
            pltpu.VMEM((rows, LANES), F32),
            pltpu.VMEM((rows, LANES), F32),
            pltpu.VMEM((rows, HEAD), F32)]


def _dsa_attn_kernel(q_ref, k_ref, v_ref, mask_ref, o_ref, s_ref, mx_ref, ls_ref, acc_ref, *, groups):
    qb = pl.program_id(2)
    tq, kc = DSA_TQ, DSA_KC
    nch = (qb * tq) // kc + 1
    rows = groups * tq
    qs = [jnp.concatenate([q_ref[0, :, (i * groups + r) * HEAD:(i * groups + r + 1) * HEAD] for r in range(groups)],
                          axis=0) for i in range(STREAMS)]

    def logits_fn(c, last):
        start = pl.multiple_of(c * kc, kc)
        sel = (mask_ref[0, 0, c].astype(F32) > 0.5)[None]
        out = []
        for i in range(STREAMS):
            k_c = k_ref[0, pl.ds(start, kc), i * HEAD:(i + 1) * HEAD]
            raw = lax.dot_general(qs[i], k_c, (((1,), (1,)), ((), ())), preferred_element_type=F32)
            out.append(jnp.where(sel, raw.reshape(groups, tq, kc), NEG_INF).reshape(rows, kc))
        return out

    def pv_fn(c, i, p):
        v_c = v_ref[0, pl.ds(pl.multiple_of(c * kc, kc), kc), i * HEAD:(i + 1) * HEAD]
        return jnp.dot(p, v_c, preferred_element_type=F32)

    out = _two_sweep_attention(nch, rows, logits_fn, pv_fn, s_ref, mx_ref, ls_ref, acc_ref)
    for r in range(STREAMS * groups):
        o_ref[0, :, r * HEAD:(r + 1) * HEAD] = out[r * tq:(r + 1) * tq].astype(o_ref.dtype)


def _dsa_attention(q, k, v, mask, bsz, seq):
    groups = A_HEADS // A_KV_HEADS
    (q_arr, q0), (k_arr, k0), (v_arr, v0) = q, k, v
    nq = seq // DSA_TQ
    n_chunks = seq // DSA_KC
    qw = STREAMS * groups * HEAD
    kw = STREAMS * HEAD
    assert q0 % (STREAMS * groups) == 0 and k0 % STREAMS == 0 and v0 % STREAMS == 0
    return pl.pallas_call(
        functools.partial(_dsa_attn_kernel, groups=groups),
        grid=(bsz, A_KV_HEADS // STREAMS, nq),
        in_specs=[pl.BlockSpec((1, DSA_TQ, qw), lambda b, g, i: (b, i, q0 // (STREAMS * groups) + g)),
                  pl.BlockSpec((1, seq, kw), lambda b, g, i: (b, 0, k0 // STREAMS + g)),
                  pl.BlockSpec((1, seq, kw), lambda b, g, i: (b, 0, v0 // STREAMS + g)),
                  pl.BlockSpec((1, 1, n_chunks, DSA_TQ, DSA_KC), lambda b, g, i: (b, i, 0, 0, 0))],
        out_specs=pl.BlockSpec((1, DSA_TQ, qw), lambda b, g, i: (b, i, g)),
        out_shape=jax.ShapeDtypeStruct((bsz, seq, A_HEADS * HEAD), BF16),
        scratch_shapes=_softmax_scratch(STREAMS * groups * DSA_TQ, n_chunks, DSA_KC),
        compiler_params=_params("parallel", "parallel", "arbitrary"),
        name="dsa_attention",
    )(q_arr, k_arr, v_arr, mask)


def _fox_cum_kernel(f_ref, bias_ref, cum_t_ref):
    z = f_ref[0] + bias_ref[...]
    x = jnp.minimum(z, 0.0) - jnp.log1p(jnp.exp(-jnp.abs(z)))
    seq = x.shape[0]
    row = lax.broadcasted_iota(jnp.int32, (seq, 1), 0)
    d = 1
    while d < seq:
        x = x + jnp.where(row >= d, pltpu.roll(x, d, axis=0), 0.0)
        d *= 2
    cum_t_ref[0] = (x * jnp.float32(LOG2E)).T


def _fox_cum(f, bias):
    bsz, seq, _ = f.shape
    return pl.pallas_call(
        _fox_cum_kernel,
        grid=(bsz,),
        in_specs=[pl.BlockSpec((1, seq, HEAD), lambda b: (b, 0, 0)), pl.BlockSpec((1, HEAD), lambda b: (0, 0))],
        out_specs=pl.BlockSpec((1, HEAD, seq), lambda b: (b, 0, 0)),
        out_shape=jax.ShapeDtypeStruct((bsz, HEAD, seq), F32),
        compiler_params=_params("parallel"),
        name="fox_cum",
    )(f, bias)


FOX_TQ = 512
FOX_KC = 512


def _fox_attn_kernel(q_ref, k_ref, v_ref, cq_ref, ck_ref, o_ref, s_ref, mx_ref, ls_ref, acc_ref, *, scale2):
    qb = pl.program_id(2)
    tq, kc = FOX_TQ, FOX_KC
    heads = [slice(i * HEAD, (i + 1) * HEAD) for i in range(STREAMS)]
    qs = [(q_ref[0, :, h].astype(F32) * jnp.float32(scale2)).astype(q_ref.dtype) for h in heads]
    cqs = [jnp.broadcast_to(cq_ref[0, i, 0], (LANES, tq)).T for i in range(STREAMS)]

    def logits_fn(c, last):
        start = pl.multiple_of(c * kc, kc)
        out = []
        for i in range(STREAMS):
            raw = lax.dot_general(qs[i], k_ref[0, pl.ds(start, kc), heads[i]], (((1,), (1,)), ((), ())),
                                  preferred_element_type=F32)
            ck = ck_ref[0, i, c]
            blocks = [blk + cqs[i] - ck[:, j * LANES:(j + 1) * LANES] for j, blk in enumerate(_lane_blocks(raw))]
            logits2 = jnp.concatenate(blocks, axis=1)
            if last:
                row = lax.broadcasted_iota(jnp.int32, (tq, 1), 0)
                lane = lax.broadcasted_iota(jnp.int32, (1, kc), 1)
                logits2 = jnp.where(lane <= row, logits2, NEG_INF)
            out.append(logits2)
        return out

    def pv_fn(c, i, p):
        v_c = v_ref[0, pl.ds(pl.multiple_of(c * kc, kc), kc), heads[i]]
        return jnp.dot(p, v_c, preferred_element_type=F32)

    out = _two_sweep_attention(qb + 1, tq, logits_fn, pv_fn, s_ref, mx_ref, ls_ref, acc_ref)
    for i in range(STREAMS):
        o_ref[0, :, heads[i]] = out[i * tq:(i + 1) * tq].astype(o_ref.dtype)


def _fox_attention(p, cum_rows, bsz, seq):
    tq = FOX_TQ
    assert FOX_TQ == FOX_KC and seq % tq == 0
    n_chunks = seq // FOX_KC
    hw = STREAMS * HEAD
    cq_rows = cum_rows.reshape(bsz, B_HEADS, 1, 1, seq)
    return pl.pallas_call(
        functools.partial(_fox_attn_kernel, scale2=HEAD ** -0.5 * LOG2E),
        grid=(bsz, B_HEADS // STREAMS, seq // tq),
        in_specs=[pl.BlockSpec((1, tq, hw), lambda b, h, i: (b, i, h)),
                  pl.BlockSpec((1, seq, hw), lambda b, h, i: (b, 0, B_HEADS // STREAMS + h)),
                  pl.BlockSpec((1, seq, hw), lambda b, h, i: (b, 0, 2 * B_HEADS // STREAMS + h)),
                  pl.BlockSpec((1, STREAMS, 1, 1, tq), lambda b, h, i: (b, h, 0, 0, i)),
                  pl.BlockSpec((1, STREAMS, n_chunks, 1, FOX_KC), lambda b, h, i: (b, h, 0, 0, 0))],
        out_specs=pl.BlockSpec((1, tq, hw), lambda b, h, i: (b, i, h)),
        out_shape=jax.ShapeDtypeStruct((bsz, seq, B_HEADS * HEAD), BF16),
        scratch_shapes=_softmax_scratch(STREAMS * tq, n_chunks, FOX_KC),
        compiler_params=_params("parallel", "parallel", "arbitrary"),
        name="fox_attention",
    )(p, p, p, cq_rows, cum_rows)


def _swa_kernel(sinks_ref, q_ref, kp_ref, kc_ref, vp_ref, vc_ref, o_ref):
    n = pl.program_id(1)
    pair = pl.program_id(2)
    w, dh = WINDOW, C_HEAD_DIM
    groups = C_HEADS // C_KV_HEADS
    kv_per_blk = LANES // dh
    i = lax.broadcasted_iota(jnp.int32, (w, 1), 0)
    j = lax.broadcasted_iota(jnp.int32, (1, 2 * w), 1)
    valid = jnp.logical_and(jnp.logical_and(j > i, j <= i + w), n * w + j >= w)
    kk = jnp.concatenate([kp_ref[0], kc_ref[0]], axis=0)
    vv = jnp.concatenate([vp_ref[0], vc_ref[0]], axis=0)
    for g in range(kv_per_blk):
        k_g = kk[:, g * dh:(g + 1) * dh]
        v_g = vv[:, g * dh:(g + 1) * dh]
        for r in range(groups):
            col = (g * groups + r) * dh
            q = q_ref[0, :, col:col + dh]
            s = lax.dot_general(q, k_g, (((1,), (1,)), ((), ())), preferred_element_type=F32)
            logits = jnp.where(valid, s, NEG_INF)
            sink = sinks_ref[(pair * kv_per_blk + g) * groups + r]
            m = jnp.maximum(jnp.max(logits, axis=1, keepdims=True), sink)
            p = jnp.exp(logits - m)
            denom = jnp.sum(p, axis=1, keepdims=True) + jnp.exp(sink - m)
            o = jnp.dot((p / denom).astype(v_g.dtype), v_g, preferred_element_type=F32)
            o_ref[0, :, col:col + dh] = o.astype(o_ref.dtype)


def _swa_attention(pr, p, sinks, bsz, seq):
    w = WINDOW
    nb = seq // w
    kv_per_blk = LANES // C_HEAD_DIM
    pairs = C_KV_HEADS // kv_per_blk
    qw = kv_per_blk * (C_HEADS // C_KV_HEADS) * C_HEAD_DIM
    k0 = C_HEADS * C_HEAD_DIM // LANES
    v0 = k0 + C_KV_HEADS * C_HEAD_DIM // LANES
    prev = lambda col0: (lambda b, n, g: (b, jnp.maximum(n - 1, 0), col0 + g))
    cur = lambda col0: (lambda b, n, g: (b, n, col0 + g))
    blk = (1, w, LANES)
    return pl.pallas_call(
        _swa_kernel,
        grid=(bsz, nb, pairs),
        in_specs=[pl.BlockSpec(memory_space=pltpu.SMEM),
                  pl.BlockSpec((1, w, qw), lambda b, n, g: (b, n, g)),
                  pl.BlockSpec(blk, prev(k0)), pl.BlockSpec(blk, cur(k0)),
                  pl.BlockSpec(blk, prev(v0)), pl.BlockSpec(blk, cur(v0))],
        out_specs=pl.BlockSpec((1, w, qw), lambda b, n, g: (b, n, g)),
        out_shape=jax.ShapeDtypeStruct((bsz, seq, C_HEADS * C_HEAD_DIM), BF16),
        compiler_params=_params("parallel", "parallel", "parallel"),
        name="swa_attention",
    )(sinks, pr, pr, pr, p, p)


FFN_SUB_BLOCKS = 4
FFN_CAST_SLICES = 2


def _ffn_up_kernel(h_ref, wg_ref, wv_ref, cwg_ref, cwv_ref, cbg_ref, cbv_ref, o_ref, w_scr, u_scr, carry,
                   *, tiles_per_seq):
    mi = pl.program_id(0)
    ni = pl.program_id(1)
    tm = h_ref.shape[0]
    tn = wg_ref.shape[1]
    kdim = h_ref.shape[1]

    def cast_weights(k0, k1):
        w_scr[k0:k1, :tn] = wg_ref[k0:k1, :].astype(w_scr.dtype)
        w_scr[k0:k1, tn:] = wv_ref[k0:k1, :].astype(w_scr.dtype)

    @pl.when((mi % tiles_per_seq) == 0)
    def _():
        carry[ni] = jnp.zeros(carry.shape[1:], F32)

    u_scr[:SUBLANES, :] = carry[ni]
    sub = tm // FFN_SUB_BLOCKS

    def matmul(s, k0=0, k1=None, accumulate=False):
        k1 = kdim if k1 is None else k1
        rows = slice(s * sub, (s + 1) * sub)
        dst = slice(SUBLANES + s * sub, SUBLANES + (s + 1) * sub)
        part = jnp.dot(h_ref[rows, k0:k1], w_scr[k0:k1, :], preferred_element_type=F32)
        u_scr[dst, :] = u_scr[dst, :] + part if accumulate else part

    def conv(s, cols, cw_ref, cb_ref):
        taps = [u_scr[pl.ds(s * sub + SUBLANES - (CONV_WIDTH - 1) + tap, sub), cols] * cw_ref[tap:tap + 1, :]
                for tap in range(CONV_WIDTH)]
        return cb_ref[...] + sum(taps[1:], taps[0])

    def epilogue(s):
        gate = conv(s, slice(0, tn), cwg_ref, cbg_ref)
        val = conv(s, slice(tn, 2 * tn), cwv_ref, cbv_ref)
        silu = gate * (0.5 * jnp.tanh(0.5 * gate) + 0.5)
        o_ref[s * sub:(s + 1) * sub, :] = (silu * val).astype(o_ref.dtype)

    kstep = kdim // FFN_CAST_SLICES
    for ks in range(FFN_CAST_SLICES):
        cast_weights(ks * kstep, (ks + 1) * kstep)
        matmul(0, ks * kstep, (ks + 1) * kstep, accumulate=ks > 0)
    for s in range(1, FFN_SUB_BLOCKS):
        matmul(s)
        epilogue(s - 1)
    epilogue(FFN_SUB_BLOCKS - 1)
    carry[ni] = u_scr[tm:, :]


def _ffn_up(h, w_up, conv_w, conv_b, layer, seq, *, tm=2048, tn=256):
    m, d = h.shape
    dff = w_up.shape[2] // 2
    tm = _tile(seq, tm, SUBLANES)
    tn = _tile(dff, tn)
    nt = dff // tn
    assert CONV_WIDTH - 1 <= SUBLANES and tm % (FFN_SUB_BLOCKS * SUBLANES) == 0
    conv_b = conv_b.reshape(conv_b.shape[0], 1, 2 * dff)
    return pl.pallas_call(
        functools.partial(_ffn_up_kernel, tiles_per_seq=seq // tm),
        grid=(m // tm, nt),
        in_specs=[pl.BlockSpec((tm, d), lambda i, j: (i, 0), pipeline_mode=pl.Buffered(1)),
                  pl.BlockSpec((None, d, tn), lambda i, j: (layer, 0, j)),
                  pl.BlockSpec((None, d, tn), lambda i, j: (layer, 0, j + nt)),
                  pl.BlockSpec((None, CONV_WIDTH, tn), lambda i, j: (layer, 0, j)),
                  pl.BlockSpec((None, CONV_WIDTH, tn), lambda i, j: (layer, 0, j + nt)),
                  pl.BlockSpec((None, 1, tn), lambda i, j: (layer, 0, j)),
                  pl.BlockSpec((None, 1, tn), lambda i, j: (layer, 0, j + nt))],
        out_specs=pl.BlockSpec((tm, tn), lambda i, j: (i, j)),
        out_shape=jax.ShapeDtypeStruct((m, dff), BF16),
        scratch_shapes=[pltpu.VMEM((d, 2 * tn), BF16),
                        pltpu.VMEM((SUBLANES + tm, 2 * tn), F32),
                        pltpu.VMEM((nt, SUBLANES, 2 * tn), F32)],
        compiler_params=_params("arbitrary", "arbitrary"),
        name="ffn_up",
    )(h, w_up, w_up, conv_w, conv_w, conv_b, conv_b)


OUT_PROJ_TN = 512


def _pad_cols(w, n):
    return jnp.pad(w, ((0, 0), (0, 0), (0, n - w.shape[2])))


def _dsa_weight_col(j, *, tn):
    nq, niq, nkv2 = A_HEADS * HEAD // tn, IDX_HEADS * HEAD // tn, 2 * A_KV_HEADS * HEAD // tn
    return jnp.where(j < nq, j, jnp.where(j < nq + niq, j + nkv2, j - niq))


def _dsa_mixer(h, x, w_in, w_small, ik_g, ik_b, w_out, slot, bsz, seq):
    a_q, a_kv, a_iq = A_HEADS * HEAD, A_KV_HEADS * HEAD, IDX_HEADS * HEAD
    p = _matmul(h, w_in, n=a_q + a_iq + 2 * a_kv, w_layer=slot, col_block=_dsa_weight_col, out_dtype=BF16)
    p2 = _matmul(h, w_small, w_layer=slot, out_dtype=F32)
    tables, half = _rope_tables(seq, HEAD, HEAD)
    n_rope = a_q + a_iq + a_kv
    pr = _rope(p, seq, tables, half, HEAD, n_cols=n_rope, n_scaled_cols=a_q, scale=HEAD ** -0.5 * LOG2E)
    ik, iw = _idx_prep(p2, seq, ik_g, ik_b, tables, half)
    pr3 = pr.reshape(bsz, seq, n_rope)
    p3 = p.reshape(bsz, seq, p.shape[1])
    n_sel = min(INDEX_TOPK, seq // 4)
    mask = _indexer((pr3, a_q // a_iq), ik.reshape(bsz, seq, HEAD), iw.reshape(bsz, seq, HEAD), n_sel)
    o = _dsa_attention((pr3, 0), (pr3, (a_q + a_iq) // HEAD), (p3, n_rope // HEAD), mask, bsz, seq)
    return _matmul(o.reshape(bsz * seq, a_q), w_out, w_layer=slot, res=x, out_dtype=F32, tn=OUT_PROJ_TN)


def _fox_mixer(h, x, w_in, w_f, f_bias, w_out, slot, bsz, seq):
    d = h.shape[1]
    p = _matmul(h, w_in, n=3 * d, w_layer=slot, out_dtype=BF16)
    f = _matmul(h, w_f, w_layer=slot, out_dtype=F32, tn=HEAD)
    bias = jnp.pad(f_bias.astype(F32), (0, HEAD - B_HEADS)).reshape(1, HEAD)
    cum_t = _fox_cum(f.reshape(bsz, seq, HEAD), bias)
    cum_rows = cum_t[:, :B_HEADS].reshape(bsz, B_HEADS, seq // FOX_KC, 1, FOX_KC)
    o = _fox_attention(p.reshape(bsz, seq, 3 * d), cum_rows, bsz, seq)
    return _matmul(o.reshape(bsz * seq, d), w_out, w_layer=slot, res=x, out_dtype=F32, tn=OUT_PROJ_TN)


def _swa_mixer(h, x, w_in, sinks, w_out, slot, bsz, seq):
    c_q, c_kv = C_HEADS * C_HEAD_DIM, C_KV_HEADS * C_HEAD_DIM
    p = _matmul(h, w_in, w_layer=slot, out_dtype=BF16)
    tables, half = _rope_tables(seq, C_HEAD_DIM, C_HEAD_DIM)
    pr = _rope(p, seq, tables, half, C_HEAD_DIM, n_cols=c_q + c_kv, n_scaled_cols=c_q, scale=C_HEAD_DIM ** -0.5)
    o = _swa_attention(pr.reshape(bsz, seq, c_q + c_kv), p.reshape(bsz, seq, p.shape[1]),
                       sinks.astype(F32), bsz, seq)
    return _matmul(o.reshape(bsz * seq, c_q), w_out, w_layer=slot, res=x, out_dtype=F32, tn=OUT_PROJ_TN)


def _conv_glu_ffn(h, x, w_up, conv_w, conv_b, w_down, layer, seq):
    g = _ffn_up(h, w_up, conv_w, conv_b, layer, seq)
    dff = g.shape[1]
    return _matmul(g, w_down, w_layer=layer, res=x, out_dtype=F32, tm=512, tn=512)


def kernel(x, attn_norm, ffn_norm, final_norm, a_w_in, a_idx_k_norm_g, a_idx_k_norm_b, a_w_out,
           b_w_in, b_f_bias, b_w_out, c_w_in, c_sinks, c_w_out,
           ffn_w_up, ffn_conv_w, ffn_conv_b, ffn_w_down):
    bsz, seq, d = x.shape
    depth = attn_norm.shape[0]
    a_main = A_HEADS * HEAD + 2 * A_KV_HEADS * HEAD + IDX_HEADS * HEAD
    a_w_small = _pad_cols(a_w_in[:, :, a_main:], 2 * HEAD).astype(BF16)
    b_w_f = _pad_cols(b_w_in[:, :, 3 * d:], HEAD).astype(BF16)
    a_w_in, a_w_out, b_w_in, b_w_out, c_w_in, c_w_out, ffn_w_down = (
        w.astype(BF16) for w in (a_w_in, a_w_out, b_w_in, b_w_out, c_w_in, c_w_out, ffn_w_down))
    x = x.reshape(bsz * seq, d)
    for layer in range(depth):
        mixer, slot = layer % N_MIXERS, layer // N_MIXERS
        h = _rmsnorm(x, attn_norm[layer], BF16)
        if mixer == 0:
            x = _dsa_mixer(h, x, a_w_in, a_w_small, a_idx_k_norm_g[slot], a_idx_k_norm_b[slot], a_w_out, slot,
                           bsz, seq)
        elif mixer == 1:
            x = _fox_mixer(h, x, b_w_in, b_w_f, b_f_bias[slot], b_w_out, slot, bsz, seq)
        else:
            x = _swa_mixer(h, x, c_w_in, c_sinks[slot], c_w_out, slot, bsz, seq)
        h = _rmsnorm(x, ffn_norm[layer], BF16)
        x = _conv_glu_ffn(h, x, ffn_w_up, ffn_conv_w, ffn_conv_b, ffn_w_down, layer, seq)
    return _rmsnorm(x, final_norm, F32).reshape(bsz, seq, d)
```

```python
import functools
import math

import jax
import jax.numpy as jnp
import numpy as np
from jax import lax
from jax.experimental import pallas as pl
from jax.experimental.pallas import tpu as pltpu

F32 = jnp.float32
BF16 = jnp.bfloat16

N_MIXERS = 3
ROPE_THETA = 500000.0
ROPE_FRACTION = 4
NORM_EPS = 1e-6
LN_EPS = 1e-6
NEG_INF = -1e30
LOG2E = math.log2(math.e)
HEAD = 128
A_HEADS, A_KV_HEADS, IDX_HEADS, INDEX_TOPK = 32, 8, 16, 256
B_HEADS = 32
C_HEADS, C_KV_HEADS, C_HEAD_DIM, WINDOW = 64, 8, 64, 128
CONV_WIDTH = 3

LANES = 128
SUBLANES = 8
VMEM_LIMIT_BYTES = 56 * 1024 * 1024
INT_MIN = -(2 ** 31)


def _params(*sem):
    return pltpu.CompilerParams(dimension_semantics=sem, vmem_limit_bytes=VMEM_LIMIT_BYTES)


def _tile(dim, target, quantum=LANES):
    if dim <= target:
        return dim
    best = None
    for t in range(quantum, target + 1, quantum):
        if dim % t == 0:
            best = t
    assert best is not None, (dim, target)
    return best


def _rmsnorm_kernel(x_ref, g_ref, o_ref):
    x = x_ref[...]
    y = x * lax.rsqrt(jnp.mean(x * x, axis=-1, keepdims=True) + NORM_EPS)
    o_ref[...] = (y * g_ref[...]).astype(o_ref.dtype)


def _rmsnorm(x, g, out_dtype):
    m, d = x.shape
    tm = _tile(m, 512, SUBLANES)
    return pl.pallas_call(
        _rmsnorm_kernel,
        grid=(m // tm,),
        in_specs=[pl.BlockSpec((tm, d), lambda i: (i, 0)), pl.BlockSpec((1, d), lambda i: (0, 0))],
        out_specs=pl.BlockSpec((tm, d), lambda i: (i, 0)),
        out_shape=jax.ShapeDtypeStruct((m, d), out_dtype),
        compiler_params=_params("parallel"),
        name="rmsnorm",
    )(x, g.reshape(1, d))


def _mm_kernel(*refs, nk, has_res):
    if has_res:
        a_ref, w_ref, r_ref, o_ref = refs[:4]
        scratch = refs[4:]
    else:
        a_ref, w_ref, o_ref = refs[:3]
        r_ref = None
        scratch = refs[3:]
    part = jnp.dot(a_ref[...], w_ref[...], preferred_element_type=F32)
    if nk == 1:
        if has_res:
            part = part + r_ref[...]
        o_ref[...] = part.astype(o_ref.dtype)
        return
    acc_ref = scratch[0]
    k = pl.program_id(2)

    @pl.when(k == 0)
    def _():
        acc_ref[...] = part

    @pl.when(k > 0)
    def _():
        acc_ref[...] += part

    @pl.when(k == nk - 1)
    def _():
        out = acc_ref[...]
        if has_res:
            out = out + r_ref[...]
        o_ref[...] = out.astype(o_ref.dtype)


def _matmul(a, w, *, res=None, out_dtype, n=None, w_layer=None, col_block=None, tm=1024, tn=1024, tk=None):
    m, kdim = a.shape
    n = w.shape[-1] if n is None else n
    tm = _tile(m, tm, SUBLANES)
    tn = _tile(n, tn)
    tk = kdim if tk is None else _tile(kdim, tk)
    nk = kdim // tk
    col = (lambda j: j) if col_block is None else functools.partial(col_block, tn=tn)
    if w.ndim == 3:
        w_spec = pl.BlockSpec((None, tk, tn), lambda i, j, k: (w_layer, k, col(j)))
    else:
        w_spec = pl.BlockSpec((tk, tn), lambda i, j, k: (k, col(j)))
    in_specs = [pl.BlockSpec((tm, tk), lambda i, j, k: (i, k)), w_spec]
    args = [a, w]
    if res is not None:
        in_specs.append(pl.BlockSpec((tm, tn), lambda i, j, k: (i, j)))
        args.append(res)
    return pl.pallas_call(
        functools.partial(_mm_kernel, nk=nk, has_res=res is not None),
        grid=(m // tm, n // tn, nk),
        in_specs=in_specs,
        out_specs=pl.BlockSpec((tm, tn), lambda i, j, k: (i, j)),
        out_shape=jax.ShapeDtypeStruct((m, n), out_dtype),
        scratch_shapes=[pltpu.VMEM((tm, tn), F32)] if nk > 1 else [],
        compiler_params=_params("parallel", "parallel", "arbitrary"),
        name="matmul",
    )(*args)


def _rope_tables(seq, head_dim, lanes_per_head):
    rot = head_dim // ROPE_FRACTION
    half = rot // 2
    inv_freq = jnp.power(jnp.float32(ROPE_THETA), -jnp.arange(half, dtype=F32) * (2.0 / rot))
    ang = jnp.arange(seq, dtype=F32)[:, None] * inv_freq[None, :]
    cos, sin = jnp.cos(ang), jnp.sin(ang)
    ones = jnp.ones((seq, lanes_per_head - rot), F32)
    zeros_h = jnp.zeros((seq, half), F32)
    zeros_r = jnp.zeros((seq, lanes_per_head - rot), F32)
    c = jnp.concatenate([cos, cos, ones], axis=1)
    a = jnp.concatenate([-sin, zeros_h, zeros_r], axis=1)
    b = jnp.concatenate([zeros_h, sin, zeros_r], axis=1)
    reps = LANES // lanes_per_head
    return tuple(jnp.tile(t, (1, reps)) for t in (c, a, b)), half


def _rope_apply(x, c, a, b, half):
    up = pltpu.roll(x, LANES - half, axis=1)
    dn = pltpu.roll(x, half, axis=1)
    return x * c + up * a + dn * b


def _rope_kernel(x_ref, c_ref, ab_ref, swap_ref, o_ref, *, n_scaled, scale):
    j = pl.program_id(1)
    s = jnp.where(j < n_scaled, jnp.float32(scale), jnp.float32(1.0))
    c, ab, swap = c_ref[...], ab_ref[...], swap_ref[...]
    for blk in range(x_ref.shape[1] // LANES):
        sl = slice(blk * LANES, (blk + 1) * LANES)
        x = x_ref[:, sl]
        partner = jnp.dot(x, swap, preferred_element_type=F32)
        o_ref[:, sl] = ((x.astype(F32) * c + partner * ab) * s).astype(o_ref.dtype)


def _rope(p, seq, tables, half, lanes_per_head, *, n_cols, n_scaled_cols, scale, tc=1024):
    m = p.shape[0]
    tm = _tile(seq, 1024, SUBLANES)
    tc = _tile(math.gcd(n_cols, n_scaled_cols) if n_scaled_cols else n_cols, tc)
    sb = seq // tm
    c, a, b = tables
    swap = np.zeros((LANES, LANES), np.float32)
    for head in range(0, LANES, lanes_per_head):
        for r in range(half):
            swap[head + r + half, head + r] = 1.0
            swap[head + r, head + r + half] = 1.0
    tab_spec = pl.BlockSpec((tm, LANES), lambda i, j: (i % sb, 0))
    return pl.pallas_call(
        functools.partial(_rope_kernel, n_scaled=n_scaled_cols // tc, scale=scale),
        grid=(m // tm, n_cols // tc),
        in_specs=[pl.BlockSpec((tm, tc), lambda i, j: (i, j)), tab_spec, tab_spec,
                  pl.BlockSpec((LANES, LANES), lambda i, j: (0, 0))],
        out_specs=pl.BlockSpec((tm, tc), lambda i, j: (i, j)),
        out_shape=jax.ShapeDtypeStruct((m, n_cols), p.dtype),
        compiler_params=_params("parallel", "parallel"),
        name="rope",
    )(p, c, a + b, jnp.asarray(swap, p.dtype))


def _idx_prep_kernel(p_ref, g_ref, beta_ref, c_ref, a_ref, b_ref, ik_ref, iw_ref, *, half, iw_scale):
    x = p_ref[:, :HEAD]
    mu = jnp.mean(x, axis=-1, keepdims=True)
    var = jnp.mean(jnp.square(x - mu), axis=-1, keepdims=True)
    y = (x - mu) * lax.rsqrt(var + LN_EPS)
    y = y * g_ref[...] + beta_ref[...]
    ik_ref[...] = _rope_apply(y, c_ref[...], a_ref[...], b_ref[...], half).astype(ik_ref.dtype)
    iw_ref[...] = p_ref[:, HEAD:] * jnp.float32(iw_scale)


def _idx_prep(p2, seq, ik_g, ik_b, tables, half):
    m = p2.shape[0]
    tm = _tile(seq, 512, SUBLANES)
    sb = seq // tm
    tab_spec = pl.BlockSpec((tm, LANES), lambda i: (i % sb, 0))
    vec_spec = pl.BlockSpec((1, HEAD), lambda i: (0, 0))
    blk = pl.BlockSpec((tm, HEAD), lambda i: (i, 0))
    return pl.pallas_call(
        functools.partial(_idx_prep_kernel, half=half, iw_scale=IDX_HEADS ** -0.5 * HEAD ** -0.5),
        grid=(m // tm,),
        in_specs=[pl.BlockSpec((tm, 2 * HEAD), lambda i: (i, 0)), vec_spec, vec_spec, tab_spec, tab_spec, tab_spec],
        out_specs=[blk, blk],
        out_shape=[jax.ShapeDtypeStruct((m, HEAD), BF16), jax.ShapeDtypeStruct((m, HEAD), F32)],
        compiler_params=_params("parallel"),
        name="idx_prep",
    )(p2, ik_g.reshape(1, HEAD), ik_b.reshape(1, HEAD), *tables)


DSA_TQ = 128
DSA_KC = 512
IDX_TQ = 128


def _indexer_kernel(iq_ref, ik_ref, iw_ref, mask_ref, key_ref, *, n_sel, n_chunks):
    qb = pl.program_id(1)
    tq, kc = IDX_TQ, DSA_KC
    nch = (qb * tq + tq - 1) // kc + 1
    idx_bits = (n_chunks * kc - 1).bit_length()
    iw = iw_ref[0]
    q_pos = qb * tq + lax.broadcasted_iota(jnp.int32, (tq, 1), 0)
    lane = lax.broadcasted_iota(jnp.int32, (1, kc), 1)

    def score_chunk(c, _):
        ikc = ik_ref[0, pl.ds(pl.multiple_of(c * kc, kc), kc), :]
        score = jnp.zeros((tq, kc), F32)
        for h in range(IDX_HEADS):
            rel = lax.dot_general(iq_ref[0, :, h * HEAD:(h + 1) * HEAD], ikc, (((1,), (1,)), ((), ())),
                                  preferred_element_type=F32)
            score = score + jnp.maximum(rel, 0.0) * iw[:, h:h + 1]
        bits = pltpu.bitcast(score, jnp.int32)
        key = jnp.where(bits < 0, bits ^ jnp.int32(0x7FFFFFFF), bits)
        key = jnp.where(bits == jnp.int32(INT_MIN), jnp.int32(0), key)
        key = jnp.where(c * kc + lane <= q_pos, key, jnp.int32(INT_MIN))
        key_ref[c] = key
        return 0

    lax.fori_loop(0, nch, score_chunk, 0)

    lane128 = lax.broadcasted_iota(jnp.int32, (1, LANES), 1)

    def for_blocks(c, fn):
        key = key_ref[c]
        return [fn(key[:, b * LANES:(b + 1) * LANES], c * kc + b * LANES + lane128) for b in range(kc // LANES)]

    def count(pred):
        def body(c, acc):
            for hit in for_blocks(c, lambda key, pos: jnp.where(pred(key, pos), jnp.int32(1), jnp.int32(0))):
                acc = acc + hit
            return acc
        acc = lax.fori_loop(0, nch, body, jnp.zeros((tq, LANES), jnp.int32))
        return jnp.broadcast_to(jnp.sum(acc, axis=1, keepdims=True), (tq, LANES))

    def write_mask(sel_fn):
        def store(c, tile):
            for r in range(tq // DSA_TQ):
                mask_ref[0, r, c] = tile[r * DSA_TQ:(r + 1) * DSA_TQ].astype(mask_ref.dtype)

        def body(c, _):
            store(c, jnp.concatenate(for_blocks(c, lambda key, pos: jnp.where(sel_fn(key, pos), 1.0, 0.0)), axis=1))
            return 0
        lax.fori_loop(0, nch, body, 0)

        def tail(c, _):
            store(c, jnp.zeros((tq, kc), F32))
            return 0
        lax.fori_loop(nch, n_chunks, tail, 0)

    cnt0 = count(lambda key, pos: key >= 0)
    nonneg = cnt0 >= n_sel
    t0 = jnp.where(nonneg, jnp.int32(0), jnp.int32(INT_MIN))
    n0 = jnp.where(nonneg, cnt0, jnp.broadcast_to(q_pos + 1, (tq, LANES)))
    n_bits = 31
    bits_per_check = 4

    def bit_step(i, t, n_ge):
        bit = jnp.where(i < n_bits, jnp.left_shift(jnp.int32(1), jnp.maximum(jnp.int32(n_bits - 1) - i, 0)), 0)
        cand = t | bit
        cnt = count(lambda key, pos: key >= cand)
        take = cnt >= n_sel
        return jnp.where(take, cand, t), jnp.where(take, cnt, n_ge)

    def search_cond(state):
        i, _, n_ge = state
        return jnp.logical_and(i < n_bits, jnp.max(n_ge) > n_sel)

    def search_body(state):
        i, t, n_ge = state
        for b in range(bits_per_check):
            t, n_ge = bit_step(i + b, t, n_ge)
        return i + bits_per_check, t, n_ge

    _, thr, n_ge = lax.while_loop(search_cond, search_body, (jnp.int32(0), t0, n0))
    real = thr != jnp.int32(INT_MIN)
    excess = jnp.max(jnp.where(jnp.logical_and(real, n_ge > n_sel), 1, 0)) > 0

    @pl.when(jnp.logical_not(excess))
    def _():
        thr_eff = jnp.maximum(thr, jnp.int32(INT_MIN + 1))
        write_mask(lambda key, pos: key >= thr_eff)

    @pl.when(excess)
    def _():
        need = n_sel - count(lambda key, pos: key > thr)

        def idx_step(i, j):
            cand = j | jnp.left_shift(jnp.int32(1), jnp.int32(idx_bits - 1) - i)
            cnt = count(lambda key, pos: jnp.logical_and(key == thr, pos < cand))
            return jnp.where(cnt < need, cand, j)

        last = lax.fori_loop(0, idx_bits, idx_step, jnp.zeros((tq, LANES), jnp.int32))
        write_mask(lambda key, pos: jnp.logical_or(
            key > thr, jnp.logical_and(jnp.logical_and(key == thr, pos <= last), real)))


def _indexer(iq, ik, iw, n_sel):
    bsz, seq = ik.shape[0], ik.shape[1]
    n_chunks = seq // DSA_KC
    nq = seq // DSA_TQ
    per_step = IDX_TQ // DSA_TQ
    assert seq % IDX_TQ == 0 and IDX_TQ % DSA_TQ == 0
    iq_arr, iq_blk = iq
    return pl.pallas_call(
        functools.partial(_indexer_kernel, n_sel=n_sel, n_chunks=n_chunks),
        grid=(bsz, seq // IDX_TQ),
        in_specs=[pl.BlockSpec((1, IDX_TQ, IDX_HEADS * HEAD), lambda b, q: (b, q, iq_blk)),
                  pl.BlockSpec((1, seq, HEAD), lambda b, q: (b, 0, 0)),
                  pl.BlockSpec((1, IDX_TQ, HEAD), lambda b, q: (b, q, 0))],
        out_specs=pl.BlockSpec((1, per_step, n_chunks, DSA_TQ, DSA_KC), lambda b, q: (b, q, 0, 0, 0)),
        out_shape=jax.ShapeDtypeStruct((bsz, nq, n_chunks, DSA_TQ, DSA_KC), BF16),
        scratch_shapes=[pltpu.VMEM((n_chunks, IDX_TQ, DSA_KC), jnp.int32)],
        compiler_params=_params("parallel", "arbitrary"),
        name="dsa_indexer",
    )(iq_arr, ik, iw)


def _lane_blocks(x):
    return [x[:, j * LANES:(j + 1) * LANES] for j in range(x.shape[1] // LANES)]


STREAMS = 2


def _two_sweep_attention(nch, rows, logits_fn, pv_fn, s_ref, mx_ref, ls_ref, acc_ref):
    mx_ref[...] = jnp.full(mx_ref.shape, -jnp.inf, F32)

    def sweep1_chunk(c, last):
        for i, logits2 in enumerate(logits_fn(c, last)):
            sl = slice(i * rows, (i + 1) * rows)
            s_ref[c, sl, :] = logits2
            mx = mx_ref[sl, :]
            for blk in _lane_blocks(logits2):
                mx = jnp.maximum(mx, blk)
            mx_ref[sl, :] = mx

    def pairs_then_rest(n, chunk_fn):
        def run(first, count, width):
            def trip(t, _):
                for k in range(width):
                    chunk_fn(first + width * t + k)
                return 0
            lax.fori_loop(0, count, trip, 0)

        triples = n // 3
        rest = n - 3 * triples
        run(0, triples, 3)
        run(3 * triples, rest // 2, 2)
        run(3 * triples + 2 * (rest // 2), rest % 2, 1)

    pairs_then_rest(nch - 1, lambda c: sweep1_chunk(c, False))
    sweep1_chunk(nch - 1, True)

    row_max = jnp.max(mx_ref[...], axis=1, keepdims=True)
    mx_ref[...] = jnp.broadcast_to(row_max, mx_ref.shape)
    ls_ref[...] = jnp.zeros(ls_ref.shape, F32)
    acc_ref[...] = jnp.zeros(acc_ref.shape, F32)

    def sweep2_chunk(c):
        for i in range(STREAMS):
            sl = slice(i * rows, (i + 1) * rows)
            m = mx_ref[sl, :]
            ps = [jnp.exp2(blk - m) for blk in _lane_blocks(s_ref[c, sl, :])]
            ls_ref[sl, :] += sum(ps[1:], ps[0])
            acc_ref[sl, :] += pv_fn(c, i, jnp.concatenate(ps, axis=1).astype(BF16))

    pairs_then_rest(nch, sweep2_chunk)
    return acc_ref[...] / jnp.sum(ls_ref[...], axis=1, keepdims=True)


def _softmax_scratch(rows, n_chunks, kc):
    return [pltpu.VMEM((n_chunks, rows, kc), F32),
            pltpu.VMEM((rows, LANES), F32),
            pltpu.VMEM((rows, LANES), F32),
            pltpu.VMEM((rows, HEAD), F32)]


def _dsa_attn_kernel(q_ref, k_ref, v_ref, mask_ref, o_ref, s_ref, mx_ref, ls_ref, acc_ref, *, groups):
    qb = pl.program_id(2)
    tq, kc = DSA_TQ, DSA_KC
    nch = (qb * tq) // kc + 1
    rows = groups * tq
    qs = [jnp.concatenate([q_ref[0, :, (i * groups + r) * HEAD:(i * groups + r + 1) * HEAD] for r in range(groups)],
                          axis=0) for i in range(STREAMS)]

    def logits_fn(c, last):
        start = pl.multiple_of(c * kc, kc)
        sel = (mask_ref[0, 0, c].astype(F32) > 0.5)[None]
        out = []
        for i in range(STREAMS):
            k_c = k_ref[0, pl.ds(start, kc), i * HEAD:(i + 1) * HEAD]
            raw = lax.dot_general(qs[i], k_c, (((1,), (1,)), ((), ())), preferred_element_type=F32)
            out.append(jnp.where(sel, raw.reshape(groups, tq, kc), NEG_INF).reshape(rows, kc))
        return out

    def pv_fn(c, i, p):
        v_c = v_ref[0, pl.ds(pl.multiple_of(c * kc, kc), kc), i * HEAD:(i + 1) * HEAD]
        return jnp.dot(p, v_c, preferred_element_type=F32)

    out = _two_sweep_attention(nch, rows, logits_fn, pv_fn, s_ref, mx_ref, ls_ref, acc_ref)
    for r in range(STREAMS * groups):
        o_ref[0, :, r * HEAD:(r + 1) * HEAD] = out[r * tq:(r + 1) * tq].astype(o_ref.dtype)


def _dsa_attention(q, k, v, mask, bsz, seq):
    groups = A_HEADS // A_KV_HEADS
    (q_arr, q0), (k_arr, k0), (v_arr, v0) = q, k, v
    nq = seq // DSA_TQ
    n_chunks = seq // DSA_KC
    qw = STREAMS * groups * HEAD
    kw = STREAMS * HEAD
    assert q0 % (STREAMS * groups) == 0 and k0 % STREAMS == 0 and v0 % STREAMS == 0
    return pl.pallas_call(
        functools.partial(_dsa_attn_kernel, groups=groups),
        grid=(bsz, A_KV_HEADS // STREAMS, nq),
        in_specs=[pl.BlockSpec((1, DSA_TQ, qw), lambda b, g, i: (b, i, q0 // (STREAMS * groups) + g)),
                  pl.BlockSpec((1, seq, kw), lambda b, g, i: (b, 0, k0 // STREAMS + g)),
                  pl.BlockSpec((1, seq, kw), lambda b, g, i: (b, 0, v0 // STREAMS + g)),
                  pl.BlockSpec((1, 1, n_chunks, DSA_TQ, DSA_KC), lambda b, g, i: (b, i, 0, 0, 0))],
        out_specs=pl.BlockSpec((1, DSA_TQ, qw), lambda b, g, i: (b, i, g)),
        out_shape=jax.ShapeDtypeStruct((bsz, seq, A_HEADS * HEAD), BF16),
        scratch_shapes=_softmax_scratch(STREAMS * groups * DSA_TQ, n_chunks, DSA_KC),
        compiler_params=_params("parallel", "parallel", "arbitrary"),
        name="dsa_attention",
    )(q_arr, k_arr, v_arr, mask)


def _fox_cum_kernel(f_ref, bias_ref, cum_t_ref):
    z = f_ref[0] + bias_ref[...]
    x = jnp.minimum(z, 0.0) - jnp.log1p(jnp.exp(-jnp.abs(z)))
    seq = x.shape[0]
    row = lax.broadcasted_iota(jnp.int32, (seq, 1), 0)
    d = 1
    while d < seq:
        x = x + jnp.where(row >= d, pltpu.roll(x, d, axis=0), 0.0)
        d *= 2
    cum_t_ref[0] = (x * jnp.float32(LOG2E)).T


def _fox_cum(f, bias):
    bsz, seq, _ = f.shape
    return pl.pallas_call(
        _fox_cum_kernel,
        grid=(bsz,),
        in_specs=[pl.BlockSpec((1, seq, HEAD), lambda b: (b, 0, 0)), pl.BlockSpec((1, HEAD), lambda b: (0, 0))],
        out_specs=pl.BlockSpec((1, HEAD, seq), lambda b: (b, 0, 0)),
        out_shape=jax.ShapeDtypeStruct((bsz, HEAD, seq), F32),
        compiler_params=_params("parallel"),
        name="fox_cum",
    )(f, bias)


FOX_TQ = 512
FOX_KC = 512


def _fox_attn_kernel(q_ref, k_ref, v_ref, cq_ref, ck_ref, o_ref, s_ref, mx_ref, ls_ref, acc_ref, *, scale2):
    qb = pl.program_id(2)
    tq, kc = FOX_TQ, FOX_KC
    heads = [slice(i * HEAD, (i + 1) * HEAD) for i in range(STREAMS)]
    qs = [(q_ref[0, :, h].astype(F32) * jnp.float32(scale2)).astype(q_ref.dtype) for h in heads]
    cqs = [jnp.broadcast_to(cq_ref[0, i, 0], (LANES, tq)).T for i in range(STREAMS)]

    def logits_fn(c, last):
        start = pl.multiple_of(c * kc, kc)
        out = []
        for i in range(STREAMS):
            raw = lax.dot_general(qs[i], k_ref[0, pl.ds(start, kc), heads[i]], (((1,), (1,)), ((), ())),
                                  preferred_element_type=F32)
            ck = ck_ref[0, i, c]
            blocks = [blk + cqs[i] - ck[:, j * LANES:(j + 1) * LANES] for j, blk in enumerate(_lane_blocks(raw))]
            logits2 = jnp.concatenate(blocks, axis=1)
            if last:
                row = lax.broadcasted_iota(jnp.int32, (tq, 1), 0)
                lane = lax.broadcasted_iota(jnp.int32, (1, kc), 1)
                logits2 = jnp.where(lane <= row, logits2, NEG_INF)
            out.append(logits2)
        return out

    def pv_fn(c, i, p):
        v_c = v_ref[0, pl.ds(pl.multiple_of(c * kc, kc), kc), heads[i]]
        return jnp.dot(p, v_c, preferred_element_type=F32)

    out = _two_sweep_attention(qb + 1, tq, logits_fn, pv_fn, s_ref, mx_ref, ls_ref, acc_ref)
    for i in range(STREAMS):
        o_ref[0, :, heads[i]] = out[i * tq:(i + 1) * tq].astype(o_ref.dtype)


def _fox_attention(p, cum_rows, bsz, seq):
    tq = FOX_TQ
    assert FOX_TQ == FOX_KC and seq % tq == 0
    n_chunks = seq // FOX_KC
    hw = STREAMS * HEAD
    cq_rows = cum_rows.reshape(bsz, B_HEADS, 1, 1, seq)
    return pl.pallas_call(
        functools.partial(_fox_attn_kernel, scale2=HEAD ** -0.5 * LOG2E),
        grid=(bsz, B_HEADS // STREAMS, seq // tq),
        in_specs=[pl.BlockSpec((1, tq, hw), lambda b, h, i: (b, i, h)),
                  pl.BlockSpec((1, seq, hw), lambda b, h, i: (b, 0, B_HEADS // STREAMS + h)),
                  pl.BlockSpec((1, seq, hw), lambda b, h, i: (b, 0, 2 * B_HEADS // STREAMS + h)),
                  pl.BlockSpec((1, STREAMS, 1, 1, tq), lambda b, h, i: (b, h, 0, 0, i)),
                  pl.BlockSpec((1, STREAMS, n_chunks, 1, FOX_KC), lambda b, h, i: (b, h, 0, 0, 0))],
        out_specs=pl.BlockSpec((1, tq, hw), lambda b, h, i: (b, i, h)),
        out_shape=jax.ShapeDtypeStruct((bsz, seq, B_HEADS * HEAD), BF16),
        scratch_shapes=_softmax_scratch(STREAMS * tq, n_chunks, FOX_KC),
        compiler_params=_params("parallel", "parallel", "arbitrary"),
        name="fox_attention",
    )(p, p, p, cq_rows, cum_rows)


def _swa_kernel(sinks_ref, q_ref, kp_ref, kc_ref, vp_ref, vc_ref, o_ref):
    n = pl.program_id(1)
    pair = pl.program_id(2)
    w, dh = WINDOW, C_HEAD_DIM
    groups = C_HEADS // C_KV_HEADS
    kv_per_blk = LANES // dh
    i = lax.broadcasted_iota(jnp.int32, (w, 1), 0)
    j = lax.broadcasted_iota(jnp.int32, (1, 2 * w), 1)
    valid = jnp.logical_and(jnp.logical_and(j > i, j <= i + w), n * w + j >= w)
    kk = jnp.concatenate([kp_ref[0], kc_ref[0]], axis=0)
    vv = jnp.concatenate([vp_ref[0], vc_ref[0]], axis=0)
    for g in range(kv_per_blk):
        k_g = kk[:, g * dh:(g + 1) * dh]
        v_g = vv[:, g * dh:(g + 1) * dh]
        for r in range(groups):
            col = (g * groups + r) * dh
            q = q_ref[0, :, col:col + dh]
            s = lax.dot_general(q, k_g, (((1,), (1,)), ((), ())), preferred_element_type=F32)
            logits = jnp.where(valid, s, NEG_INF)
            sink = sinks_ref[(pair * kv_per_blk + g) * groups + r]
            m = jnp.maximum(jnp.max(logits, axis=1, keepdims=True), sink)
            p = jnp.exp(logits - m)
            denom = jnp.sum(p, axis=1, keepdims=True) + jnp.exp(sink - m)
            o = jnp.dot((p / denom).astype(v_g.dtype), v_g, preferred_element_type=F32)
            o_ref[0, :, col:col + dh] = o.astype(o_ref.dtype)


def _swa_attention(pr, p, sinks, bsz, seq):
    w = WINDOW
    nb = seq // w
    kv_per_blk = LANES // C_HEAD_DIM
    pairs = C_KV_HEADS // kv_per_blk
    qw = kv_per_blk * (C_HEADS // C_KV_HEADS) * C_HEAD_DIM
    k0 = C_HEADS * C_HEAD_DIM // LANES
    v0 = k0 + C_KV_HEADS * C_HEAD_DIM // LANES
    prev = lambda col0: (lambda b, n, g: (b, jnp.maximum(n - 1, 0), col0 + g))
    cur = lambda col0: (lambda b, n, g: (b, n, col0 + g))
    blk = (1, w, LANES)
    return pl.pallas_call(
        _swa_kernel,
        grid=(bsz, nb, pairs),
        in_specs=[pl.BlockSpec(memory_space=pltpu.SMEM),
                  pl.BlockSpec((1, w, qw), lambda b, n, g: (b, n, g)),
                  pl.BlockSpec(blk, prev(k0)), pl.BlockSpec(blk, cur(k0)),
                  pl.BlockSpec(blk, prev(v0)), pl.BlockSpec(blk, cur(v0))],
        out_specs=pl.BlockSpec((1, w, qw), lambda b, n, g: (b, n, g)),
        out_shape=jax.ShapeDtypeStruct((bsz, seq, C_HEADS * C_HEAD_DIM), BF16),
        compiler_params=_params("parallel", "parallel", "parallel"),
        name="swa_attention",
    )(sinks, pr, pr, pr, p, p)


FFN_SUB_BLOCKS = 4
FFN_CAST_SLICES = 2


def _ffn_up_kernel(h_ref, wg_ref, wv_ref, cwg_ref, cwv_ref, cbg_ref, cbv_ref, o_ref, w_scr, u_scr, carry,
                   *, tiles_per_seq):
    mi = pl.program_id(0)
    ni = pl.program_id(1)
    tm = h_ref.shape[0]
    tn = wg_ref.shape[1]
    kdim = h_ref.shape[1]

    def cast_weights(k0, k1):
        w_scr[k0:k1, :tn] = wg_ref[k0:k1, :].astype(w_scr.dtype)
        w_scr[k0:k1, tn:] = wv_ref[k0:k1, :].astype(w_scr.dtype)

    @pl.when((mi % tiles_per_seq) == 0)
    def _():
        carry[ni] = jnp.zeros(carry.shape[1:], F32)

    u_scr[:SUBLANES, :] = carry[ni]
    sub = tm // FFN_SUB_BLOCKS

    def matmul(s, k0=0, k1=None, accumulate=False):
        k1 = kdim if k1 is None else k1
        rows = slice(s * sub, (s + 1) * sub)
        dst = slice(SUBLANES + s * sub, SUBLANES + (s + 1) * sub)
        part = jnp.dot(h_ref[rows, k0:k1], w_scr[k0:k1, :], preferred_element_type=F32)
        u_scr[dst, :] = u_scr[dst, :] + part if accumulate else part

    def conv(s, cols, cw_ref, cb_ref):
        window = u_scr[s * sub:s * sub + SUBLANES + sub, cols]
        taps = [(window if tap == CONV_WIDTH - 1 else pltpu.roll(window, CONV_WIDTH - 1 - tap, axis=0))[SUBLANES:]
                * cw_ref[tap:tap + 1, :] for tap in range(CONV_WIDTH)]
        return cb_ref[...] + sum(taps[1:], taps[0])

    def epilogue(s):
        gate = conv(s, slice(0, tn), cwg_ref, cbg_ref)
        val = conv(s, slice(tn, 2 * tn), cwv_ref, cbv_ref)
        half_gate = 0.5 * gate
        silu = half_gate * jnp.tanh(half_gate) + half_gate
        o_ref[s * sub:(s + 1) * sub, :] = (silu * val).astype(o_ref.dtype)

    kstep = kdim // FFN_CAST_SLICES
    for ks in range(FFN_CAST_SLICES):
        cast_weights(ks * kstep, (ks + 1) * kstep)
        matmul(0, ks * kstep, (ks + 1) * kstep, accumulate=ks > 0)
    for s in range(1, FFN_SUB_BLOCKS):
        matmul(s)
        epilogue(s - 1)
    epilogue(FFN_SUB_BLOCKS - 1)
    carry[ni] = u_scr[tm:, :]


def _ffn_up(h, w_up, conv_w, conv_b, layer, seq, *, tm=2048, tn=256):
    m, d = h.shape
    dff = w_up.shape[2] // 2
    tm = _tile(seq, tm, SUBLANES)
    tn = _tile(dff, tn)
    nt = dff // tn
    assert CONV_WIDTH - 1 <= SUBLANES and tm % (FFN_SUB_BLOCKS * SUBLANES) == 0
    conv_b = conv_b.reshape(conv_b.shape[0], 1, 2 * dff)
    return pl.pallas_call(
        functools.partial(_ffn_up_kernel, tiles_per_seq=seq // tm),
        grid=(m // tm, nt),
        in_specs=[pl.BlockSpec((tm, d), lambda i, j: (i, 0), pipeline_mode=pl.Buffered(1)),
                  pl.BlockSpec((None, d, tn), lambda i, j: (layer, 0, j)),
                  pl.BlockSpec((None, d, tn), lambda i, j: (layer, 0, j + nt)),
                  pl.BlockSpec((None, CONV_WIDTH, tn), lambda i, j: (layer, 0, j)),
                  pl.BlockSpec((None, CONV_WIDTH, tn), lambda i, j: (layer, 0, j + nt)),
                  pl.BlockSpec((None, 1, tn), lambda i, j: (layer, 0, j)),
                  pl.BlockSpec((None, 1, tn), lambda i, j: (layer, 0, j + nt))],
        out_specs=pl.BlockSpec((tm, tn), lambda i, j: (i, j)),
        out_shape=jax.ShapeDtypeStruct((m, dff), BF16),
        scratch_shapes=[pltpu.VMEM((d, 2 * tn), BF16),
                        pltpu.VMEM((SUBLANES + tm, 2 * tn), F32),
                        pltpu.VMEM((nt, SUBLANES, 2 * tn), F32)],
        compiler_params=_params("arbitrary", "arbitrary"),
        name="ffn_up",
    )(h, w_up, w_up, conv_w, conv_w, conv_b, conv_b)


OUT_PROJ_TN = 512


def _pad_cols(w, n):
    return jnp.pad(w, ((0, 0), (0, 0), (0, n - w.shape[2])))


def _dsa_weight_col(j, *, tn):
    nq, niq, nkv2 = A_HEADS * HEAD // tn, IDX_HEADS * HEAD // tn, 2 * A_KV_HEADS * HEAD // tn
    return jnp.where(j < nq, j, jnp.where(j < nq + niq, j + nkv2, j - niq))


def _dsa_mixer(h, x, w_in, w_small, ik_g, ik_b, w_out, slot, bsz, seq):
    a_q, a_kv, a_iq = A_HEADS * HEAD, A_KV_HEADS * HEAD, IDX_HEADS * HEAD
    p = _matmul(h, w_in, n=a_q + a_iq + 2 * a_kv, w_layer=slot, col_block=_dsa_weight_col, out_dtype=BF16)
    p2 = _matmul(h, w_small, w_layer=slot, out_dtype=F32)
    tables, half = _rope_tables(seq, HEAD, HEAD)
    n_rope = a_q + a_iq + a_kv
    pr = _rope(p, seq, tables, half, HEAD, n_cols=n_rope, n_scaled_cols=a_q, scale=HEAD ** -0.5 * LOG2E)
    ik, iw = _idx_prep(p2, seq, ik_g, ik_b, tables, half)
    pr3 = pr.reshape(bsz, seq, n_rope)
    p3 = p.reshape(bsz, seq, p.shape[1])
    n_sel = min(INDEX_TOPK, seq // 4)
    mask = _indexer((pr3, a_q // a_iq), ik.reshape(bsz, seq, HEAD), iw.reshape(bsz, seq, HEAD), n_sel)
    o = _dsa_attention((pr3, 0), (pr3, (a_q + a_iq) // HEAD), (p3, n_rope // HEAD), mask, bsz, seq)
    return _matmul(o.reshape(bsz * seq, a_q), w_out, w_layer=slot, res=x, out_dtype=F32, tn=OUT_PROJ_TN)


def _fox_mixer(h, x, w_in, w_f, f_bias, w_out, slot, bsz, seq):
    d = h.shape[1]
    p = _matmul(h, w_in, n=3 * d, w_layer=slot, out_dtype=BF16)
    f = _matmul(h, w_f, w_layer=slot, out_dtype=F32, tn=HEAD)
    bias = jnp.pad(f_bias.astype(F32), (0, HEAD - B_HEADS)).reshape(1, HEAD)
    cum_t = _fox_cum(f.reshape(bsz, seq, HEAD), bias)
    cum_rows = cum_t[:, :B_HEADS].reshape(bsz, B_HEADS, seq // FOX_KC, 1, FOX_KC)
    o = _fox_attention(p.reshape(bsz, seq, 3 * d), cum_rows, bsz, seq)
    return _matmul(o.reshape(bsz * seq, d), w_out, w_layer=slot, res=x, out_dtype=F32, tn=OUT_PROJ_TN)


def _swa_mixer(h, x, w_in, sinks, w_out, slot, bsz, seq):
    c_q, c_kv = C_HEADS * C_HEAD_DIM, C_KV_HEADS * C_HEAD_DIM
    p = _matmul(h, w_in, w_layer=slot, out_dtype=BF16)
    tables, half = _rope_tables(seq, C_HEAD_DIM, C_HEAD_DIM)
    pr = _rope(p, seq, tables, half, C_HEAD_DIM, n_cols=c_q + c_kv, n_scaled_cols=c_q, scale=C_HEAD_DIM ** -0.5)
    o = _swa_attention(pr.reshape(bsz, seq, c_q + c_kv), p.reshape(bsz, seq, p.shape[1]),
                       sinks.astype(F32), bsz, seq)
    return _matmul(o.reshape(bsz * seq, c_q), w_out, w_layer=slot, res=x, out_dtype=F32, tn=OUT_PROJ_TN)


def _conv_glu_ffn(h, x, w_up, conv_w, conv_b, w_down, layer, seq):
    g = _ffn_up(h, w_up, conv_w, conv_b, layer, seq)
    dff = g.shape[1]
    return _matmul(g, w_down, w_layer=layer, res=x, out_dtype=F32, tm=512, tn=512)


def kernel(x, attn_norm, ffn_norm, final_norm, a_w_in, a_idx_k_norm_g, a_idx_k_norm_b, a_w_out,
           b_w_in, b_f_bias, b_w_out, c_w_in, c_sinks, c_w_out,
           ffn_w_up, ffn_conv_w, ffn_conv_b, ffn_w_down):
    bsz, seq, d = x.shape
    depth = attn_norm.shape[0]
    a_main = A_HEADS * HEAD + 2 * A_KV_HEADS * HEAD + IDX_HEADS * HEAD
    a_w_small = _pad_cols(a_w_in[:, :, a_main:], 2 * HEAD).astype(BF16)
    b_w_f = _pad_cols(b_w_in[:, :, 3 * d:], HEAD).astype(BF16)
    a_w_in, a_w_out, b_w_in, b_w_out, c_w_in, c_w_out, ffn_w_down = (
        w.astype(BF16) for w in (a_w_in, a_w_out, b_w_in, b_w_out, c_w_in, c_w_out, ffn_w_down))
    x = x.reshape(bsz * seq, d)
    for layer in range(depth):
        mixer, slot = layer % N_MIXERS, layer // N_MIXERS
        h = _rmsnorm(x, attn_norm[layer], BF16)
        if mixer == 0:
            x = _dsa_mixer(h, x, a_w_in, a_w_small, a_idx_k_norm_g[slot], a_idx_k_norm_b[slot], a_w_out, slot,
                           bsz, seq)
        elif mixer == 1:
            x = _fox_mixer(h, x, b_w_in, b_w_f, b_f_bias[slot], b_w_out, slot, bsz, seq)
        else:
            x = _swa_mixer(h, x, c_w_in, c_sinks[slot], c_w_out, slot, bsz, seq)
        h = _rmsnorm(x, ffn_norm[layer], BF16)
        x = _conv_glu_ffn(h, x, ffn_w_up, ffn_conv_w, ffn_conv_b, ffn_w_down, layer, seq)
    return _rmsnorm(x, final_norm, F32).reshape(bsz, seq, d)
```

```python
import functools
import math

import jax
import jax.numpy as jnp
import numpy as np
from jax import lax
from jax.experimental import pallas as pl
from jax.experimental.pallas import tpu as pltpu

F32 = jnp.float32
BF16 = jnp.bfloat16

N_MIXERS = 3
ROPE_THETA = 500000.0
ROPE_FRACTION = 4
NORM_EPS = 1e-6
LN_EPS = 1e-6
NEG_INF = -1e30
LOG2E = math.log2(math.e)
HEAD = 128
A_HEADS, A_KV_HEADS, IDX_HEADS, INDEX_TOPK = 32, 8, 16, 256
B_HEADS = 32
C_HEADS, C_KV_HEADS, C_HEAD_DIM, WINDOW = 64, 8, 64, 128
CONV_WIDTH = 3

LANES = 128
SUBLANES = 8
VMEM_LIMIT_BYTES = 56 * 1024 * 1024
INT_MIN = -(2 ** 31)


def _params(*sem):
    return pltpu.CompilerParams(dimension_semantics=sem, vmem_limit_bytes=VMEM_LIMIT_BYTES)


def _tile(dim, target, quantum=LANES):
    if dim <= target:
        return dim
    best = None
    for t in range(quantum, target + 1, quantum):
        if dim % t == 0:
            best = t
    assert best is not None, (dim, target)
    return best


def _rmsnorm_kernel(x_ref, g_ref, o_ref):
    x = x_ref[...]
    y = x * lax.rsqrt(jnp.mean(x * x, axis=-1, keepdims=True) + NORM_EPS)
    o_ref[...] = (y * g_ref[...]).astype(o_ref.dtype)


def _rmsnorm(x, g, out_dtype):
    m, d = x.shape
    tm = _tile(m, 512, SUBLANES)
    return pl.pallas_call(
        _rmsnorm_kernel,
        grid=(m // tm,),
        in_specs=[pl.BlockSpec((tm, d), lambda i: (i, 0)), pl.BlockSpec((1, d), lambda i: (0, 0))],
        out_specs=pl.BlockSpec((tm, d), lambda i: (i, 0)),
        out_shape=jax.ShapeDtypeStruct((m, d), out_dtype),
        compiler_params=_params("parallel"),
        name="rmsnorm",
    )(x, g.reshape(1, d))


def _row_scale(ss_ref, d):
    return lax.rsqrt(jnp.sum(ss_ref[...], axis=1, keepdims=True) * (1.0 / d) + NORM_EPS)


def _mm_kernel(*refs, has_scale, has_res, has_gain):
    refs = list(refs)
    a_ref, w_ref = refs[:2]
    rest = refs[2:]
    ss_ref = rest.pop(0) if has_scale else None
    r_ref = rest.pop(0) if has_res else None
    g_ref = rest.pop(0) if has_gain else None
    o_ref = rest.pop(0)
    out = jnp.dot(a_ref[...], w_ref[...], preferred_element_type=F32)
    if has_scale:
        out = out * _row_scale(ss_ref, a_ref.shape[1])
    if has_res:
        out = out + r_ref[...]
    o_ref[...] = out.astype(o_ref.dtype)
    if has_gain:
        xg_ref, ss_out_ref = rest
        xg_ref[...] = (out * g_ref[...]).astype(xg_ref.dtype)
        squares = _lane_blocks(out * out)
        part = sum(squares[1:], squares[0])
        j = pl.program_id(1)

        @pl.when(j == 0)
        def _():
            ss_out_ref[...] = part

        @pl.when(j > 0)
        def _():
            ss_out_ref[...] += part


def _matmul(a, w, *, res=None, row_ss=None, gain=None, out_dtype, n=None, w_layer=None, col_block=None,
            tm=1024, tn=1024):
    m, kdim = a.shape
    n = w.shape[-1] if n is None else n
    tm = _tile(m, tm, SUBLANES)
    tn = _tile(n, tn)
    col = (lambda j: j) if col_block is None else functools.partial(col_block, tn=tn)
    if w.ndim == 3:
        w_spec = pl.BlockSpec((None, kdim, tn), lambda i, j: (w_layer, 0, col(j)))
    else:
        w_spec = pl.BlockSpec((kdim, tn), lambda i, j: (0, col(j)))
    tile = pl.BlockSpec((tm, tn), lambda i, j: (i, j))
    rows = pl.BlockSpec((tm, LANES), lambda i, j: (i, 0))
    in_specs = [pl.BlockSpec((tm, kdim), lambda i, j: (i, 0)), w_spec]
    args = [a, w]
    if row_ss is not None:
        in_specs.append(rows)
        args.append(row_ss)
    if res is not None:
        in_specs.append(tile)
        args.append(res)
    out_specs, out_shape = tile, jax.ShapeDtypeStruct((m, n), out_dtype)
    if gain is not None:
        in_specs.append(pl.BlockSpec((1, tn), lambda i, j: (0, j)))
        args.append(gain.reshape(1, n))
        out_specs = [tile, tile, rows]
        out_shape = [out_shape, jax.ShapeDtypeStruct((m, n), BF16), jax.ShapeDtypeStruct((m, LANES), F32)]
    return pl.pallas_call(
        functools.partial(_mm_kernel, has_scale=row_ss is not None, has_res=res is not None,
                          has_gain=gain is not None),
        grid=(m // tm, n // tn),
        in_specs=in_specs,
        out_specs=out_specs,
        out_shape=out_shape,
        compiler_params=_params("parallel", "arbitrary" if gain is not None else "parallel"),
        name="matmul",
    )(*args)


def _rope_tables(seq, head_dim, lanes_per_head):
    rot = head_dim // ROPE_FRACTION
    half = rot // 2
    inv_freq = jnp.power(jnp.float32(ROPE_THETA), -jnp.arange(half, dtype=F32) * (2.0 / rot))
    ang = jnp.arange(seq, dtype=F32)[:, None] * inv_freq[None, :]
    cos, sin = jnp.cos(ang), jnp.sin(ang)
    ones = jnp.ones((seq, lanes_per_head - rot), F32)
    zeros_h = jnp.zeros((seq, half), F32)
    zeros_r = jnp.zeros((seq, lanes_per_head - rot), F32)
    c = jnp.concatenate([cos, cos, ones], axis=1)
    a = jnp.concatenate([-sin, zeros_h, zeros_r], axis=1)
    b = jnp.concatenate([zeros_h, sin, zeros_r], axis=1)
    reps = LANES // lanes_per_head
    return tuple(jnp.tile(t, (1, reps)) for t in (c, a, b)), half


def _rope_apply(x, c, a, b, half):
    up = pltpu.roll(x, LANES - half, axis=1)
    dn = pltpu.roll(x, half, axis=1)
    return x * c + up * a + dn * b


def _rope_kernel(x_ref, c_ref, ab_ref, swap_ref, o_ref, *, n_scaled, scale):
    j = pl.program_id(1)
    s = jnp.where(j < n_scaled, jnp.float32(scale), jnp.float32(1.0))
    c, ab, swap = c_ref[...], ab_ref[...], swap_ref[...]
    for blk in range(x_ref.shape[1] // LANES):
        sl = slice(blk * LANES, (blk + 1) * LANES)
        x = x_ref[:, sl]
        partner = jnp.dot(x, swap, preferred_element_type=F32)
        o_ref[:, sl] = ((x.astype(F32) * c + partner * ab) * s).astype(o_ref.dtype)


def _rope(p, seq, tables, half, lanes_per_head, *, n_cols, n_scaled_cols, scale, tc=1024):
    m = p.shape[0]
    tm = _tile(seq, 1024, SUBLANES)
    tc = _tile(math.gcd(n_cols, n_scaled_cols) if n_scaled_cols else n_cols, tc)
    sb = seq // tm
    c, a, b = tables
    swap = np.zeros((LANES, LANES), np.float32)
    for head in range(0, LANES, lanes_per_head):
        for r in range(half):
            swap[head + r + half, head + r] = 1.0
            swap[head + r, head + r + half] = 1.0
    tab_spec = pl.BlockSpec((tm, LANES), lambda i, j: (i % sb, 0))
    return pl.pallas_call(
        functools.partial(_rope_kernel, n_scaled=n_scaled_cols // tc, scale=scale),
        grid=(m // tm, n_cols // tc),
        in_specs=[pl.BlockSpec((tm, tc), lambda i, j: (i, j)), tab_spec, tab_spec,
                  pl.BlockSpec((LANES, LANES), lambda i, j: (0, 0))],
        out_specs=pl.BlockSpec((tm, tc), lambda i, j: (i, j)),
        out_shape=jax.ShapeDtypeStruct((m, n_cols), p.dtype),
        compiler_params=_params("parallel", "parallel"),
        name="rope",
    )(p, c, a + b, jnp.asarray(swap, p.dtype))


def _idx_prep_kernel(p_ref, g_ref, beta_ref, c_ref, a_ref, b_ref, ik_ref, iw_ref, *, half, iw_scale):
    x = p_ref[:, :HEAD]
    mu = jnp.mean(x, axis=-1, keepdims=True)
    var = jnp.mean(jnp.square(x - mu), axis=-1, keepdims=True)
    y = (x - mu) * lax.rsqrt(var + LN_EPS)
    y = y * g_ref[...] + beta_ref[...]
    ik_ref[...] = _rope_apply(y, c_ref[...], a_ref[...], b_ref[...], half).astype(ik_ref.dtype)
    iw_ref[...] = p_ref[:, HEAD:] * jnp.float32(iw_scale)


def _idx_prep(p2, seq, ik_g, ik_b, tables, half):
    m = p2.shape[0]
    tm = _tile(seq, 512, SUBLANES)
    sb = seq // tm
    tab_spec = pl.BlockSpec((tm, LANES), lambda i: (i % sb, 0))
    vec_spec = pl.BlockSpec((1, HEAD), lambda i: (0, 0))
    blk = pl.BlockSpec((tm, HEAD), lambda i: (i, 0))
    return pl.pallas_call(
        functools.partial(_idx_prep_kernel, half=half, iw_scale=IDX_HEADS ** -0.5 * HEAD ** -0.5),
        grid=(m // tm,),
        in_specs=[pl.BlockSpec((tm, 2 * HEAD), lambda i: (i, 0)), vec_spec, vec_spec, tab_spec, tab_spec, tab_spec],
        out_specs=[blk, blk],
        out_shape=[jax.ShapeDtypeStruct((m, HEAD), BF16), jax.ShapeDtypeStruct((m, HEAD), F32)],
        compiler_params=_params("parallel"),
        name="idx_prep",
    )(p2, ik_g.reshape(1, HEAD), ik_b.reshape(1, HEAD), *tables)


DSA_TQ = 128
DSA_KC = 512
IDX_TQ = 128


def _indexer_kernel(iq_ref, ik_ref, iw_ref, mask_ref, key_ref, *, n_sel, n_chunks):
    qb = pl.program_id(1)
    tq, kc = IDX_TQ, DSA_KC
    nch = (qb * tq + tq - 1) // kc + 1
    idx_bits = (n_chunks * kc - 1).bit_length()
    iw = iw_ref[0]
    q_pos = qb * tq + lax.broadcasted_iota(jnp.int32, (tq, 1), 0)
    lane = lax.broadcasted_iota(jnp.int32, (1, kc), 1)

    def score_chunk(c, _):
        ikc = ik_ref[0, pl.ds(pl.multiple_of(c * kc, kc), kc), :]
        score = jnp.zeros((tq, kc), F32)
        for h in range(IDX_HEADS):
            rel = lax.dot_general(iq_ref[0, :, h * HEAD:(h + 1) * HEAD], ikc, (((1,), (1,)), ((), ())),
                                  preferred_element_type=F32)
            score = score + jnp.maximum(rel, 0.0) * iw[:, h:h + 1]
        bits = pltpu.bitcast(score, jnp.int32)
        key = jnp.where(bits < 0, bits ^ jnp.int32(0x7FFFFFFF), bits)
        key = jnp.where(bits == jnp.int32(INT_MIN), jnp.int32(0), key)
        key = jnp.where(c * kc + lane <= q_pos, key, jnp.int32(INT_MIN))
        key_ref[c] = key
        return 0

    lax.fori_loop(0, nch, score_chunk, 0)

    lane128 = lax.broadcasted_iota(jnp.int32, (1, LANES), 1)

    def for_blocks(c, fn):
        key = key_ref[c]
        return [fn(key[:, b * LANES:(b + 1) * LANES], c * kc + b * LANES + lane128) for b in range(kc // LANES)]

    def count(pred):
        def body(c, acc):
            for hit in for_blocks(c, lambda key, pos: jnp.where(pred(key, pos), jnp.int32(1), jnp.int32(0))):
                acc = acc + hit
            return acc
        acc = lax.fori_loop(0, nch, body, jnp.zeros((tq, LANES), jnp.int32))
        return jnp.broadcast_to(jnp.sum(acc, axis=1, keepdims=True), (tq, LANES))

    def write_mask(sel_fn):
        def store(c, tile):
            for r in range(tq // DSA_TQ):
                mask_ref[0, r, c] = tile[r * DSA_TQ:(r + 1) * DSA_TQ].astype(mask_ref.dtype)

        def body(c, _):
            store(c, jnp.concatenate(for_blocks(c, lambda key, pos: jnp.where(sel_fn(key, pos), 1.0, 0.0)), axis=1))
            return 0
        lax.fori_loop(0, nch, body, 0)

        def tail(c, _):
            store(c, jnp.zeros((tq, kc), F32))
            return 0
        lax.fori_loop(nch, n_chunks, tail, 0)

    cnt0 = count(lambda key, pos: key >= 0)
    nonneg = cnt0 >= n_sel
    t0 = jnp.where(nonneg, jnp.int32(0), jnp.int32(INT_MIN))
    n0 = jnp.where(nonneg, cnt0, jnp.broadcast_to(q_pos + 1, (tq, LANES)))
    n_bits = 31
    bits_per_check = 4

    def bit_step(i, t, n_ge):
        bit = jnp.where(i < n_bits, jnp.left_shift(jnp.int32(1), jnp.maximum(jnp.int32(n_bits - 1) - i, 0)), 0)
        cand = t | bit
        cnt = count(lambda key, pos: key >= cand)
        take = cnt >= n_sel
        return jnp.where(take, cand, t), jnp.where(take, cnt, n_ge)

    def search_cond(state):
        i, _, n_ge = state
        return jnp.logical_and(i < n_bits, jnp.max(n_ge) > n_sel)

    def search_body(state):
        i, t, n_ge = state
        for b in range(bits_per_check):
            t, n_ge = bit_step(i + b, t, n_ge)
        return i + bits_per_check, t, n_ge

    _, thr, n_ge = lax.while_loop(search_cond, search_body, (jnp.int32(0), t0, n0))
    real = thr != jnp.int32(INT_MIN)
    excess = jnp.max(jnp.where(jnp.logical_and(real, n_ge > n_sel), 1, 0)) > 0

    @pl.when(jnp.logical_not(excess))
    def _():
        thr_eff = jnp.maximum(thr, jnp.int32(INT_MIN + 1))
        write_mask(lambda key, pos: key >= thr_eff)

    @pl.when(excess)
    def _():
        need = n_sel - count(lambda key, pos: key > thr)

        def idx_step(i, j):
            cand = j | jnp.left_shift(jnp.int32(1), jnp.int32(idx_bits - 1) - i)
            cnt = count(lambda key, pos: jnp.logical_and(key == thr, pos < cand))
            return jnp.where(cnt < need, cand, j)

        last = lax.fori_loop(0, idx_bits, idx_step, jnp.zeros((tq, LANES), jnp.int32))
        write_mask(lambda key, pos: jnp.logical_or(
            key > thr, jnp.logical_and(jnp.logical_and(key == thr, pos <= last), real)))


def _indexer(iq, ik, iw, n_sel):
    bsz, seq = ik.shape[0], ik.shape[1]
    n_chunks = seq // DSA_KC
    nq = seq // DSA_TQ
    per_step = IDX_TQ // DSA_TQ
    assert seq % IDX_TQ == 0 and IDX_TQ % DSA_TQ == 0
    iq_arr, iq_blk = iq
    return pl.pallas_call(
        functools.partial(_indexer_kernel, n_sel=n_sel, n_chunks=n_chunks),
        grid=(bsz, seq // IDX_TQ),
        in_specs=[pl.BlockSpec((1, IDX_TQ, IDX_HEADS * HEAD), lambda b, q: (b, q, iq_blk)),
                  pl.BlockSpec((1, seq, HEAD), lambda b, q: (b, 0, 0)),
                  pl.BlockSpec((1, IDX_TQ, HEAD), lambda b, q: (b, q, 0))],
        out_specs=pl.BlockSpec((1, per_step, n_chunks, DSA_TQ, DSA_KC), lambda b, q: (b, q, 0, 0, 0)),
        out_shape=jax.ShapeDtypeStruct((bsz, nq, n_chunks, DSA_TQ, DSA_KC), BF16),
        scratch_shapes=[pltpu.VMEM((n_chunks, IDX_TQ, DSA_KC), jnp.int32)],
        compiler_params=_params("parallel", "arbitrary"),
        name="dsa_indexer",
    )(iq_arr, ik, iw)


def _lane_blocks(x):
    return [x[:, j * LANES:(j + 1) * LANES] for j in range(x.shape[1] // LANES)]


STREAMS = 2


def _two_sweep_attention(nch, rows, logits_fn, pv_fn, s_ref, mx_ref, ls_ref, acc_ref):
    mx_ref[...] = jnp.full(mx_ref.shape, -jnp.inf, F32)

    def sweep1_chunk(c, last):
        for i, logits2 in enumerate(logits_fn(c, last)):
            sl = slice(i * rows, (i + 1) * rows)
            s_ref[c, sl, :] = logits2
            mx = mx_ref[sl, :]
            for blk in _lane_blocks(logits2):
                mx = jnp.maximum(mx, blk)
            mx_ref[sl, :] = mx

    def pairs_then_rest(n, chunk_fn):
        def run(first, count, width):
            def trip(t, _):
                for k in range(width):
                    chunk_fn(first + width * t + k)
                return 0
            lax.fori_loop(0, count, trip, 0)

        triples = n // 3
        rest = n - 3 * triples
        run(0, triples, 3)
        run(3 * triples, rest // 2, 2)
        run(3 * triples + 2 * (rest // 2), rest % 2, 1)

    pairs_then_rest(nch - 1, lambda c: sweep1_chunk(c, False))
    sweep1_chunk(nch - 1, True)

    row_max = jnp.max(mx_ref[...], axis=1, keepdims=True)
    mx_ref[...] = jnp.broadcast_to(row_max, mx_ref.shape)
    ls_ref[...] = jnp.zeros(ls_ref.shape, F32)
    acc_ref[...] = jnp.zeros(acc_ref.shape, F32)

    def sweep2_chunk(c):
        for i in range(STREAMS):
            sl = slice(i * rows, (i + 1) * rows)
            m = mx_ref[sl, :]
            ps = [jnp.exp2(blk - m) for blk in _lane_blocks(s_ref[c, sl, :])]
            ls_ref[sl, :] += sum(ps[1:], ps[0])
            acc_ref[sl, :] += pv_fn(c, i, jnp.concatenate(ps, axis=1).astype(BF16))

    pairs_then_rest(nch, sweep2_chunk)
    return acc_ref[...] / jnp.sum(ls_ref[...], axis=1, keepdims=True)


def _softmax_scratch(rows, n_chunks, kc):
    return [pltpu.VMEM((n_chunks, rows, kc), F32),
            pltpu.VMEM((rows, LANES), F32),
            pltpu.VMEM((rows, LANES), F32),
            pltpu.VMEM((rows, HEAD), F32)]


def _dsa_attn_kernel(q_ref, k_ref, v_ref, mask_ref, o_ref, s_ref, mx_ref, ls_ref, acc_ref, *, groups):
    qb = pl.program_id(2)
    tq, kc = DSA_TQ, DSA_KC
    nch = (qb * tq) // kc + 1
    rows = groups * tq
    qs = [jnp.concatenate([q_ref[0, :, (i * groups + r) * HEAD:(i * groups + r + 1) * HEAD] for r in range(groups)],
                          axis=0) for i in range(STREAMS)]

    def logits_fn(c, last):
        start = pl.multiple_of(c * kc, kc)
        sel = (mask_ref[0, 0, c].astype(F32) > 0.5)[None]
        out = []
        for i in range(STREAMS):
            k_c = k_ref[0, pl.ds(start, kc), i * HEAD:(i + 1) * HEAD]
            raw = lax.dot_general(qs[i], k_c, (((1,), (1,)), ((), ())), preferred_element_type=F32)
            out.append(jnp.where(sel, raw.reshape(groups, tq, kc), NEG_INF).reshape(rows, kc))
        return out

    def pv_fn(c, i, p):
        v_c = v_ref[0, pl.ds(pl.multiple_of(c * kc, kc), kc), i * HEAD:(i + 1) * HEAD]
        return jnp.dot(p, v_c, preferred_element_type=F32)

    out = _two_sweep_attention(nch, rows, logits_fn, pv_fn, s_ref, mx_ref, ls_ref, acc_ref)
    for r in range(STREAMS * groups):
        o_ref[0, :, r * HEAD:(r + 1) * HEAD] = out[r * tq:(r + 1) * tq].astype(o_ref.dtype)


def _dsa_attention(q, k, v, mask, bsz, seq):
    groups = A_HEADS // A_KV_HEADS
    (q_arr, q0), (k_arr, k0), (v_arr, v0) = q, k, v
    nq = seq // DSA_TQ
    n_chunks = seq // DSA_KC
    qw = STREAMS * groups * HEAD
    kw = STREAMS * HEAD
    assert q0 % (STREAMS * groups) == 0 and k0 % STREAMS == 0 and v0 % STREAMS == 0
    return pl.pallas_call(
        functools.partial(_dsa_attn_kernel, groups=groups),
        grid=(bsz, A_KV_HEADS // STREAMS, nq),
        in_specs=[pl.BlockSpec((1, DSA_TQ, qw), lambda b, g, i: (b, i, q0 // (STREAMS * groups) + g)),
                  pl.BlockSpec((1, seq, kw), lambda b, g, i: (b, 0, k0 // STREAMS + g)),
                  pl.BlockSpec((1, seq, kw), lambda b, g, i: (b, 0, v0 // STREAMS + g)),
                  pl.BlockSpec((1, 1, n_chunks, DSA_TQ, DSA_KC), lambda b, g, i: (b, i, 0, 0, 0))],
        out_specs=pl.BlockSpec((1, DSA_TQ, qw), lambda b, g, i: (b, i, g)),
        out_shape=jax.ShapeDtypeStruct((bsz, seq, A_HEADS * HEAD), BF16),
        scratch_shapes=_softmax_scratch(STREAMS * groups * DSA_TQ, n_chunks, DSA_KC),
        compiler_params=_params("parallel", "parallel", "arbitrary"),
        name="dsa_attention",
    )(q_arr, k_arr, v_arr, mask)


def _fox_cum_kernel(f_ref, bias_ref, cum_t_ref):
    z = f_ref[0] + bias_ref[...]
    x = jnp.minimum(z, 0.0) - jnp.log1p(jnp.exp(-jnp.abs(z)))
    seq = x.shape[0]
    row = lax.broadcasted_iota(jnp.int32, (seq, 1), 0)
    d = 1
    while d < seq:
        x = x + jnp.where(row >= d, pltpu.roll(x, d, axis=0), 0.0)
        d *= 2
    cum_t_ref[0] = (x * jnp.float32(LOG2E)).T


def _fox_cum(f, bias):
    bsz, seq, _ = f.shape
    return pl.pallas_call(
        _fox_cum_kernel,
        grid=(bsz,),
        in_specs=[pl.BlockSpec((1, seq, HEAD), lambda b: (b, 0, 0)), pl.BlockSpec((1, HEAD), lambda b: (0, 0))],
        out_specs=pl.BlockSpec((1, HEAD, seq), lambda b: (b, 0, 0)),
        out_shape=jax.ShapeDtypeStruct((bsz, HEAD, seq), F32),
        compiler_params=_params("parallel"),
        name="fox_cum",
    )(f, bias)


FOX_TQ = 512
FOX_KC = 512


def _fox_attn_kernel(q_ref, k_ref, v_ref, cq_ref, ck_ref, o_ref, s_ref, mx_ref, ls_ref, acc_ref, *, scale2):
    qb = pl.program_id(2)
    tq, kc = FOX_TQ, FOX_KC
    heads = [slice(i * HEAD, (i + 1) * HEAD) for i in range(STREAMS)]
    qs = [(q_ref[0, :, h].astype(F32) * jnp.float32(scale2)).astype(q_ref.dtype) for h in heads]
    cqs = [jnp.broadcast_to(cq_ref[0, i, 0], (LANES, tq)).T for i in range(STREAMS)]

    def logits_fn(c, last):
        start = pl.multiple_of(c * kc, kc)
        out = []
        for i in range(STREAMS):
            raw = lax.dot_general(qs[i], k_ref[0, pl.ds(start, kc), heads[i]], (((1,), (1,)), ((), ())),
                                  preferred_element_type=F32)
            ck = ck_ref[0, i, c]
            blocks = [blk + cqs[i] - ck[:, j * LANES:(j + 1) * LANES] for j, blk in enumerate(_lane_blocks(raw))]
            logits2 = jnp.concatenate(blocks, axis=1)
            if last:
                row = lax.broadcasted_iota(jnp.int32, (tq, 1), 0)
                lane = lax.broadcasted_iota(jnp.int32, (1, kc), 1)
                logits2 = jnp.where(lane <= row, logits2, NEG_INF)
            out.append(logits2)
        return out

    def pv_fn(c, i, p):
        v_c = v_ref[0, pl.ds(pl.multiple_of(c * kc, kc), kc), heads[i]]
        return jnp.dot(p, v_c, preferred_element_type=F32)

    out = _two_sweep_attention(qb + 1, tq, logits_fn, pv_fn, s_ref, mx_ref, ls_ref, acc_ref)
    for i in range(STREAMS):
        o_ref[0, :, heads[i]] = out[i * tq:(i + 1) * tq].astype(o_ref.dtype)


def _fox_attention(p, cum_rows, bsz, seq):
    tq = FOX_TQ
    assert FOX_TQ == FOX_KC and seq % tq == 0
    n_chunks = seq // FOX_KC
    hw = STREAMS * HEAD
    cq_rows = cum_rows.reshape(bsz, B_HEADS, 1, 1, seq)
    return pl.pallas_call(
        functools.partial(_fox_attn_kernel, scale2=HEAD ** -0.5 * LOG2E),
        grid=(bsz, B_HEADS // STREAMS, seq // tq),
        in_specs=[pl.BlockSpec((1, tq, hw), lambda b, h, i: (b, i, h)),
                  pl.BlockSpec((1, seq, hw), lambda b, h, i: (b, 0, B_HEADS // STREAMS + h)),
                  pl.BlockSpec((1, seq, hw), lambda b, h, i: (b, 0, 2 * B_HEADS // STREAMS + h)),
                  pl.BlockSpec((1, STREAMS, 1, 1, tq), lambda b, h, i: (b, h, 0, 0, i)),
                  pl.BlockSpec((1, STREAMS, n_chunks, 1, FOX_KC), lambda b, h, i: (b, h, 0, 0, 0))],
        out_specs=pl.BlockSpec((1, tq, hw), lambda b, h, i: (b, i, h)),
        out_shape=jax.ShapeDtypeStruct((bsz, seq, B_HEADS * HEAD), BF16),
        scratch_shapes=_softmax_scratch(STREAMS * tq, n_chunks, FOX_KC),
        compiler_params=_params("parallel", "parallel", "arbitrary"),
        name="fox_attention",
    )(p, p, p, cq_rows, cum_rows)


def _swa_kernel(sinks_ref, q_ref, kp_ref, kc_ref, vp_ref, vc_ref, o_ref):
    n = pl.program_id(1)
    pair = pl.program_id(2)
    w, dh = WINDOW, C_HEAD_DIM
    groups = C_HEADS // C_KV_HEADS
    kv_per_blk = LANES // dh
    i = lax.broadcasted_iota(jnp.int32, (w, 1), 0)
    j = lax.broadcasted_iota(jnp.int32, (1, 2 * w), 1)
    valid = jnp.logical_and(jnp.logical_and(j > i, j <= i + w), n * w + j >= w)
    kk = jnp.concatenate([kp_ref[0], kc_ref[0]], axis=0)
    vv = jnp.concatenate([vp_ref[0], vc_ref[0]], axis=0)
    for g in range(kv_per_blk):
        k_g = kk[:, g * dh:(g + 1) * dh]
        v_g = vv[:, g * dh:(g + 1) * dh]
        for r in range(groups):
            col = (g * groups + r) * dh
            q = q_ref[0, :, col:col + dh]
            s = lax.dot_general(q, k_g, (((1,), (1,)), ((), ())), preferred_element_type=F32)
            logits = jnp.where(valid, s, NEG_INF)
            sink = sinks_ref[(pair * kv_per_blk + g) * groups + r]
            m = jnp.maximum(jnp.max(logits, axis=1, keepdims=True), sink)
            p = jnp.exp(logits - m)
            denom = jnp.sum(p, axis=1, keepdims=True) + jnp.exp(sink - m)
            o = jnp.dot((p / denom).astype(v_g.dtype), v_g, preferred_element_type=F32)
            o_ref[0, :, col:col + dh] = o.astype(o_ref.dtype)


def _swa_attention(pr, p, sinks, bsz, seq):
    w = WINDOW
    nb = seq // w
    kv_per_blk = LANES // C_HEAD_DIM
    pairs = C_KV_HEADS // kv_per_blk
    qw = kv_per_blk * (C_HEADS // C_KV_HEADS) * C_HEAD_DIM
    k0 = C_HEADS * C_HEAD_DIM // LANES
    v0 = k0 + C_KV_HEADS * C_HEAD_DIM // LANES
    prev = lambda col0: (lambda b, n, g: (b, jnp.maximum(n - 1, 0), col0 + g))
    cur = lambda col0: (lambda b, n, g: (b, n, col0 + g))
    blk = (1, w, LANES)
    return pl.pallas_call(
        _swa_kernel,
        grid=(bsz, nb, pairs),
        in_specs=[pl.BlockSpec(memory_space=pltpu.SMEM),
                  pl.BlockSpec((1, w, qw), lambda b, n, g: (b, n, g)),
                  pl.BlockSpec(blk, prev(k0)), pl.BlockSpec(blk, cur(k0)),
                  pl.BlockSpec(blk, prev(v0)), pl.BlockSpec(blk, cur(v0))],
        out_specs=pl.BlockSpec((1, w, qw), lambda b, n, g: (b, n, g)),
        out_shape=jax.ShapeDtypeStruct((bsz, seq, C_HEADS * C_HEAD_DIM), BF16),
        compiler_params=_params("parallel", "parallel", "parallel"),
        name="swa_attention",
    )(sinks, pr, pr, pr, p, p)


FFN_SUB_BLOCKS = 4
FFN_CAST_SLICES = 2


def _ffn_up_kernel(h_ref, wg_ref, wv_ref, cwg_ref, cwv_ref, cbg_ref, cbv_ref, o_ref, w_scr, u_scr, carry,
                   *, tiles_per_seq):
    mi = pl.program_id(0)
    ni = pl.program_id(1)
    tm = h_ref.shape[0]
    tn = wg_ref.shape[1]
    kdim = h_ref.shape[1]

    def cast_weights(k0, k1):
        w_scr[k0:k1, :tn] = wg_ref[k0:k1, :].astype(w_scr.dtype)
        w_scr[k0:k1, tn:] = wv_ref[k0:k1, :].astype(w_scr.dtype)

    @pl.when((mi % tiles_per_seq) == 0)
    def _():
        carry[ni] = jnp.zeros(carry.shape[1:], F32)

    u_scr[:SUBLANES, :] = carry[ni]
    sub = tm // FFN_SUB_BLOCKS

    def matmul(s, k0=0, k1=None, accumulate=False):
        k1 = kdim if k1 is None else k1
        rows = slice(s * sub, (s + 1) * sub)
        dst = slice(SUBLANES + s * sub, SUBLANES + (s + 1) * sub)
        part = jnp.dot(h_ref[rows, k0:k1], w_scr[k0:k1, :], preferred_element_type=F32)
        u_scr[dst, :] = u_scr[dst, :] + part if accumulate else part

    def conv(s, cols, cw_ref, cb_ref):
        window = u_scr[s * sub:s * sub + SUBLANES + sub, cols]
        taps = [(window if tap == CONV_WIDTH - 1 else pltpu.roll(window, CONV_WIDTH - 1 - tap, axis=0))[SUBLANES:]
                * cw_ref[tap:tap + 1, :] for tap in range(CONV_WIDTH)]
        return cb_ref[...] + sum(taps[1:], taps[0])

    def epilogue(s):
        gate = conv(s, slice(0, tn), cwg_ref, cbg_ref)
        val = conv(s, slice(tn, 2 * tn), cwv_ref, cbv_ref)
        half_gate = 0.5 * gate
        silu = half_gate * jnp.tanh(half_gate) + half_gate
        o_ref[s * sub:(s + 1) * sub, :] = (silu * val).astype(o_ref.dtype)

    kstep = kdim // FFN_CAST_SLICES
    for ks in range(FFN_CAST_SLICES):
        cast_weights(ks * kstep, (ks + 1) * kstep)
        matmul(0, ks * kstep, (ks + 1) * kstep, accumulate=ks > 0)
    for s in range(1, FFN_SUB_BLOCKS):
        matmul(s)
        epilogue(s - 1)
    epilogue(FFN_SUB_BLOCKS - 1)
    carry[ni] = u_scr[tm:, :]


def _ffn_up(h, w_up, conv_w, conv_b, layer, seq, *, tm=2048, tn=256):
    m, d = h.shape
    dff = w_up.shape[2] // 2
    tm = _tile(seq, tm, SUBLANES)
    tn = _tile(dff, tn)
    nt = dff // tn
    assert CONV_WIDTH - 1 <= SUBLANES and tm % (FFN_SUB_BLOCKS * SUBLANES) == 0
    conv_b = conv_b.reshape(conv_b.shape[0], 1, 2 * dff)
    return pl.pallas_call(
        functools.partial(_ffn_up_kernel, tiles_per_seq=seq // tm),
        grid=(m // tm, nt),
        in_specs=[pl.BlockSpec((tm, d), lambda i, j: (i, 0), pipeline_mode=pl.Buffered(1)),
                  pl.BlockSpec((None, d, tn), lambda i, j: (layer, 0, j)),
                  pl.BlockSpec((None, d, tn), lambda i, j: (layer, 0, j + nt)),
                  pl.BlockSpec((None, CONV_WIDTH, tn), lambda i, j: (layer, 0, j)),
                  pl.BlockSpec((None, CONV_WIDTH, tn), lambda i, j: (layer, 0, j + nt)),
                  pl.BlockSpec((None, 1, tn), lambda i, j: (layer, 0, j)),
                  pl.BlockSpec((None, 1, tn), lambda i, j: (layer, 0, j + nt))],
        out_specs=pl.BlockSpec((tm, tn), lambda i, j: (i, j)),
        out_shape=jax.ShapeDtypeStruct((m, dff), BF16),
        scratch_shapes=[pltpu.VMEM((d, 2 * tn), BF16),
                        pltpu.VMEM((SUBLANES + tm, 2 * tn), F32),
                        pltpu.VMEM((nt, SUBLANES, 2 * tn), F32)],
        compiler_params=_params("arbitrary", "arbitrary"),
        name="ffn_up",
    )(h, w_up, w_up, conv_w, conv_w, conv_b, conv_b)


OUT_PROJ_TN = 512


def _pad_cols(w, n):
    return jnp.pad(w, ((0, 0), (0, 0), (0, n - w.shape[2])))


def _dsa_weight_col(j, *, tn):
    nq, niq, nkv2 = A_HEADS * HEAD // tn, IDX_HEADS * HEAD // tn, 2 * A_KV_HEADS * HEAD // tn
    return jnp.where(j < nq, j, jnp.where(j < nq + niq, j + nkv2, j - niq))


def _dsa_mixer(h, row_ss, x, w_in, w_small, ik_g, ik_b, w_out, slot, bsz, seq):
    a_q, a_kv, a_iq = A_HEADS * HEAD, A_KV_HEADS * HEAD, IDX_HEADS * HEAD
    p = _matmul(h, w_in, row_ss=row_ss, n=a_q + a_iq + 2 * a_kv, w_layer=slot, col_block=_dsa_weight_col,
                out_dtype=BF16)
    p2 = _matmul(h, w_small, row_ss=row_ss, w_layer=slot, out_dtype=F32)
    tables, half = _rope_tables(seq, HEAD, HEAD)
    n_rope = a_q + a_iq + a_kv
    pr = _rope(p, seq, tables, half, HEAD, n_cols=n_rope, n_scaled_cols=a_q, scale=HEAD ** -0.5 * LOG2E)
    ik, iw = _idx_prep(p2, seq, ik_g, ik_b, tables, half)
    pr3 = pr.reshape(bsz, seq, n_rope)
    p3 = p.reshape(bsz, seq, p.shape[1])
    n_sel = min(INDEX_TOPK, seq // 4)
    mask = _indexer((pr3, a_q // a_iq), ik.reshape(bsz, seq, HEAD), iw.reshape(bsz, seq, HEAD), n_sel)
    o = _dsa_attention((pr3, 0), (pr3, (a_q + a_iq) // HEAD), (p3, n_rope // HEAD), mask, bsz, seq)
    return _matmul(o.reshape(bsz * seq, a_q), w_out, w_layer=slot, res=x, out_dtype=F32, tn=OUT_PROJ_TN)


def _fox_mixer(h, row_ss, x, w_in, w_f, f_bias, w_out, slot, bsz, seq):
    d = h.shape[1]
    p = _matmul(h, w_in, row_ss=row_ss, n=3 * d, w_layer=slot, out_dtype=BF16)
    f = _matmul(h, w_f, row_ss=row_ss, w_layer=slot, out_dtype=F32, tn=HEAD)
    bias = jnp.pad(f_bias.astype(F32), (0, HEAD - B_HEADS)).reshape(1, HEAD)
    cum_t = _fox_cum(f.reshape(bsz, seq, HEAD), bias)
    cum_rows = cum_t[:, :B_HEADS].reshape(bsz, B_HEADS, seq // FOX_KC, 1, FOX_KC)
    o = _fox_attention(p.reshape(bsz, seq, 3 * d), cum_rows, bsz, seq)
    return _matmul(o.reshape(bsz * seq, d), w_out, w_layer=slot, res=x, out_dtype=F32, tn=OUT_PROJ_TN)


def _swa_mixer(h, row_ss, x, w_in, sinks, w_out, slot, bsz, seq):
    c_q, c_kv = C_HEADS * C_HEAD_DIM, C_KV_HEADS * C_HEAD_DIM
    p = _matmul(h, w_in, row_ss=row_ss, w_layer=slot, out_dtype=BF16)
    tables, half = _rope_tables(seq, C_HEAD_DIM, C_HEAD_DIM)
    pr = _rope(p, seq, tables, half, C_HEAD_DIM, n_cols=c_q + c_kv, n_scaled_cols=c_q, scale=C_HEAD_DIM ** -0.5)
    o = _swa_attention(pr.reshape(bsz, seq, c_q + c_kv), p.reshape(bsz, seq, p.shape[1]),
                       sinks.astype(F32), bsz, seq)
    return _matmul(o.reshape(bsz * seq, c_q), w_out, w_layer=slot, res=x, out_dtype=F32, tn=OUT_PROJ_TN)


def _conv_glu_ffn(h, x, w_up, conv_w, conv_b, w_down, layer, next_gain, seq):
    g = _ffn_up(h, w_up, conv_w, conv_b, layer, seq)
    return _matmul(g, w_down, w_layer=layer, res=x, gain=next_gain, out_dtype=F32, tm=512, tn=512)


def kernel(x, attn_norm, ffn_norm, final_norm, a_w_in, a_idx_k_norm_g, a_idx_k_norm_b, a_w_out,
           b_w_in, b_f_bias, b_w_out, c_w_in, c_sinks, c_w_out,
           ffn_w_up, ffn_conv_w, ffn_conv_b, ffn_w_down):
    bsz, seq, d = x.shape
    depth = attn_norm.shape[0]
    a_main = A_HEADS * HEAD + 2 * A_KV_HEADS * HEAD + IDX_HEADS * HEAD
    a_w_small = _pad_cols(a_w_in[:, :, a_main:], 2 * HEAD).astype(BF16)
    b_w_f = _pad_cols(b_w_in[:, :, 3 * d:], HEAD).astype(BF16)
    a_w_in, a_w_out, b_w_in, b_w_out, c_w_in, c_w_out, ffn_w_down = (
        w.astype(BF16) for w in (a_w_in, a_w_out, b_w_in, b_w_out, c_w_in, c_w_out, ffn_w_down))
    x = x.reshape(bsz * seq, d)
    h, row_ss = _rmsnorm(x, attn_norm[0], BF16), None
    for layer in range(depth):
        mixer, slot = layer % N_MIXERS, layer // N_MIXERS
        if mixer == 0:
            x = _dsa_mixer(h, row_ss, x, a_w_in, a_w_small, a_idx_k_norm_g[slot], a_idx_k_norm_b[slot], a_w_out,
                           slot, bsz, seq)
        elif mixer == 1:
            x = _fox_mixer(h, row_ss, x, b_w_in, b_w_f, b_f_bias[slot], b_w_out, slot, bsz, seq)
        else:
            x = _swa_mixer(h, row_ss, x, c_w_in, c_sinks[slot], c_w_out, slot, bsz, seq)
        h = _rmsnorm(x, ffn_norm[layer], BF16)
        if layer + 1 < depth:
            x, h, row_ss = _conv_glu_ffn(h, x, ffn_w_up, ffn_conv_w, ffn_conv_b, ffn_w_down, layer,
                                         attn_norm[layer + 1], seq)
        else:
            x = _conv_glu_ffn(h, x, ffn_w_up, ffn_conv_w, ffn_conv_b, ffn_w_down, layer, None, seq)
    return _rmsnorm(x, final_norm, F32).reshape(bsz, seq, d)
```

```python
import functools
import math

import jax
import jax.numpy as jnp
import numpy as np
from jax import lax
from jax.experimental import pallas as pl
from jax.experimental.pallas import tpu as pltpu

F32 = jnp.float32
BF16 = jnp.bfloat16

N_MIXERS = 3
ROPE_THETA = 500000.0
ROPE_FRACTION = 4
NORM_EPS = 1e-6
LN_EPS = 1e-6
NEG_INF = -1e30
LOG2E = math.log2(math.e)
HEAD = 128
A_HEADS, A_KV_HEADS, IDX_HEADS, INDEX_TOPK = 32, 8, 16, 256
B_HEADS = 32
C_HEADS, C_KV_HEADS, C_HEAD_DIM, WINDOW = 64, 8, 64, 128
CONV_WIDTH = 3

LANES = 128
SUBLANES = 8
VMEM_LIMIT_BYTES = 56 * 1024 * 1024
INT_MIN = -(2 ** 31)


def _params(*sem):
    return pltpu.CompilerParams(dimension_semantics=sem, vmem_limit_bytes=VMEM_LIMIT_BYTES)


def _tile(dim, target, quantum=LANES):
    if dim <= target:
        return dim
    best = None
    for t in range(quantum, target + 1, quantum):
        if dim % t == 0:
            best = t
    assert best is not None, (dim, target)
    return best


def _rmsnorm_kernel(x_ref, g_ref, o_ref):
    x = x_ref[...]
    y = x * lax.rsqrt(jnp.mean(x * x, axis=-1, keepdims=True) + NORM_EPS)
    o_ref[...] = (y * g_ref[...]).astype(o_ref.dtype)


def _rmsnorm(x, g, out_dtype):
    m, d = x.shape
    tm = _tile(m, 512, SUBLANES)
    return pl.pallas_call(
        _rmsnorm_kernel,
        grid=(m // tm,),
        in_specs=[pl.BlockSpec((tm, d), lambda i: (i, 0)), pl.BlockSpec((1, d), lambda i: (0, 0))],
        out_specs=pl.BlockSpec((tm, d), lambda i: (i, 0)),
        out_shape=jax.ShapeDtypeStruct((m, d), out_dtype),
        compiler_params=_params("parallel"),
        name="rmsnorm",
    )(x, g.reshape(1, d))


def _row_scale(ss_ref, d):
    return lax.rsqrt(jnp.sum(ss_ref[...], axis=1, keepdims=True) * (1.0 / d) + NORM_EPS)


def _mm_kernel(*refs, w_is_nk, has_scale, has_res, has_gain):
    refs = list(refs)
    a_ref, w_ref = refs[:2]
    rest = refs[2:]
    ss_ref = rest.pop(0) if has_scale else None
    r_ref = rest.pop(0) if has_res else None
    g_ref = rest.pop(0) if has_gain else None
    o_ref = rest.pop(0)
    w_contract = 1 if w_is_nk else 0
    out = lax.dot_general(a_ref[...], w_ref[...], (((1,), (w_contract,)), ((), ())), preferred_element_type=F32)
    if has_scale:
        out = out * _row_scale(ss_ref, a_ref.shape[1])
    if has_res:
        out = out + r_ref[...]
    o_ref[...] = out.astype(o_ref.dtype)
    if has_gain:
        xg_ref, ss_out_ref = rest
        xg_ref[...] = (out * g_ref[...]).astype(xg_ref.dtype)
        squares = _lane_blocks(out * out)
        part = sum(squares[1:], squares[0])
        j = pl.program_id(1)

        @pl.when(j == 0)
        def _():
            ss_out_ref[...] = part

        @pl.when(j > 0)
        def _():
            ss_out_ref[...] += part


def _matmul(a, w, *, res=None, row_ss=None, gain=None, out_dtype, n=None, w_layer=None, w_is_nk=False,
            col_block=None, tm=1024, tn=1024):
    m, kdim = a.shape
    n = w.shape[1 if w_is_nk else 2] if n is None else n
    tm = _tile(m, tm, SUBLANES)
    tn = _tile(n, tn)
    col = (lambda j: j) if col_block is None else functools.partial(col_block, tn=tn)
    if w_is_nk:
        w_spec = pl.BlockSpec((None, tn, kdim), lambda i, j: (w_layer, col(j), 0))
    else:
        w_spec = pl.BlockSpec((None, kdim, tn), lambda i, j: (w_layer, 0, col(j)))
    tile = pl.BlockSpec((tm, tn), lambda i, j: (i, j))
    rows = pl.BlockSpec((tm, LANES), lambda i, j: (i, 0))
    in_specs = [pl.BlockSpec((tm, kdim), lambda i, j: (i, 0)), w_spec]
    args = [a, w]
    if row_ss is not None:
        in_specs.append(rows)
        args.append(row_ss)
    if res is not None:
        in_specs.append(tile)
        args.append(res)
    out_specs, out_shape = tile, jax.ShapeDtypeStruct((m, n), out_dtype)
    if gain is not None:
        in_specs.append(pl.BlockSpec((1, tn), lambda i, j: (0, j)))
        args.append(gain.reshape(1, n))
        out_specs = [tile, tile, rows]
        out_shape = [out_shape, jax.ShapeDtypeStruct((m, n), BF16), jax.ShapeDtypeStruct((m, LANES), F32)]
    return pl.pallas_call(
        functools.partial(_mm_kernel, w_is_nk=w_is_nk, has_scale=row_ss is not None, has_res=res is not None,
                          has_gain=gain is not None),
        grid=(m // tm, n // tn),
        in_specs=in_specs,
        out_specs=out_specs,
        out_shape=out_shape,
        compiler_params=_params("parallel", "arbitrary" if gain is not None else "parallel"),
        name="matmul",
    )(*args)


def _rope_tables(seq, head_dim, lanes_per_head):
    rot = head_dim // ROPE_FRACTION
    half = rot // 2
    inv_freq = jnp.power(jnp.float32(ROPE_THETA), -jnp.arange(half, dtype=F32) * (2.0 / rot))
    ang = jnp.arange(seq, dtype=F32)[:, None] * inv_freq[None, :]
    cos, sin = jnp.cos(ang), jnp.sin(ang)
    ones = jnp.ones((seq, lanes_per_head - rot), F32)
    zeros_h = jnp.zeros((seq, half), F32)
    zeros_r = jnp.zeros((seq, lanes_per_head - rot), F32)
    c = jnp.concatenate([cos, cos, ones], axis=1)
    a = jnp.concatenate([-sin, zeros_h, zeros_r], axis=1)
    b = jnp.concatenate([zeros_h, sin, zeros_r], axis=1)
    reps = LANES // lanes_per_head
    return tuple(jnp.tile(t, (1, reps)) for t in (c, a, b)), half


def _rope_apply(x, c, a, b, half):
    up = pltpu.roll(x, LANES - half, axis=1)
    dn = pltpu.roll(x, half, axis=1)
    return x * c + up * a + dn * b


def _rope_kernel(x_ref, c_ref, ab_ref, swap_ref, o_ref, *, n_scaled, scale):
    j = pl.program_id(1)
    s = jnp.where(j < n_scaled, jnp.float32(scale), jnp.float32(1.0))
    c, ab, swap = c_ref[...], ab_ref[...], swap_ref[...]
    for blk in range(x_ref.shape[1] // LANES):
        sl = slice(blk * LANES, (blk + 1) * LANES)
        x = x_ref[:, sl]
        partner = jnp.dot(x, swap, preferred_element_type=F32)
        o_ref[:, sl] = ((x.astype(F32) * c + partner * ab) * s).astype(o_ref.dtype)


def _rope(p, seq, tables, half, lanes_per_head, *, n_cols, n_scaled_cols, scale, tc=1024):
    m = p.shape[0]
    tm = _tile(seq, 1024, SUBLANES)
    tc = _tile(math.gcd(n_cols, n_scaled_cols) if n_scaled_cols else n_cols, tc)
    sb = seq // tm
    c, a, b = tables
    swap = np.zeros((LANES, LANES), np.float32)
    for head in range(0, LANES, lanes_per_head):
        for r in range(half):
            swap[head + r + half, head + r] = 1.0
            swap[head + r, head + r + half] = 1.0
    tab_spec = pl.BlockSpec((tm, LANES), lambda i, j: (i % sb, 0))
    return pl.pallas_call(
        functools.partial(_rope_kernel, n_scaled=n_scaled_cols // tc, scale=scale),
        grid=(m // tm, n_cols // tc),
        in_specs=[pl.BlockSpec((tm, tc), lambda i, j: (i, j)), tab_spec, tab_spec,
                  pl.BlockSpec((LANES, LANES), lambda i, j: (0, 0))],
        out_specs=pl.BlockSpec((tm, tc), lambda i, j: (i, j)),
        out_shape=jax.ShapeDtypeStruct((m, n_cols), p.dtype),
        compiler_params=_params("parallel", "parallel"),
        name="rope",
    )(p, c, a + b, jnp.asarray(swap, p.dtype))


def _idx_prep_kernel(p_ref, g_ref, beta_ref, c_ref, a_ref, b_ref, ik_ref, iw_ref, *, half, iw_scale):
    x = p_ref[:, :HEAD]
    mu = jnp.mean(x, axis=-1, keepdims=True)
    var = jnp.mean(jnp.square(x - mu), axis=-1, keepdims=True)
    y = (x - mu) * lax.rsqrt(var + LN_EPS)
    y = y * g_ref[...] + beta_ref[...]
    ik_ref[...] = _rope_apply(y, c_ref[...], a_ref[...], b_ref[...], half).astype(ik_ref.dtype)
    iw_ref[...] = p_ref[:, HEAD:] * jnp.float32(iw_scale)


def _idx_prep(p2, seq, ik_g, ik_b, tables, half):
    m = p2.shape[0]
    tm = _tile(seq, 512, SUBLANES)
    sb = seq // tm
    tab_spec = pl.BlockSpec((tm, LANES), lambda i: (i % sb, 0))
    vec_spec = pl.BlockSpec((1, HEAD), lambda i: (0, 0))
    blk = pl.BlockSpec((tm, HEAD), lambda i: (i, 0))
    return pl.pallas_call(
        functools.partial(_idx_prep_kernel, half=half, iw_scale=IDX_HEADS ** -0.5 * HEAD ** -0.5),
        grid=(m // tm,),
        in_specs=[pl.BlockSpec((tm, 2 * HEAD), lambda i: (i, 0)), vec_spec, vec_spec, tab_spec, tab_spec, tab_spec],
        out_specs=[blk, blk],
        out_shape=[jax.ShapeDtypeStruct((m, HEAD), BF16), jax.ShapeDtypeStruct((m, HEAD), F32)],
        compiler_params=_params("parallel"),
        name="idx_prep",
    )(p2, ik_g.reshape(1, HEAD), ik_b.reshape(1, HEAD), *tables)


DSA_TQ = 128
DSA_KC = 512
IDX_TQ = 128


def _indexer_kernel(iq_ref, ik_ref, iw_ref, mask_ref, key_ref, *, n_sel, n_chunks):
    qb = pl.program_id(1)
    tq, kc = IDX_TQ, DSA_KC
    nch = (qb * tq + tq - 1) // kc + 1
    idx_bits = (n_chunks * kc - 1).bit_length()
    iw = iw_ref[0]
    q_pos = qb * tq + lax.broadcasted_iota(jnp.int32, (tq, 1), 0)
    lane = lax.broadcasted_iota(jnp.int32, (1, kc), 1)

    def score_chunk(c, _):
        ikc = ik_ref[0, pl.ds(pl.multiple_of(c * kc, kc), kc), :]
        score = jnp.zeros((tq, kc), F32)
        for h in range(IDX_HEADS):
            rel = lax.dot_general(iq_ref[0, :, h * HEAD:(h + 1) * HEAD], ikc, (((1,), (1,)), ((), ())),
                                  preferred_element_type=F32)
            score = score + jnp.maximum(rel, 0.0) * iw[:, h:h + 1]
        bits = pltpu.bitcast(score, jnp.int32)
        key = jnp.where(bits < 0, bits ^ jnp.int32(0x7FFFFFFF), bits)
        key = jnp.where(bits == jnp.int32(INT_MIN), jnp.int32(0), key)
        key = jnp.where(c * kc + lane <= q_pos, key, jnp.int32(INT_MIN))
        key_ref[c] = key
        return 0

    lax.fori_loop(0, nch, score_chunk, 0)

    lane128 = lax.broadcasted_iota(jnp.int32, (1, LANES), 1)

    def for_blocks(c, fn):
        key = key_ref[c]
        return [fn(key[:, b * LANES:(b + 1) * LANES], c * kc + b * LANES + lane128) for b in range(kc // LANES)]

    def count(pred):
        def body(c, acc):
            for hit in for_blocks(c, lambda key, pos: jnp.where(pred(key, pos), jnp.int32(1), jnp.int32(0))):
                acc = acc + hit
            return acc
        acc = lax.fori_loop(0, nch, body, jnp.zeros((tq, LANES), jnp.int32))
        return jnp.broadcast_to(jnp.sum(acc, axis=1, keepdims=True), (tq, LANES))

    def write_mask(sel_fn):
        def store(c, tile):
            for r in range(tq // DSA_TQ):
                mask_ref[0, r, c] = tile[r * DSA_TQ:(r + 1) * DSA_TQ].astype(mask_ref.dtype)

        def body(c, _):
            store(c, jnp.concatenate(for_blocks(c, lambda key, pos: jnp.where(sel_fn(key, pos), 1.0, 0.0)), axis=1))
            return 0
        lax.fori_loop(0, nch, body, 0)

        def tail(c, _):
            store(c, jnp.zeros((tq, kc), F32))
            return 0
        lax.fori_loop(nch, n_chunks, tail, 0)

    cnt0 = count(lambda key, pos: key >= 0)
    nonneg = cnt0 >= n_sel
    t0 = jnp.where(nonneg, jnp.int32(0), jnp.int32(INT_MIN))
    n0 = jnp.where(nonneg, cnt0, jnp.broadcast_to(q_pos + 1, (tq, LANES)))
    n_bits = 31
    bits_per_check = 4

    def bit_step(i, t, n_ge):
        bit = jnp.where(i < n_bits, jnp.left_shift(jnp.int32(1), jnp.maximum(jnp.int32(n_bits - 1) - i, 0)), 0)
        cand = t | bit
        cnt = count(lambda key, pos: key >= cand)
        take = cnt >= n_sel
        return jnp.where(take, cand, t), jnp.where(take, cnt, n_ge)

    def search_cond(state):
        i, _, n_ge = state
        return jnp.logical_and(i < n_bits, jnp.max(n_ge) > n_sel)

    def search_body(state):
        i, t, n_ge = state
        for b in range(bits_per_check):
            t, n_ge = bit_step(i + b, t, n_ge)
        return i + bits_per_check, t, n_ge

    _, thr, n_ge = lax.while_loop(search_cond, search_body, (jnp.int32(0), t0, n0))
    real = thr != jnp.int32(INT_MIN)
    excess = jnp.max(jnp.where(jnp.logical_and(real, n_ge > n_sel), 1, 0)) > 0

    @pl.when(jnp.logical_not(excess))
    def _():
        thr_eff = jnp.maximum(thr, jnp.int32(INT_MIN + 1))
        write_mask(lambda key, pos: key >= thr_eff)

    @pl.when(excess)
    def _():
        need = n_sel - count(lambda key, pos: key > thr)

        def idx_step(i, j):
            cand = j | jnp.left_shift(jnp.int32(1), jnp.int32(idx_bits - 1) - i)
            cnt = count(lambda key, pos: jnp.logical_and(key == thr, pos < cand))
            return jnp.where(cnt < need, cand, j)

        last = lax.fori_loop(0, idx_bits, idx_step, jnp.zeros((tq, LANES), jnp.int32))
        write_mask(lambda key, pos: jnp.logical_or(
            key > thr, jnp.logical_and(jnp.logical_and(key == thr, pos <= last), real)))


def _indexer(iq, ik, iw, n_sel):
    bsz, seq = ik.shape[0], ik.shape[1]
    n_chunks = seq // DSA_KC
    nq = seq // DSA_TQ
    per_step = IDX_TQ // DSA_TQ
    assert seq % IDX_TQ == 0 and IDX_TQ % DSA_TQ == 0
    iq_arr, iq_blk = iq
    return pl.pallas_call(
        functools.partial(_indexer_kernel, n_sel=n_sel, n_chunks=n_chunks),
        grid=(bsz, seq // IDX_TQ),
        in_specs=[pl.BlockSpec((1, IDX_TQ, IDX_HEADS * HEAD), lambda b, q: (b, q, iq_blk)),
                  pl.BlockSpec((1, seq, HEAD), lambda b, q: (b, 0, 0)),
                  pl.BlockSpec((1, IDX_TQ, HEAD), lambda b, q: (b, q, 0))],
        out_specs=pl.BlockSpec((1, per_step, n_chunks, DSA_TQ, DSA_KC), lambda b, q: (b, q, 0, 0, 0)),
        out_shape=jax.ShapeDtypeStruct((bsz, nq, n_chunks, DSA_TQ, DSA_KC), BF16),
        scratch_shapes=[pltpu.VMEM((n_chunks, IDX_TQ, DSA_KC), jnp.int32)],
        compiler_params=_params("parallel", "arbitrary"),
        name="dsa_indexer",
    )(iq_arr, ik, iw)


def _lane_blocks(x):
    return [x[:, j * LANES:(j + 1) * LANES] for j in range(x.shape[1] // LANES)]


STREAMS = 2


def _two_sweep_attention(nch, rows, logits_fn, pv_fn, s_ref, mx_ref, ls_ref, acc_ref):
    mx_ref[...] = jnp.full(mx_ref.shape, -jnp.inf, F32)

    def sweep1_chunk(c, last):
        for i, logits2 in enumerate(logits_fn(c, last)):
            sl = slice(i * rows, (i + 1) * rows)
            s_ref[c, sl, :] = logits2
            mx = mx_ref[sl, :]
            for blk in _lane_blocks(logits2):
                mx = jnp.maximum(mx, blk)
            mx_ref[sl, :] = mx

    def pairs_then_rest(n, chunk_fn):
        def run(first, count, width):
            def trip(t, _):
                for k in range(width):
                    chunk_fn(first + width * t + k)
                return 0
            lax.fori_loop(0, count, trip, 0)

        triples = n // 3
        rest = n - 3 * triples
        run(0, triples, 3)
        run(3 * triples, rest // 2, 2)
        run(3 * triples + 2 * (rest // 2), rest % 2, 1)

    pairs_then_rest(nch - 1, lambda c: sweep1_chunk(c, False))
    sweep1_chunk(nch - 1, True)

    row_max = jnp.max(mx_ref[...], axis=1, keepdims=True)
    mx_ref[...] = jnp.broadcast_to(row_max, mx_ref.shape)
    ls_ref[...] = jnp.zeros(ls_ref.shape, F32)
    acc_ref[...] = jnp.zeros(acc_ref.shape, F32)

    def sweep2_chunk(c):
        for i in range(STREAMS):
            sl = slice(i * rows, (i + 1) * rows)
            m = mx_ref[sl, :]
            ps = [jnp.exp2(blk - m) for blk in _lane_blocks(s_ref[c, sl, :])]
            ls_ref[sl, :] += sum(ps[1:], ps[0])
            acc_ref[sl, :] += pv_fn(c, i, jnp.concatenate(ps, axis=1).astype(BF16))

    pairs_then_rest(nch, sweep2_chunk)
    return acc_ref[...] / jnp.sum(ls_ref[...], axis=1, keepdims=True)


def _softmax_scratch(rows, n_chunks, kc):
    return [pltpu.VMEM((n_chunks, rows, kc), F32),
            pltpu.VMEM((rows, LANES), F32),
            pltpu.VMEM((rows, LANES), F32),
            pltpu.VMEM((rows, HEAD), F32)]


def _dsa_attn_kernel(q_ref, k_ref, v_ref, mask_ref, o_ref, s_ref, mx_ref, ls_ref, acc_ref, *, groups):
    qb = pl.program_id(2)
    tq, kc = DSA_TQ, DSA_KC
    nch = (qb * tq) // kc + 1
    rows = groups * tq
    qs = [jnp.concatenate([q_ref[0, :, (i * groups + r) * HEAD:(i * groups + r + 1) * HEAD] for r in range(groups)],
                          axis=0) for i in range(STREAMS)]

    def logits_fn(c, last):
        start = pl.multiple_of(c * kc, kc)
        sel = (mask_ref[0, 0, c].astype(F32) > 0.5)[None]
        out = []
        for i in range(STREAMS):
            k_c = k_ref[0, pl.ds(start, kc), i * HEAD:(i + 1) * HEAD]
            raw = lax.dot_general(qs[i], k_c, (((1,), (1,)), ((), ())), preferred_element_type=F32)
            out.append(jnp.where(sel, raw.reshape(groups, tq, kc), NEG_INF).reshape(rows, kc))
        return out

    def pv_fn(c, i, p):
        v_c = v_ref[0, pl.ds(pl.multiple_of(c * kc, kc), kc), i * HEAD:(i + 1) * HEAD]
        return jnp.dot(p, v_c, preferred_element_type=F32)

    out = _two_sweep_attention(nch, rows, logits_fn, pv_fn, s_ref, mx_ref, ls_ref, acc_ref)
    for r in range(STREAMS * groups):
        o_ref[0, :, r * HEAD:(r + 1) * HEAD] = out[r * tq:(r + 1) * tq].astype(o_ref.dtype)


def _dsa_attention(q, k, v, mask, bsz, seq):
    groups = A_HEADS // A_KV_HEADS
    (q_arr, q0), (k_arr, k0), (v_arr, v0) = q, k, v
    nq = seq // DSA_TQ
    n_chunks = seq // DSA_KC
    qw = STREAMS * groups * HEAD
    kw = STREAMS * HEAD
    assert q0 % (STREAMS * groups) == 0 and k0 % STREAMS == 0 and v0 % STREAMS == 0
    return pl.pallas_call(
        functools.partial(_dsa_attn_kernel, groups=groups),
        grid=(bsz, A_KV_HEADS // STREAMS, nq),
        in_specs=[pl.BlockSpec((1, DSA_TQ, qw), lambda b, g, i: (b, i, q0 // (STREAMS * groups) + g)),
                  pl.BlockSpec((1, seq, kw), lambda b, g, i: (b, 0, k0 // STREAMS + g)),
                  pl.BlockSpec((1, seq, kw), lambda b, g, i: (b, 0, v0 // STREAMS + g)),
                  pl.BlockSpec((1, 1, n_chunks, DSA_TQ, DSA_KC), lambda b, g, i: (b, i, 0, 0, 0))],
        out_specs=pl.BlockSpec((1, DSA_TQ, qw), lambda b, g, i: (b, i, g)),
        out_shape=jax.ShapeDtypeStruct((bsz, seq, A_HEADS * HEAD), BF16),
        scratch_shapes=_softmax_scratch(STREAMS * groups * DSA_TQ, n_chunks, DSA_KC),
        compiler_params=_params("parallel", "parallel", "arbitrary"),
        name="dsa_attention",
    )(q_arr, k_arr, v_arr, mask)


def _fox_cum_kernel(f_ref, bias_ref, cum_t_ref):
    z = f_ref[0] + bias_ref[...]
    x = jnp.minimum(z, 0.0) - jnp.log1p(jnp.exp(-jnp.abs(z)))
    seq = x.shape[0]
    row = lax.broadcasted_iota(jnp.int32, (seq, 1), 0)
    d = 1
    while d < seq:
        x = x + jnp.where(row >= d, pltpu.roll(x, d, axis=0), 0.0)
        d *= 2
    cum_t_ref[0] = (x * jnp.float32(LOG2E)).T


def _fox_cum(f, bias):
    bsz, seq, _ = f.shape
    return pl.pallas_call(
        _fox_cum_kernel,
        grid=(bsz,),
        in_specs=[pl.BlockSpec((1, seq, HEAD), lambda b: (b, 0, 0)), pl.BlockSpec((1, HEAD), lambda b: (0, 0))],
        out_specs=pl.BlockSpec((1, HEAD, seq), lambda b: (b, 0, 0)),
        out_shape=jax.ShapeDtypeStruct((bsz, HEAD, seq), F32),
        compiler_params=_params("parallel"),
        name="fox_cum",
    )(f, bias)


FOX_TQ = 512
FOX_KC = 512


def _fox_attn_kernel(q_ref, k_ref, v_ref, cq_ref, ck_ref, o_ref, s_ref, mx_ref, ls_ref, acc_ref, *, scale2):
    qb = pl.program_id(2)
    tq, kc = FOX_TQ, FOX_KC
    heads = [slice(i * HEAD, (i + 1) * HEAD) for i in range(STREAMS)]
    qs = [(q_ref[0, :, h].astype(F32) * jnp.float32(scale2)).astype(q_ref.dtype) for h in heads]
    cqs = [jnp.broadcast_to(cq_ref[0, i, 0], (LANES, tq)).T for i in range(STREAMS)]

    def logits_fn(c, last):
        start = pl.multiple_of(c * kc, kc)
        out = []
        for i in range(STREAMS):
            raw = lax.dot_general(qs[i], k_ref[0, pl.ds(start, kc), heads[i]], (((1,), (1,)), ((), ())),
                                  preferred_element_type=F32)
            ck = ck_ref[0, i, c]
            blocks = [blk + cqs[i] - ck[:, j * LANES:(j + 1) * LANES] for j, blk in enumerate(_lane_blocks(raw))]
            logits2 = jnp.concatenate(blocks, axis=1)
            if last:
                row = lax.broadcasted_iota(jnp.int32, (tq, 1), 0)
                lane = lax.broadcasted_iota(jnp.int32, (1, kc), 1)
                logits2 = jnp.where(lane <= row, logits2, NEG_INF)
            out.append(logits2)
        return out

    def pv_fn(c, i, p):
        v_c = v_ref[0, pl.ds(pl.multiple_of(c * kc, kc), kc), heads[i]]
        return jnp.dot(p, v_c, preferred_element_type=F32)

    out = _two_sweep_attention(qb + 1, tq, logits_fn, pv_fn, s_ref, mx_ref, ls_ref, acc_ref)
    for i in range(STREAMS):
        o_ref[0, :, heads[i]] = out[i * tq:(i + 1) * tq].astype(o_ref.dtype)


def _fox_attention(p, cum_rows, bsz, seq):
    tq = FOX_TQ
    assert FOX_TQ == FOX_KC and seq % tq == 0
    n_chunks = seq // FOX_KC
    hw = STREAMS * HEAD
    cq_rows = cum_rows.reshape(bsz, B_HEADS, 1, 1, seq)
    return pl.pallas_call(
        functools.partial(_fox_attn_kernel, scale2=HEAD ** -0.5 * LOG2E),
        grid=(bsz, B_HEADS // STREAMS, seq // tq),
        in_specs=[pl.BlockSpec((1, tq, hw), lambda b, h, i: (b, i, h)),
                  pl.BlockSpec((1, seq, hw), lambda b, h, i: (b, 0, B_HEADS // STREAMS + h)),
                  pl.BlockSpec((1, seq, hw), lambda b, h, i: (b, 0, 2 * B_HEADS // STREAMS + h)),
                  pl.BlockSpec((1, STREAMS, 1, 1, tq), lambda b, h, i: (b, h, 0, 0, i)),
                  pl.BlockSpec((1, STREAMS, n_chunks, 1, FOX_KC), lambda b, h, i: (b, h, 0, 0, 0))],
        out_specs=pl.BlockSpec((1, tq, hw), lambda b, h, i: (b, i, h)),
        out_shape=jax.ShapeDtypeStruct((bsz, seq, B_HEADS * HEAD), BF16),
        scratch_shapes=_softmax_scratch(STREAMS * tq, n_chunks, FOX_KC),
        compiler_params=_params("parallel", "parallel", "arbitrary"),
        name="fox_attention",
    )(p, p, p, cq_rows, cum_rows)


def _swa_kernel(sinks_ref, q_ref, kp_ref, kc_ref, vp_ref, vc_ref, o_ref):
    n = pl.program_id(1)
    pair = pl.program_id(2)
    w, dh = WINDOW, C_HEAD_DIM
    groups = C_HEADS // C_KV_HEADS
    kv_per_blk = LANES // dh
    i = lax.broadcasted_iota(jnp.int32, (w, 1), 0)
    j = lax.broadcasted_iota(jnp.int32, (1, 2 * w), 1)
    valid = jnp.logical_and(jnp.logical_and(j > i, j <= i + w), n * w + j >= w)
    kk = jnp.concatenate([kp_ref[0], kc_ref[0]], axis=0)
    vv = jnp.concatenate([vp_ref[0], vc_ref[0]], axis=0)
    for g in range(kv_per_blk):
        k_g = kk[:, g * dh:(g + 1) * dh]
        v_g = vv[:, g * dh:(g + 1) * dh]
        for r in range(groups):
            col = (g * groups + r) * dh
            q = q_ref[0, :, col:col + dh]
            s = lax.dot_general(q, k_g, (((1,), (1,)), ((), ())), preferred_element_type=F32)
            logits = jnp.where(valid, s, NEG_INF)
            sink = sinks_ref[(pair * kv_per_blk + g) * groups + r]
            m = jnp.maximum(jnp.max(logits, axis=1, keepdims=True), sink)
            p = jnp.exp(logits - m)
            denom = jnp.sum(p, axis=1, keepdims=True) + jnp.exp(sink - m)
            o = jnp.dot((p / denom).astype(v_g.dtype), v_g, preferred_element_type=F32)
            o_ref[0, :, col:col + dh] = o.astype(o_ref.dtype)


def _swa_attention(pr, p, sinks, bsz, seq):
    w = WINDOW
    nb = seq // w
    kv_per_blk = LANES // C_HEAD_DIM
    pairs = C_KV_HEADS // kv_per_blk
    qw = kv_per_blk * (C_HEADS // C_KV_HEADS) * C_HEAD_DIM
    k0 = C_HEADS * C_HEAD_DIM // LANES
    v0 = k0 + C_KV_HEADS * C_HEAD_DIM // LANES
    prev = lambda col0: (lambda b, n, g: (b, jnp.maximum(n - 1, 0), col0 + g))
    cur = lambda col0: (lambda b, n, g: (b, n, col0 + g))
    blk = (1, w, LANES)
    return pl.pallas_call(
        _swa_kernel,
        grid=(bsz, nb, pairs),
        in_specs=[pl.BlockSpec(memory_space=pltpu.SMEM),
                  pl.BlockSpec((1, w, qw), lambda b, n, g: (b, n, g)),
                  pl.BlockSpec(blk, prev(k0)), pl.BlockSpec(blk, cur(k0)),
                  pl.BlockSpec(blk, prev(v0)), pl.BlockSpec(blk, cur(v0))],
        out_specs=pl.BlockSpec((1, w, qw), lambda b, n, g: (b, n, g)),
        out_shape=jax.ShapeDtypeStruct((bsz, seq, C_HEADS * C_HEAD_DIM), BF16),
        compiler_params=_params("parallel", "parallel", "parallel"),
        name="swa_attention",
    )(sinks, pr, pr, pr, p, p)


FFN_SUB_BLOCKS = 4
FFN_CAST_SLICES = 2


def _ffn_up_kernel(h_ref, wg_ref, wv_ref, cwg_ref, cwv_ref, cbg_ref, cbv_ref, o_ref, w_scr, u_scr, carry,
                   *, tiles_per_seq):
    mi = pl.program_id(0)
    ni = pl.program_id(1)
    tm = h_ref.shape[0]
    tn = wg_ref.shape[1]
    kdim = h_ref.shape[1]

    def cast_weights(k0, k1):
        w_scr[k0:k1, :tn] = wg_ref[k0:k1, :].astype(w_scr.dtype)
        w_scr[k0:k1, tn:] = wv_ref[k0:k1, :].astype(w_scr.dtype)

    @pl.when((mi % tiles_per_seq) == 0)
    def _():
        carry[ni] = jnp.zeros(carry.shape[1:], F32)

    u_scr[:SUBLANES, :] = carry[ni]
    sub = tm // FFN_SUB_BLOCKS

    def matmul(s, k0=0, k1=None, accumulate=False):
        k1 = kdim if k1 is None else k1
        rows = slice(s * sub, (s + 1) * sub)
        dst = slice(SUBLANES + s * sub, SUBLANES + (s + 1) * sub)
        part = jnp.dot(h_ref[rows, k0:k1], w_scr[k0:k1, :], preferred_element_type=F32)
        u_scr[dst, :] = u_scr[dst, :] + part if accumulate else part

    def conv(s, cols, cw_ref, cb_ref):
        window = u_scr[s * sub:s * sub + SUBLANES + sub, cols]
        acc = window * cw_ref[0:1, :]
        for tap in range(1, CONV_WIDTH):
            acc = pltpu.roll(acc, 1, axis=0) + window * cw_ref[tap:tap + 1, :]
        return cb_ref[...] + acc[SUBLANES:]

    def epilogue(s):
        gate = conv(s, slice(0, tn), cwg_ref, cbg_ref)
        val = conv(s, slice(tn, 2 * tn), cwv_ref, cbv_ref)
        half_gate = 0.5 * gate
        silu = half_gate * jnp.tanh(half_gate) + half_gate
        o_ref[s * sub:(s + 1) * sub, :] = (silu * val).astype(o_ref.dtype)

    kstep = kdim // FFN_CAST_SLICES
    for ks in range(FFN_CAST_SLICES):
        cast_weights(ks * kstep, (ks + 1) * kstep)
        matmul(0, ks * kstep, (ks + 1) * kstep, accumulate=ks > 0)
    for s in range(1, FFN_SUB_BLOCKS):
        matmul(s)
        epilogue(s - 1)
    epilogue(FFN_SUB_BLOCKS - 1)
    carry[ni] = u_scr[tm:, :]


def _ffn_up(h, w_up, conv_w, conv_b, layer, seq, *, tm=2048, tn=256):
    m, d = h.shape
    dff = w_up.shape[2] // 2
    tm = _tile(seq, tm, SUBLANES)
    tn = _tile(dff, tn)
    nt = dff // tn
    assert CONV_WIDTH - 1 <= SUBLANES and tm % (FFN_SUB_BLOCKS * SUBLANES) == 0
    conv_b = conv_b.reshape(conv_b.shape[0], 1, 2 * dff)
    return pl.pallas_call(
        functools.partial(_ffn_up_kernel, tiles_per_seq=seq // tm),
        grid=(m // tm, nt),
        in_specs=[pl.BlockSpec((tm, d), lambda i, j: (i, 0), pipeline_mode=pl.Buffered(1)),
                  pl.BlockSpec((None, d, tn), lambda i, j: (layer, 0, j)),
                  pl.BlockSpec((None, d, tn), lambda i, j: (layer, 0, j + nt)),
                  pl.BlockSpec((None, CONV_WIDTH, tn), lambda i, j: (layer, 0, j)),
                  pl.BlockSpec((None, CONV_WIDTH, tn), lambda i, j: (layer, 0, j + nt)),
                  pl.BlockSpec((None, 1, tn), lambda i, j: (layer, 0, j)),
                  pl.BlockSpec((None, 1, tn), lambda i, j: (layer, 0, j + nt))],
        out_specs=pl.BlockSpec((tm, tn), lambda i, j: (i, j)),
        out_shape=jax.ShapeDtypeStruct((m, dff), BF16),
        scratch_shapes=[pltpu.VMEM((d, 2 * tn), BF16),
                        pltpu.VMEM((SUBLANES + tm, 2 * tn), F32),
                        pltpu.VMEM((nt, SUBLANES, 2 * tn), F32)],
        compiler_params=_params("arbitrary", "arbitrary"),
        name="ffn_up",
    )(h, w_up, w_up, conv_w, conv_w, conv_b, conv_b)


OUT_PROJ_TN = 512


def _pad_rows(w, n):
    return jnp.pad(w, ((0, 0), (0, n - w.shape[1]), (0, 0)))


def _dsa_weight_col(j, *, tn):
    nq, niq, nkv2 = A_HEADS * HEAD // tn, IDX_HEADS * HEAD // tn, 2 * A_KV_HEADS * HEAD // tn
    return jnp.where(j < nq, j, jnp.where(j < nq + niq, j + nkv2, j - niq))


def _dsa_mixer(h, row_ss, x, w_in, w_small, ik_g, ik_b, w_out, slot, bsz, seq):
    a_q, a_kv, a_iq = A_HEADS * HEAD, A_KV_HEADS * HEAD, IDX_HEADS * HEAD
    p = _matmul(h, w_in, row_ss=row_ss, n=a_q + a_iq + 2 * a_kv, w_layer=slot, w_is_nk=True,
                col_block=_dsa_weight_col, out_dtype=BF16)
    p2 = _matmul(h, w_small, row_ss=row_ss, w_layer=slot, w_is_nk=True, out_dtype=F32)
    tables, half = _rope_tables(seq, HEAD, HEAD)
    n_rope = a_q + a_iq + a_kv
    pr = _rope(p, seq, tables, half, HEAD, n_cols=n_rope, n_scaled_cols=a_q, scale=HEAD ** -0.5 * LOG2E)
    ik, iw = _idx_prep(p2, seq, ik_g, ik_b, tables, half)
    pr3 = pr.reshape(bsz, seq, n_rope)
    p3 = p.reshape(bsz, seq, p.shape[1])
    n_sel = min(INDEX_TOPK, seq // 4)
    mask = _indexer((pr3, a_q // a_iq), ik.reshape(bsz, seq, HEAD), iw.reshape(bsz, seq, HEAD), n_sel)
    o = _dsa_attention((pr3, 0), (pr3, (a_q + a_iq) // HEAD), (p3, n_rope // HEAD), mask, bsz, seq)
    return _matmul(o.reshape(bsz * seq, a_q), w_out, w_layer=slot, res=x, out_dtype=F32, tn=OUT_PROJ_TN)


def _fox_mixer(h, row_ss, x, w_in, w_f, f_bias, w_out, slot, bsz, seq):
    d = h.shape[1]
    p = _matmul(h, w_in, row_ss=row_ss, n=3 * d, w_layer=slot, w_is_nk=True, out_dtype=BF16)
    f = _matmul(h, w_f, row_ss=row_ss, w_layer=slot, w_is_nk=True, out_dtype=F32, tn=HEAD)
    bias = jnp.pad(f_bias.astype(F32), (0, HEAD - B_HEADS)).reshape(1, HEAD)
    cum_t = _fox_cum(f.reshape(bsz, seq, HEAD), bias)
    cum_rows = cum_t[:, :B_HEADS].reshape(bsz, B_HEADS, seq // FOX_KC, 1, FOX_KC)
    o = _fox_attention(p.reshape(bsz, seq, 3 * d), cum_rows, bsz, seq)
    return _matmul(o.reshape(bsz * seq, d), w_out, w_layer=slot, res=x, out_dtype=F32, tn=OUT_PROJ_TN)


def _swa_mixer(h, row_ss, x, w_in, sinks, w_out, slot, bsz, seq):
    c_q, c_kv = C_HEADS * C_HEAD_DIM, C_KV_HEADS * C_HEAD_DIM
    p = _matmul(h, w_in, row_ss=row_ss, w_layer=slot, out_dtype=BF16)
    tables, half = _rope_tables(seq, C_HEAD_DIM, C_HEAD_DIM)
    pr = _rope(p, seq, tables, half, C_HEAD_DIM, n_cols=c_q + c_kv, n_scaled_cols=c_q, scale=C_HEAD_DIM ** -0.5)
    o = _swa_attention(pr.reshape(bsz, seq, c_q + c_kv), p.reshape(bsz, seq, p.shape[1]),
                       sinks.astype(F32), bsz, seq)
    return _matmul(o.reshape(bsz * seq, c_q), w_out, w_layer=slot, res=x, out_dtype=F32, tn=OUT_PROJ_TN)


def _conv_glu_ffn(h, x, w_up, conv_w, conv_b, w_down, layer, next_gain, seq):
    g = _ffn_up(h, w_up, conv_w, conv_b, layer, seq)
    return _matmul(g, w_down, w_layer=layer, res=x, gain=next_gain, out_dtype=F32, tm=512, tn=512)


def kernel(x, attn_norm, ffn_norm, final_norm, a_w_in, a_idx_k_norm_g, a_idx_k_norm_b, a_w_out,
           b_w_in, b_f_bias, b_w_out, c_w_in, c_sinks, c_w_out,
           ffn_w_up, ffn_conv_w, ffn_conv_b, ffn_w_down):
    bsz, seq, d = x.shape
    depth = attn_norm.shape[0]
    a_main = A_HEADS * HEAD + 2 * A_KV_HEADS * HEAD + IDX_HEADS * HEAD
    a_w_in, b_w_in = (jnp.swapaxes(w, 1, 2).astype(BF16) for w in (a_w_in, b_w_in))
    a_w_out, b_w_out, c_w_in, c_w_out, ffn_w_down = (
        w.astype(BF16) for w in (a_w_out, b_w_out, c_w_in, c_w_out, ffn_w_down))
    a_w_small = _pad_rows(a_w_in[:, a_main:, :], 2 * HEAD)
    b_w_f = _pad_rows(b_w_in[:, 3 * d:, :], HEAD)
    x = x.reshape(bsz * seq, d)
    h, row_ss = _rmsnorm(x, attn_norm[0], BF16), None
    for layer in range(depth):
        mixer, slot = layer % N_MIXERS, layer // N_MIXERS
        if mixer == 0:
            x = _dsa_mixer(h, row_ss, x, a_w_in, a_w_small, a_idx_k_norm_g[slot], a_idx_k_norm_b[slot], a_w_out,
                           slot, bsz, seq)
        elif mixer == 1:
            x = _fox_mixer(h, row_ss, x, b_w_in, b_w_f, b_f_bias[slot], b_w_out, slot, bsz, seq)
        else:
            x = _swa_mixer(h, row_ss, x, c_w_in, c_sinks[slot], c_w_out, slot, bsz, seq)
        h = _rmsnorm(x, ffn_norm[layer], BF16)
        if layer + 1 < depth:
            x, h, row_ss = _conv_glu_ffn(h, x, ffn_w_up, ffn_conv_w, ffn_conv_b, ffn_w_down, layer,
                                         attn_norm[layer + 1], seq)
        else:
            x = _conv_glu_ffn(h, x, ffn_w_up, ffn_conv_w, ffn_conv_b, ffn_w_down, layer, None, seq)
    return _rmsnorm(x, final_norm, F32).reshape(bsz, seq, d)
```

```python
import functools
import math

import jax
import jax.numpy as jnp
import numpy as np
from jax import lax
from jax.experimental import pallas as pl
from jax.experimental.pallas import tpu as pltpu

F32 = jnp.float32
BF16 = jnp.bfloat16

N_MIXERS = 3
ROPE_THETA = 500000.0
ROPE_FRACTION = 4
NORM_EPS = 1e-6
LN_EPS = 1e-6
NEG_INF = -1e30
LOG2E = math.log2(math.e)
HEAD = 128
A_HEADS, A_KV_HEADS, IDX_HEADS, INDEX_TOPK = 32, 8, 16, 256
B_HEADS = 32
C_HEADS, C_KV_HEADS, C_HEAD_DIM, WINDOW = 64, 8, 64, 128
CONV_WIDTH = 3

LANES = 128
SUBLANES = 8
VMEM_LIMIT_BYTES = 56 * 1024 * 1024
INT_MIN = -(2 ** 31)


def _params(*sem):
    return pltpu.CompilerParams(dimension_semantics=sem, vmem_limit_bytes=VMEM_LIMIT_BYTES)


def _tile(dim, target, quantum=LANES):
    if dim <= target:
        return dim
    best = None
    for t in range(quantum, target + 1, quantum):
        if dim % t == 0:
            best = t
    assert best is not None, (dim, target)
    return best


def _rmsnorm_kernel(x_ref, g_ref, o_ref):
    x = x_ref[...]
    y = x * lax.rsqrt(jnp.mean(x * x, axis=-1, keepdims=True) + NORM_EPS)
    o_ref[...] = (y * g_ref[...]).astype(o_ref.dtype)


def _rmsnorm(x, g, out_dtype):
    m, d = x.shape
    tm = _tile(m, 512, SUBLANES)
    return pl.pallas_call(
        _rmsnorm_kernel,
        grid=(m // tm,),
        in_specs=[pl.BlockSpec((tm, d), lambda i: (i, 0)), pl.BlockSpec((1, d), lambda i: (0, 0))],
        out_specs=pl.BlockSpec((tm, d), lambda i: (i, 0)),
        out_shape=jax.ShapeDtypeStruct((m, d), out_dtype),
        compiler_params=_params("parallel"),
        name="rmsnorm",
    )(x, g.reshape(1, d))


def _row_scale(ss_ref, d):
    return lax.rsqrt(jnp.sum(ss_ref[...], axis=1, keepdims=True) * (1.0 / d) + NORM_EPS)


def _mm_kernel(*refs, w_is_nk, has_scale, has_res, has_gain):
    refs = list(refs)
    a_ref, w_ref = refs[:2]
    rest = refs[2:]
    ss_ref = rest.pop(0) if has_scale else None
    r_ref = rest.pop(0) if has_res else None
    g_ref = rest.pop(0) if has_gain else None
    o_ref = rest.pop(0)
    w_contract = 1 if w_is_nk else 0
    out = lax.dot_general(a_ref[...], w_ref[...], (((1,), (w_contract,)), ((), ())), preferred_element_type=F32)
    if has_scale:
        out = out * _row_scale(ss_ref, a_ref.shape[1])
    if has_res:
        out = out + r_ref[...]
    o_ref[...] = out.astype(o_ref.dtype)
    if has_gain:
        xg_ref, ss_out_ref = rest
        xg_ref[...] = (out * g_ref[...]).astype(xg_ref.dtype)
        squares = _lane_blocks(out * out)
        part = sum(squares[1:], squares[0])
        j = pl.program_id(1)

        @pl.when(j == 0)
        def _():
            ss_out_ref[...] = part

        @pl.when(j > 0)
        def _():
            ss_out_ref[...] += part


def _matmul(a, w, *, res=None, row_ss=None, gain=None, out_dtype, n=None, w_layer=None, w_is_nk=False,
            col_block=None, tm=1024, tn=1024):
    m, kdim = a.shape
    n = w.shape[1 if w_is_nk else 2] if n is None else n
    tm = _tile(m, tm, SUBLANES)
    tn = _tile(n, tn)
    col = (lambda j: j) if col_block is None else functools.partial(col_block, tn=tn)
    if w_is_nk:
        w_spec = pl.BlockSpec((None, tn, kdim), lambda i, j: (w_layer, col(j), 0))
    else:
        w_spec = pl.BlockSpec((None, kdim, tn), lambda i, j: (w_layer, 0, col(j)))
    tile = pl.BlockSpec((tm, tn), lambda i, j: (i, j))
    rows = pl.BlockSpec((tm, LANES), lambda i, j: (i, 0))
    in_specs = [pl.BlockSpec((tm, kdim), lambda i, j: (i, 0)), w_spec]
    args = [a, w]
    if row_ss is not None:
        in_specs.append(rows)
        args.append(row_ss)
    if res is not None:
        in_specs.append(tile)
        args.append(res)
    out_specs, out_shape = tile, jax.ShapeDtypeStruct((m, n), out_dtype)
    if gain is not None:
        in_specs.append(pl.BlockSpec((1, tn), lambda i, j: (0, j)))
        args.append(gain.reshape(1, n))
        out_specs = [tile, tile, rows]
        out_shape = [out_shape, jax.ShapeDtypeStruct((m, n), BF16), jax.ShapeDtypeStruct((m, LANES), F32)]
    return pl.pallas_call(
        functools.partial(_mm_kernel, w_is_nk=w_is_nk, has_scale=row_ss is not None, has_res=res is not None,
                          has_gain=gain is not None),
        grid=(m // tm, n // tn),
        in_specs=in_specs,
        out_specs=out_specs,
        out_shape=out_shape,
        compiler_params=_params("parallel", "arbitrary" if gain is not None else "parallel"),
        name="matmul",
    )(*args)


def _rope_tables(seq, head_dim, lanes_per_head):
    rot = head_dim // ROPE_FRACTION
    half = rot // 2
    inv_freq = jnp.power(jnp.float32(ROPE_THETA), -jnp.arange(half, dtype=F32) * (2.0 / rot))
    ang = jnp.arange(seq, dtype=F32)[:, None] * inv_freq[None, :]
    cos, sin = jnp.cos(ang), jnp.sin(ang)
    ones = jnp.ones((seq, lanes_per_head - rot), F32)
    zeros_h = jnp.zeros((seq, half), F32)
    zeros_r = jnp.zeros((seq, lanes_per_head - rot), F32)
    c = jnp.concatenate([cos, cos, ones], axis=1)
    a = jnp.concatenate([-sin, zeros_h, zeros_r], axis=1)
    b = jnp.concatenate([zeros_h, sin, zeros_r], axis=1)
    reps = LANES // lanes_per_head
    return tuple(jnp.tile(t, (1, reps)) for t in (c, a, b)), half


def _rope_apply(x, c, a, b, half):
    up = pltpu.roll(x, LANES - half, axis=1)
    dn = pltpu.roll(x, half, axis=1)
    return x * c + up * a + dn * b


def _rope_kernel(x_ref, c_ref, ab_ref, swap_ref, o_ref, *, n_scaled, scale):
    j = pl.program_id(1)
    s = jnp.where(j < n_scaled, jnp.float32(scale), jnp.float32(1.0))
    c, ab, swap = c_ref[...], ab_ref[...], swap_ref[...]
    for blk in range(x_ref.shape[1] // LANES):
        sl = slice(blk * LANES, (blk + 1) * LANES)
        x = x_ref[:, sl]
        partner = jnp.dot(x, swap, preferred_element_type=F32)
        o_ref[:, sl] = ((x.astype(F32) * c + partner * ab) * s).astype(o_ref.dtype)


def _rope(p, seq, tables, half, lanes_per_head, *, n_cols, n_scaled_cols, scale, tc=1024):
    m = p.shape[0]
    tm = _tile(seq, 1024, SUBLANES)
    tc = _tile(math.gcd(n_cols, n_scaled_cols) if n_scaled_cols else n_cols, tc)
    sb = seq // tm
    c, a, b = tables
    swap = np.zeros((LANES, LANES), np.float32)
    for head in range(0, LANES, lanes_per_head):
        for r in range(half):
            swap[head + r + half, head + r] = 1.0
            swap[head + r, head + r + half] = 1.0
    tab_spec = pl.BlockSpec((tm, LANES), lambda i, j: (i % sb, 0))
    return pl.pallas_call(
        functools.partial(_rope_kernel, n_scaled=n_scaled_cols // tc, scale=scale),
        grid=(m // tm, n_cols // tc),
        in_specs=[pl.BlockSpec((tm, tc), lambda i, j: (i, j)), tab_spec, tab_spec,
                  pl.BlockSpec((LANES, LANES), lambda i, j: (0, 0))],
        out_specs=pl.BlockSpec((tm, tc), lambda i, j: (i, j)),
        out_shape=jax.ShapeDtypeStruct((m, n_cols), p.dtype),
        compiler_params=_params("parallel", "parallel"),
        name="rope",
    )(p, c, a + b, jnp.asarray(swap, p.dtype))


def _idx_prep_kernel(p_ref, g_ref, beta_ref, c_ref, a_ref, b_ref, ik_ref, iw_ref, *, half, iw_scale):
    x = p_ref[:, :HEAD]
    mu = jnp.mean(x, axis=-1, keepdims=True)
    var = jnp.mean(jnp.square(x - mu), axis=-1, keepdims=True)
    y = (x - mu) * lax.rsqrt(var + LN_EPS)
    y = y * g_ref[...] + beta_ref[...]
    ik_ref[...] = _rope_apply(y, c_ref[...], a_ref[...], b_ref[...], half).astype(ik_ref.dtype)
    iw_ref[...] = p_ref[:, HEAD:] * jnp.float32(iw_scale)


def _idx_prep(p2, seq, ik_g, ik_b, tables, half):
    m = p2.shape[0]
    tm = _tile(seq, 512, SUBLANES)
    sb = seq // tm
    tab_spec = pl.BlockSpec((tm, LANES), lambda i: (i % sb, 0))
    vec_spec = pl.BlockSpec((1, HEAD), lambda i: (0, 0))
    blk = pl.BlockSpec((tm, HEAD), lambda i: (i, 0))
    return pl.pallas_call(
        functools.partial(_idx_prep_kernel, half=half, iw_scale=IDX_HEADS ** -0.5 * HEAD ** -0.5),
        grid=(m // tm,),
        in_specs=[pl.BlockSpec((tm, 2 * HEAD), lambda i: (i, 0)), vec_spec, vec_spec, tab_spec, tab_spec, tab_spec],
        out_specs=[blk, blk],
        out_shape=[jax.ShapeDtypeStruct((m, HEAD), BF16), jax.ShapeDtypeStruct((m, HEAD), F32)],
        compiler_params=_params("parallel"),
        name="idx_prep",
    )(p2, ik_g.reshape(1, HEAD), ik_b.reshape(1, HEAD), *tables)


DSA_TQ = 128
DSA_KC = 512
IDX_TQ = 128


def _indexer_kernel(iq_ref, ik_ref, iw_ref, mask_ref, key_ref, *, n_sel, n_chunks):
    qb = pl.program_id(1)
    tq, kc = IDX_TQ, DSA_KC
    nch = (qb * tq + tq - 1) // kc + 1
    idx_bits = (n_chunks * kc - 1).bit_length()
    iw = iw_ref[0]
    q_pos = qb * tq + lax.broadcasted_iota(jnp.int32, (tq, 1), 0)
    lane = lax.broadcasted_iota(jnp.int32, (1, kc), 1)

    def score_chunk(c, _):
        ikc = ik_ref[0, pl.ds(pl.multiple_of(c * kc, kc), kc), :]
        score = jnp.zeros((tq, kc), F32)
        for h in range(IDX_HEADS):
            rel = lax.dot_general(iq_ref[0, :, h * HEAD:(h + 1) * HEAD], ikc, (((1,), (1,)), ((), ())),
                                  preferred_element_type=F32)
            score = score + jnp.maximum(rel, 0.0) * iw[:, h:h + 1]
        bits = pltpu.bitcast(score, jnp.int32)
        key = jnp.where(bits < 0, bits ^ jnp.int32(0x7FFFFFFF), bits)
        key = jnp.where(bits == jnp.int32(INT_MIN), jnp.int32(0), key)
        key = jnp.where(c * kc + lane <= q_pos, key, jnp.int32(INT_MIN))
        key_ref[c] = key
        return 0

    lax.fori_loop(0, nch, score_chunk, 0)

    lane128 = lax.broadcasted_iota(jnp.int32, (1, LANES), 1)

    def for_blocks(c, fn):
        key = key_ref[c]
        return [fn(key[:, b * LANES:(b + 1) * LANES], c * kc + b * LANES + lane128) for b in range(kc // LANES)]

    def count(pred):
        def body(c, acc):
            for hit in for_blocks(c, lambda key, pos: jnp.where(pred(key, pos), jnp.int32(1), jnp.int32(0))):
                acc = acc + hit
            return acc
        acc = lax.fori_loop(0, nch, body, jnp.zeros((tq, LANES), jnp.int32))
        return jnp.broadcast_to(jnp.sum(acc, axis=1, keepdims=True), (tq, LANES))

    def write_mask(sel_fn):
        def store(c, tile):
            for r in range(tq // DSA_TQ):
                mask_ref[0, r, c] = tile[r * DSA_TQ:(r + 1) * DSA_TQ].astype(mask_ref.dtype)

        def body(c, _):
            store(c, jnp.concatenate(for_blocks(c, lambda key, pos: jnp.where(sel_fn(key, pos), 1.0, 0.0)), axis=1))
            return 0
        lax.fori_loop(0, nch, body, 0)

        def tail(c, _):
            store(c, jnp.zeros((tq, kc), F32))
            return 0
        lax.fori_loop(nch, n_chunks, tail, 0)

    cnt0 = count(lambda key, pos: key >= 0)
    nonneg = cnt0 >= n_sel
    t0 = jnp.where(nonneg, jnp.int32(0), jnp.int32(INT_MIN))
    n0 = jnp.where(nonneg, cnt0, jnp.broadcast_to(q_pos + 1, (tq, LANES)))
    n_bits = 31
    bits_per_check = 4

    def bit_step(i, t, n_ge):
        bit = jnp.where(i < n_bits, jnp.left_shift(jnp.int32(1), jnp.maximum(jnp.int32(n_bits - 1) - i, 0)), 0)
        cand = t | bit
        cnt = count(lambda key, pos: key >= cand)
        take = cnt >= n_sel
        return jnp.where(take, cand, t), jnp.where(take, cnt, n_ge)

    def search_cond(state):
        i, _, n_ge = state
        return jnp.logical_and(i < n_bits, jnp.max(n_ge) > n_sel)

    def search_body(state):
        i, t, n_ge = state
        for b in range(bits_per_check):
            t, n_ge = bit_step(i + b, t, n_ge)
        return i + bits_per_check, t, n_ge

    _, thr, n_ge = lax.while_loop(search_cond, search_body, (jnp.int32(0), t0, n0))
    real = thr != jnp.int32(INT_MIN)
    excess = jnp.max(jnp.where(jnp.logical_and(real, n_ge > n_sel), 1, 0)) > 0

    @pl.when(jnp.logical_not(excess))
    def _():
        thr_eff = jnp.maximum(thr, jnp.int32(INT_MIN + 1))
        write_mask(lambda key, pos: key >= thr_eff)

    @pl.when(excess)
    def _():
        need = n_sel - count(lambda key, pos: key > thr)

        def idx_step(i, j):
            cand = j | jnp.left_shift(jnp.int32(1), jnp.int32(idx_bits - 1) - i)
            cnt = count(lambda key, pos: jnp.logical_and(key == thr, pos < cand))
            return jnp.where(cnt < need, cand, j)

        last = lax.fori_loop(0, idx_bits, idx_step, jnp.zeros((tq, LANES), jnp.int32))
        write_mask(lambda key, pos: jnp.logical_or(
            key > thr, jnp.logical_and(jnp.logical_and(key == thr, pos <= last), real)))


def _indexer(iq, ik, iw, n_sel):
    bsz, seq = ik.shape[0], ik.shape[1]
    n_chunks = seq // DSA_KC
    nq = seq // DSA_TQ
    per_step = IDX_TQ // DSA_TQ
    assert seq % IDX_TQ == 0 and IDX_TQ % DSA_TQ == 0
    iq_arr, iq_blk = iq
    return pl.pallas_call(
        functools.partial(_indexer_kernel, n_sel=n_sel, n_chunks=n_chunks),
        grid=(bsz, seq // IDX_TQ),
        in_specs=[pl.BlockSpec((1, IDX_TQ, IDX_HEADS * HEAD), lambda b, q: (b, q, iq_blk)),
                  pl.BlockSpec((1, seq, HEAD), lambda b, q: (b, 0, 0)),
                  pl.BlockSpec((1, IDX_TQ, HEAD), lambda b, q: (b, q, 0))],
        out_specs=pl.BlockSpec((1, per_step, n_chunks, DSA_TQ, DSA_KC), lambda b, q: (b, q, 0, 0, 0)),
        out_shape=jax.ShapeDtypeStruct((bsz, nq, n_chunks, DSA_TQ, DSA_KC), BF16),
        scratch_shapes=[pltpu.VMEM((n_chunks, IDX_TQ, DSA_KC), jnp.int32)],
        compiler_params=_params("parallel", "arbitrary"),
        name="dsa_indexer",
    )(iq_arr, ik, iw)


def _lane_blocks(x):
    return [x[:, j * LANES:(j + 1) * LANES] for j in range(x.shape[1] // LANES)]


STREAMS = 4


def _two_sweep_attention(nch, rows, logits_fn, pv_fn, s_ref, mx_ref, ls_ref, acc_ref):
    mx_ref[...] = jnp.full(mx_ref.shape, -jnp.inf, F32)

    def sweep1_chunk(c, last):
        for i, logits2 in enumerate(logits_fn(c, last)):
            sl = slice(i * rows, (i + 1) * rows)
            s_ref[c, sl, :] = logits2
            mx = mx_ref[sl, :]
            for blk in _lane_blocks(logits2):
                mx = jnp.maximum(mx, blk)
            mx_ref[sl, :] = mx

    def pairs_then_rest(n, chunk_fn):
        def run(first, count, width):
            def trip(t, _):
                for k in range(width):
                    chunk_fn(first + width * t + k)
                return 0
            lax.fori_loop(0, count, trip, 0)

        triples = n // 3
        rest = n - 3 * triples
        run(0, triples, 3)
        run(3 * triples, rest // 2, 2)
        run(3 * triples + 2 * (rest // 2), rest % 2, 1)

    pairs_then_rest(nch - 1, lambda c: sweep1_chunk(c, False))
    sweep1_chunk(nch - 1, True)

    row_max = jnp.max(mx_ref[...], axis=1, keepdims=True)
    mx_ref[...] = jnp.broadcast_to(row_max, mx_ref.shape)
    ls_ref[...] = jnp.zeros(ls_ref.shape, F32)
    acc_ref[...] = jnp.zeros(acc_ref.shape, F32)

    def sweep2_chunk(c):
        for i in range(STREAMS):
            sl = slice(i * rows, (i + 1) * rows)
            m = mx_ref[sl, :]
            ps = [jnp.exp2(blk - m) for blk in _lane_blocks(s_ref[c, sl, :])]
            ls_ref[sl, :] += sum(ps[1:], ps[0])
            acc_ref[sl, :] += pv_fn(c, i, jnp.concatenate(ps, axis=1).astype(BF16))

    pairs_then_rest(nch, sweep2_chunk)
    return acc_ref[...] / jnp.sum(ls_ref[...], axis=1, keepdims=True)


def _softmax_scratch(rows, n_chunks, kc):
    return [pltpu.VMEM((n_chunks, rows, kc), F32),
            pltpu.VMEM((rows, LANES), F32),
            pltpu.VMEM((rows, LANES), F32),
            pltpu.VMEM((rows, HEAD), F32)]


def _dsa_attn_kernel(q_ref, k_ref, v_ref, mask_ref, o_ref, s_ref, mx_ref, ls_ref, acc_ref, *, groups):
    qb = pl.program_id(2)
    tq, kc = DSA_TQ, DSA_KC
    nch = (qb * tq) // kc + 1
    rows = groups * tq
    qs = [jnp.concatenate([q_ref[0, :, (i * groups + r) * HEAD:(i * groups + r + 1) * HEAD] for r in range(groups)],
                          axis=0) for i in range(STREAMS)]

    def logits_fn(c, last):
        start = pl.multiple_of(c * kc, kc)
        sel = (mask_ref[0, 0, c].astype(F32) > 0.5)[None]
        out = []
        for i in range(STREAMS):
            k_c = k_ref[0, pl.ds(start, kc), i * HEAD:(i + 1) * HEAD]
            raw = lax.dot_general(qs[i], k_c, (((1,), (1,)), ((), ())), preferred_element_type=F32)
            out.append(jnp.where(sel, raw.reshape(groups, tq, kc), NEG_INF).reshape(rows, kc))
        return out

    def pv_fn(c, i, p):
        v_c = v_ref[0, pl.ds(pl.multiple_of(c * kc, kc), kc), i * HEAD:(i + 1) * HEAD]
        return jnp.dot(p, v_c, preferred_element_type=F32)

    out = _two_sweep_attention(nch, rows, logits_fn, pv_fn, s_ref, mx_ref, ls_ref, acc_ref)
    for r in range(STREAMS * groups):
        o_ref[0, :, r * HEAD:(r + 1) * HEAD] = out[r * tq:(r + 1) * tq].astype(o_ref.dtype)


def _dsa_attention(q, k, v, mask, bsz, seq):
    groups = A_HEADS // A_KV_HEADS
    (q_arr, q0), (k_arr, k0), (v_arr, v0) = q, k, v
    nq = seq // DSA_TQ
    n_chunks = seq // DSA_KC
    qw = STREAMS * groups * HEAD
    kw = STREAMS * HEAD
    assert q0 % (STREAMS * groups) == 0 and k0 % STREAMS == 0 and v0 % STREAMS == 0
    return pl.pallas_call(
        functools.partial(_dsa_attn_kernel, groups=groups),
        grid=(bsz, A_KV_HEADS // STREAMS, nq),
        in_specs=[pl.BlockSpec((1, DSA_TQ, qw), lambda b, g, i: (b, i, q0 // (STREAMS * groups) + g)),
                  pl.BlockSpec((1, seq, kw), lambda b, g, i: (b, 0, k0 // STREAMS + g), pipeline_mode=pl.Buffered(1)),
                  pl.BlockSpec((1, seq, kw), lambda b, g, i: (b, 0, v0 // STREAMS + g), pipeline_mode=pl.Buffered(1)),
                  pl.BlockSpec((1, 1, n_chunks, DSA_TQ, DSA_KC), lambda b, g, i: (b, i, 0, 0, 0))],
        out_specs=pl.BlockSpec((1, DSA_TQ, qw), lambda b, g, i: (b, i, g)),
        out_shape=jax.ShapeDtypeStruct((bsz, seq, A_HEADS * HEAD), BF16),
        scratch_shapes=_softmax_scratch(STREAMS * groups * DSA_TQ, n_chunks, DSA_KC),
        compiler_params=_params("parallel", "parallel", "arbitrary"),
        name="dsa_attention",
    )(q_arr, k_arr, v_arr, mask)


def _fox_cum_kernel(f_ref, bias_ref, cum_t_ref):
    z = f_ref[0] + bias_ref[...]
    x = jnp.minimum(z, 0.0) - jnp.log1p(jnp.exp(-jnp.abs(z)))
    seq = x.shape[0]
    row = lax.broadcasted_iota(jnp.int32, (seq, 1), 0)
    d = 1
    while d < seq:
        x = x + jnp.where(row >= d, pltpu.roll(x, d, axis=0), 0.0)
        d *= 2
    cum_t_ref[0] = (x * jnp.float32(LOG2E)).T


def _fox_cum(f, bias):
    bsz, seq, _ = f.shape
    return pl.pallas_call(
        _fox_cum_kernel,
        grid=(bsz,),
        in_specs=[pl.BlockSpec((1, seq, HEAD), lambda b: (b, 0, 0)), pl.BlockSpec((1, HEAD), lambda b: (0, 0))],
        out_specs=pl.BlockSpec((1, HEAD, seq), lambda b: (b, 0, 0)),
        out_shape=jax.ShapeDtypeStruct((bsz, HEAD, seq), F32),
        compiler_params=_params("parallel"),
        name="fox_cum",
    )(f, bias)


FOX_TQ = 512
FOX_KC = 512


def _fox_attn_kernel(q_ref, k_ref, v_ref, cq_ref, ck_ref, o_ref, s_ref, mx_ref, ls_ref, acc_ref, *, scale2):
    qb = pl.program_id(2)
    tq, kc = FOX_TQ, FOX_KC
    heads = [slice(i * HEAD, (i + 1) * HEAD) for i in range(STREAMS)]
    qs = [(q_ref[0, :, h].astype(F32) * jnp.float32(scale2)).astype(q_ref.dtype) for h in heads]
    cqs = [jnp.broadcast_to(cq_ref[0, i, 0], (LANES, tq)).T for i in range(STREAMS)]

    def logits_fn(c, last):
        start = pl.multiple_of(c * kc, kc)
        out = []
        for i in range(STREAMS):
            raw = lax.dot_general(qs[i], k_ref[0, pl.ds(start, kc), heads[i]], (((1,), (1,)), ((), ())),
                                  preferred_element_type=F32)
            ck = ck_ref[0, i, c]
            blocks = [blk + cqs[i] - ck[:, j * LANES:(j + 1) * LANES] for j, blk in enumerate(_lane_blocks(raw))]
            logits2 = jnp.concatenate(blocks, axis=1)
            if last:
                row = lax.broadcasted_iota(jnp.int32, (tq, 1), 0)
                lane = lax.broadcasted_iota(jnp.int32, (1, kc), 1)
                logits2 = jnp.where(lane <= row, logits2, NEG_INF)
            out.append(logits2)
        return out

    def pv_fn(c, i, p):
        v_c = v_ref[0, pl.ds(pl.multiple_of(c * kc, kc), kc), heads[i]]
        return jnp.dot(p, v_c, preferred_element_type=F32)

    out = _two_sweep_attention(qb + 1, tq, logits_fn, pv_fn, s_ref, mx_ref, ls_ref, acc_ref)
    for i in range(STREAMS):
        o_ref[0, :, heads[i]] = out[i * tq:(i + 1) * tq].astype(o_ref.dtype)


def _fox_attention(p, cum_rows, bsz, seq):
    tq = FOX_TQ
    assert FOX_TQ == FOX_KC and seq % tq == 0
    n_chunks = seq // FOX_KC
    hw = STREAMS * HEAD
    cq_rows = cum_rows.reshape(bsz, B_HEADS, 1, 1, seq)
    return pl.pallas_call(
        functools.partial(_fox_attn_kernel, scale2=HEAD ** -0.5 * LOG2E),
        grid=(bsz, B_HEADS // STREAMS, seq // tq),
        in_specs=[pl.BlockSpec((1, tq, hw), lambda b, h, i: (b, i, h)),
                  pl.BlockSpec((1, seq, hw), lambda b, h, i: (b, 0, B_HEADS // STREAMS + h),
                               pipeline_mode=pl.Buffered(1)),
                  pl.BlockSpec((1, seq, hw), lambda b, h, i: (b, 0, 2 * B_HEADS // STREAMS + h),
                               pipeline_mode=pl.Buffered(1)),
                  pl.BlockSpec((1, STREAMS, 1, 1, tq), lambda b, h, i: (b, h, 0, 0, i)),
                  pl.BlockSpec((1, STREAMS, n_chunks, 1, FOX_KC), lambda b, h, i: (b, h, 0, 0, 0))],
        out_specs=pl.BlockSpec((1, tq, hw), lambda b, h, i: (b, i, h)),
        out_shape=jax.ShapeDtypeStruct((bsz, seq, B_HEADS * HEAD), BF16),
        scratch_shapes=_softmax_scratch(STREAMS * tq, n_chunks, FOX_KC),
        compiler_params=_params("parallel", "parallel", "arbitrary"),
        name="fox_attention",
    )(p, p, p, cq_rows, cum_rows)


def _swa_kernel(sinks_ref, q_ref, kp_ref, kc_ref, vp_ref, vc_ref, o_ref):
    n = pl.program_id(1)
    pair = pl.program_id(2)
    w, dh = WINDOW, C_HEAD_DIM
    groups = C_HEADS // C_KV_HEADS
    kv_per_blk = LANES // dh
    i = lax.broadcasted_iota(jnp.int32, (w, 1), 0)
    j = lax.broadcasted_iota(jnp.int32, (1, 2 * w), 1)
    valid = jnp.logical_and(jnp.logical_and(j > i, j <= i + w), n * w + j >= w)
    kk = jnp.concatenate([kp_ref[0], kc_ref[0]], axis=0)
    vv = jnp.concatenate([vp_ref[0], vc_ref[0]], axis=0)
    for g in range(kv_per_blk):
        k_g = kk[:, g * dh:(g + 1) * dh]
        v_g = vv[:, g * dh:(g + 1) * dh]
        for r in range(groups):
            col = (g * groups + r) * dh
            q = q_ref[0, :, col:col + dh]
            s = lax.dot_general(q, k_g, (((1,), (1,)), ((), ())), preferred_element_type=F32)
            logits = jnp.where(valid, s, NEG_INF)
            sink = sinks_ref[(pair * kv_per_blk + g) * groups + r]
            m = jnp.maximum(jnp.max(logits, axis=1, keepdims=True), sink)
            p = jnp.exp(logits - m)
            denom = jnp.sum(p, axis=1, keepdims=True) + jnp.exp(sink - m)
            o = jnp.dot((p / denom).astype(v_g.dtype), v_g, preferred_element_type=F32)
            o_ref[0, :, col:col + dh] = o.astype(o_ref.dtype)


def _swa_attention(pr, p, sinks, bsz, seq):
    w = WINDOW
    nb = seq // w
    kv_per_blk = LANES // C_HEAD_DIM
    pairs = C_KV_HEADS // kv_per_blk
    qw = kv_per_blk * (C_HEADS // C_KV_HEADS) * C_HEAD_DIM
    k0 = C_HEADS * C_HEAD_DIM // LANES
    v0 = k0 + C_KV_HEADS * C_HEAD_DIM // LANES
    prev = lambda col0: (lambda b, n, g: (b, jnp.maximum(n - 1, 0), col0 + g))
    cur = lambda col0: (lambda b, n, g: (b, n, col0 + g))
    blk = (1, w, LANES)
    return pl.pallas_call(
        _swa_kernel,
        grid=(bsz, nb, pairs),
        in_specs=[pl.BlockSpec(memory_space=pltpu.SMEM),
                  pl.BlockSpec((1, w, qw), lambda b, n, g: (b, n, g)),
                  pl.BlockSpec(blk, prev(k0)), pl.BlockSpec(blk, cur(k0)),
                  pl.BlockSpec(blk, prev(v0)), pl.BlockSpec(blk, cur(v0))],
        out_specs=pl.BlockSpec((1, w, qw), lambda b, n, g: (b, n, g)),
        out_shape=jax.ShapeDtypeStruct((bsz, seq, C_HEADS * C_HEAD_DIM), BF16),
        compiler_params=_params("parallel", "parallel", "parallel"),
        name="swa_attention",
    )(sinks, pr, pr, pr, p, p)


FFN_SUB_BLOCKS = 4
FFN_CAST_SLICES = 2


def _ffn_up_kernel(h_ref, wg_ref, wv_ref, cwg_ref, cwv_ref, cbg_ref, cbv_ref, o_ref, w_scr, u_scr, carry,
                   *, tiles_per_seq):
    mi = pl.program_id(0)
    ni = pl.program_id(1)
    tm = h_ref.shape[0]
    tn = wg_ref.shape[1]
    kdim = h_ref.shape[1]

    def cast_weights(k0, k1):
        w_scr[k0:k1, :tn] = wg_ref[k0:k1, :].astype(w_scr.dtype)
        w_scr[k0:k1, tn:] = wv_ref[k0:k1, :].astype(w_scr.dtype)

    @pl.when((mi % tiles_per_seq) == 0)
    def _():
        carry[ni] = jnp.zeros(carry.shape[1:], F32)

    u_scr[:SUBLANES, :] = carry[ni]
    sub = tm // FFN_SUB_BLOCKS

    def matmul(s, k0=0, k1=None, accumulate=False):
        k1 = kdim if k1 is None else k1
        rows = slice(s * sub, (s + 1) * sub)
        dst = slice(SUBLANES + s * sub, SUBLANES + (s + 1) * sub)
        part = jnp.dot(h_ref[rows, k0:k1], w_scr[k0:k1, :], preferred_element_type=F32)
        u_scr[dst, :] = u_scr[dst, :] + part if accumulate else part

    def conv(s, cols, cw_ref, cb_ref):
        window = u_scr[s * sub:s * sub + SUBLANES + sub, cols]
        acc = window * cw_ref[0:1, :]
        for tap in range(1, CONV_WIDTH):
            acc = pltpu.roll(acc, 1, axis=0) + window * cw_ref[tap:tap + 1, :]
        return cb_ref[...] + acc[SUBLANES:]

    def epilogue(s):
        gate = conv(s, slice(0, tn), cwg_ref, cbg_ref)
        val = conv(s, slice(tn, 2 * tn), cwv_ref, cbv_ref)
        half_gate = 0.5 * gate
        silu = half_gate * jnp.tanh(half_gate) + half_gate
        o_ref[s * sub:(s + 1) * sub, :] = (silu * val).astype(o_ref.dtype)

    kstep = kdim // FFN_CAST_SLICES
    for ks in range(FFN_CAST_SLICES):
        cast_weights(ks * kstep, (ks + 1) * kstep)
        matmul(0, ks * kstep, (ks + 1) * kstep, accumulate=ks > 0)
    for s in range(1, FFN_SUB_BLOCKS):
        matmul(s)
        epilogue(s - 1)
    epilogue(FFN_SUB_BLOCKS - 1)
    carry[ni] = u_scr[tm:, :]


def _ffn_up(h, w_up, conv_w, conv_b, layer, seq, *, tm=2048, tn=256):
    m, d = h.shape
    dff = w_up.shape[2] // 2
    tm = _tile(seq, tm, SUBLANES)
    tn = _tile(dff, tn)
    nt = dff // tn
    assert CONV_WIDTH - 1 <= SUBLANES and tm % (FFN_SUB_BLOCKS * SUBLANES) == 0
    conv_b = conv_b.reshape(conv_b.shape[0], 1, 2 * dff)
    return pl.pallas_call(
        functools.partial(_ffn_up_kernel, tiles_per_seq=seq // tm),
        grid=(m // tm, nt),
        in_specs=[pl.BlockSpec((tm, d), lambda i, j: (i, 0), pipeline_mode=pl.Buffered(1)),
                  pl.BlockSpec((None, d, tn), lambda i, j: (layer, 0, j)),
                  pl.BlockSpec((None, d, tn), lambda i, j: (layer, 0, j + nt)),
                  pl.BlockSpec((None, CONV_WIDTH, tn), lambda i, j: (layer, 0, j)),
                  pl.BlockSpec((None, CONV_WIDTH, tn), lambda i, j: (layer, 0, j + nt)),
                  pl.BlockSpec((None, 1, tn), lambda i, j: (layer, 0, j)),
                  pl.BlockSpec((None, 1, tn), lambda i, j: (layer, 0, j + nt))],
        out_specs=pl.BlockSpec((tm, tn), lambda i, j: (i, j)),
        out_shape=jax.ShapeDtypeStruct((m, dff), BF16),
        scratch_shapes=[pltpu.VMEM((d, 2 * tn), BF16),
                        pltpu.VMEM((SUBLANES + tm, 2 * tn), F32),
                        pltpu.VMEM((nt, SUBLANES, 2 * tn), F32)],
        compiler_params=_params("arbitrary", "arbitrary"),
        name="ffn_up",
    )(h, w_up, w_up, conv_w, conv_w, conv_b, conv_b)


OUT_PROJ_TN = 512


def _pad_rows(w, n):
    return jnp.pad(w, ((0, 0), (0, n - w.shape[1]), (0, 0)))


def _dsa_weight_col(j, *, tn):
    nq, niq, nkv2 = A_HEADS * HEAD // tn, IDX_HEADS * HEAD // tn, 2 * A_KV_HEADS * HEAD // tn
    return jnp.where(j < nq, j, jnp.where(j < nq + niq, j + nkv2, j - niq))


def _dsa_mixer(h, row_ss, x, w_in, w_small, ik_g, ik_b, w_out, slot, bsz, seq):
    a_q, a_kv, a_iq = A_HEADS * HEAD, A_KV_HEADS * HEAD, IDX_HEADS * HEAD
    p = _matmul(h, w_in, row_ss=row_ss, n=a_q + a_iq + 2 * a_kv, w_layer=slot, w_is_nk=True,
                col_block=_dsa_weight_col, out_dtype=BF16)
    p2 = _matmul(h, w_small, row_ss=row_ss, w_layer=slot, w_is_nk=True, out_dtype=F32)
    tables, half = _rope_tables(seq, HEAD, HEAD)
    n_rope = a_q + a_iq + a_kv
    pr = _rope(p, seq, tables, half, HEAD, n_cols=n_rope, n_scaled_cols=a_q, scale=HEAD ** -0.5 * LOG2E)
    ik, iw = _idx_prep(p2, seq, ik_g, ik_b, tables, half)
    pr3 = pr.reshape(bsz, seq, n_rope)
    p3 = p.reshape(bsz, seq, p.shape[1])
    n_sel = min(INDEX_TOPK, seq // 4)
    mask = _indexer((pr3, a_q // a_iq), ik.reshape(bsz, seq, HEAD), iw.reshape(bsz, seq, HEAD), n_sel)
    o = _dsa_attention((pr3, 0), (pr3, (a_q + a_iq) // HEAD), (p3, n_rope // HEAD), mask, bsz, seq)
    return _matmul(o.reshape(bsz * seq, a_q), w_out, w_layer=slot, res=x, out_dtype=F32, tn=OUT_PROJ_TN)


def _fox_mixer(h, row_ss, x, w_in, w_f, f_bias, w_out, slot, bsz, seq):
    d = h.shape[1]
    p = _matmul(h, w_in, row_ss=row_ss, n=3 * d, w_layer=slot, w_is_nk=True, out_dtype=BF16)
    f = _matmul(h, w_f, row_ss=row_ss, w_layer=slot, w_is_nk=True, out_dtype=F32, tn=HEAD)
    bias = jnp.pad(f_bias.astype(F32), (0, HEAD - B_HEADS)).reshape(1, HEAD)
    cum_t = _fox_cum(f.reshape(bsz, seq, HEAD), bias)
    cum_rows = cum_t[:, :B_HEADS].reshape(bsz, B_HEADS, seq // FOX_KC, 1, FOX_KC)
    o = _fox_attention(p.reshape(bsz, seq, 3 * d), cum_rows, bsz, seq)
    return _matmul(o.reshape(bsz * seq, d), w_out, w_layer=slot, res=x, out_dtype=F32, tn=OUT_PROJ_TN)


def _swa_mixer(h, row_ss, x, w_in, sinks, w_out, slot, bsz, seq):
    c_q, c_kv = C_HEADS * C_HEAD_DIM, C_KV_HEADS * C_HEAD_DIM
    p = _matmul(h, w_in, row_ss=row_ss, w_layer=slot, out_dtype=BF16)
    tables, half = _rope_tables(seq, C_HEAD_DIM, C_HEAD_DIM)
    pr = _rope(p, seq, tables, half, C_HEAD_DIM, n_cols=c_q + c_kv, n_scaled_cols=c_q, scale=C_HEAD_DIM ** -0.5)
    o = _swa_attention(pr.reshape(bsz, seq, c_q + c_kv), p.reshape(bsz, seq, p.shape[1]),
                       sinks.astype(F32), bsz, seq)
    return _matmul(o.reshape(bsz * seq, c_q), w_out, w_layer=slot, res=x, out_dtype=F32, tn=OUT_PROJ_TN)


def _conv_glu_ffn(h, x, w_up, conv_w, conv_b, w_down, layer, next_gain, seq):
    g = _ffn_up(h, w_up, conv_w, conv_b, layer, seq)
    return _matmul(g, w_down, w_layer=layer, res=x, gain=next_gain, out_dtype=F32, tm=512, tn=512)


def kernel(x, attn_norm, ffn_norm, final_norm, a_w_in, a_idx_k_norm_g, a_idx_k_norm_b, a_w_out,
           b_w_in, b_f_bias, b_w_out, c_w_in, c_sinks, c_w_out,
           ffn_w_up, ffn_conv_w, ffn_conv_b, ffn_w_down):
    bsz, seq, d = x.shape
    depth = attn_norm.shape[0]
    a_main = A_HEADS * HEAD + 2 * A_KV_HEADS * HEAD + IDX_HEADS * HEAD
    a_w_in, b_w_in = (jnp.swapaxes(w, 1, 2).astype(BF16) for w in (a_w_in, b_w_in))
    a_w_out, b_w_out, c_w_in, c_w_out, ffn_w_down = (
        w.astype(BF16) for w in (a_w_out, b_w_out, c_w_in, c_w_out, ffn_w_down))
    a_w_small = _pad_rows(a_w_in[:, a_main:, :], 2 * HEAD)
    b_w_f = _pad_rows(b_w_in[:, 3 * d:, :], HEAD)
    x = x.reshape(bsz * seq, d)
    h, row_ss = _rmsnorm(x, attn_norm[0], BF16), None
    for layer in range(depth):
        mixer, slot = layer % N_MIXERS, layer // N_MIXERS
        if mixer == 0:
            x = _dsa_mixer(h, row_ss, x, a_w_in, a_w_small, a_idx_k_norm_g[slot], a_idx_k_norm_b[slot], a_w_out,
                           slot, bsz, seq)
        elif mixer == 1:
            x = _fox_mixer(h, row_ss, x, b_w_in, b_w_f, b_f_bias[slot], b_w_out, slot, bsz, seq)
        else:
            x = _swa_mixer(h, row_ss, x, c_w_in, c_sinks[slot], c_w_out, slot, bsz, seq)
        h = _rmsnorm(x, ffn_norm[layer], BF16)
        if layer + 1 < depth:
            x, h, row_ss = _conv_glu_ffn(h, x, ffn_w_up, ffn_conv_w, ffn_conv_b, ffn_w_down, layer,
                                         attn_norm[layer + 1], seq)
        else:
            x = _conv_glu_ffn(h, x, ffn_w_up, ffn_conv_w, ffn_conv_b, ffn_w_down, layer, None, seq)
    return _rmsnorm(x, final_norm, F32).reshape(bsz, seq, d)
```

```python
import functools
import math

import jax
import jax.numpy as jnp
import numpy as np
from jax import lax
from jax.experimental import pallas as pl
from jax.experimental.pallas import tpu as pltpu

F32 = jnp.float32
BF16 = jnp.bfloat16

N_MIXERS = 3
ROPE_THETA = 500000.0
ROPE_FRACTION = 4
NORM_EPS = 1e-6
LN_EPS = 1e-6
NEG_INF = -1e30
LOG2E = math.log2(math.e)
HEAD = 128
A_HEADS, A_KV_HEADS, IDX_HEADS, INDEX_TOPK = 32, 8, 16, 256
B_HEADS = 32
C_HEADS, C_KV_HEADS, C_HEAD_DIM, WINDOW = 64, 8, 64, 128
CONV_WIDTH = 3

LANES = 128
SUBLANES = 8
VMEM_LIMIT_BYTES = 56 * 1024 * 1024
INT_MIN = -(2 ** 31)


def _params(*sem):
    return pltpu.CompilerParams(dimension_semantics=sem, vmem_limit_bytes=VMEM_LIMIT_BYTES)


def _tile(dim, target, quantum=LANES):
    if dim <= target:
        return dim
    best = None
    for t in range(quantum, target + 1, quantum):
        if dim % t == 0:
            best = t
    assert best is not None, (dim, target)
    return best


def _rmsnorm_kernel(x_ref, g_ref, o_ref):
    x = x_ref[...]
    y = x * lax.rsqrt(jnp.mean(x * x, axis=-1, keepdims=True) + NORM_EPS)
    o_ref[...] = (y * g_ref[...]).astype(o_ref.dtype)


def _rmsnorm(x, g, out_dtype):
    m, d = x.shape
    tm = _tile(m, 512, SUBLANES)
    return pl.pallas_call(
        _rmsnorm_kernel,
        grid=(m // tm,),
        in_specs=[pl.BlockSpec((tm, d), lambda i: (i, 0)), pl.BlockSpec((1, d), lambda i: (0, 0))],
        out_specs=pl.BlockSpec((tm, d), lambda i: (i, 0)),
        out_shape=jax.ShapeDtypeStruct((m, d), out_dtype),
        compiler_params=_params("parallel"),
        name="rmsnorm",
    )(x, g.reshape(1, d))


def _row_scale(ss_ref, d):
    return lax.rsqrt(jnp.sum(ss_ref[...], axis=1, keepdims=True) * (1.0 / d) + NORM_EPS)


def _mm_kernel(*refs, w_is_nk, has_scale, has_res, has_gain, rope):
    refs = list(refs)
    a_ref, w_ref = refs[:2]
    rest = refs[2:]
    ss_ref = rest.pop(0) if has_scale else None
    r_ref = rest.pop(0) if has_res else None
    g_ref = rest.pop(0) if has_gain else None
    rope_refs = [rest.pop(0) for _ in range(3)] if rope else None
    o_ref = rest.pop(0)
    w_contract = 1 if w_is_nk else 0
    out = lax.dot_general(a_ref[...], w_ref[...], (((1,), (w_contract,)), ((), ())), preferred_element_type=F32)
    if has_scale:
        out = out * _row_scale(ss_ref, a_ref.shape[1])
    if has_res:
        out = out + r_ref[...]
    if rope:
        rope_tiles, scaled_tiles, scale = rope
        j = pl.program_id(1)
        y = out.astype(o_ref.dtype)

        @pl.when(j < rope_tiles)
        def _():
            c, ab, swap = (r[...] for r in rope_refs)
            s = jnp.where(j < scaled_tiles, jnp.float32(scale), jnp.float32(1.0))
            for blk in range(y.shape[1] // LANES):
                sl = slice(blk * LANES, (blk + 1) * LANES)
                x = y[:, sl]
                partner = jnp.dot(x, swap, preferred_element_type=F32)
                o_ref[:, sl] = ((x.astype(F32) * c + partner * ab) * s).astype(o_ref.dtype)

        @pl.when(j >= rope_tiles)
        def _():
            o_ref[...] = y
    else:
        o_ref[...] = out.astype(o_ref.dtype)
    if has_gain:
        xg_ref, ss_out_ref = rest
        xg_ref[...] = (out * g_ref[...]).astype(xg_ref.dtype)
        squares = _lane_blocks(out * out)
        part = sum(squares[1:], squares[0])
        j = pl.program_id(1)

        @pl.when(j == 0)
        def _():
            ss_out_ref[...] = part

        @pl.when(j > 0)
        def _():
            ss_out_ref[...] += part


def _matmul(a, w, *, res=None, row_ss=None, gain=None, rope=None, out_dtype, n=None, w_layer=None, w_is_nk=False,
            col_block=None, tm=1024, tn=1024):
    m, kdim = a.shape
    n = w.shape[1 if w_is_nk else 2] if n is None else n
    tm = _tile(m, tm, SUBLANES)
    tn = _tile(n, tn)
    col = (lambda j: j) if col_block is None else functools.partial(col_block, tn=tn)
    if w_is_nk:
        w_spec = pl.BlockSpec((None, tn, kdim), lambda i, j: (w_layer, col(j), 0))
    else:
        w_spec = pl.BlockSpec((None, kdim, tn), lambda i, j: (w_layer, 0, col(j)))
    tile = pl.BlockSpec((tm, tn), lambda i, j: (i, j))
    rows = pl.BlockSpec((tm, LANES), lambda i, j: (i, 0))
    in_specs = [pl.BlockSpec((tm, kdim), lambda i, j: (i, 0)), w_spec]
    args = [a, w]
    if row_ss is not None:
        in_specs.append(rows)
        args.append(row_ss)
    if res is not None:
        in_specs.append(tile)
        args.append(res)
    out_specs, out_shape = tile, jax.ShapeDtypeStruct((m, n), out_dtype)
    if gain is not None:
        in_specs.append(pl.BlockSpec((1, tn), lambda i, j: (0, j)))
        args.append(gain.reshape(1, n))
        out_specs = [tile, tile, rows]
        out_shape = [out_shape, jax.ShapeDtypeStruct((m, n), BF16), jax.ShapeDtypeStruct((m, LANES), F32)]
    rope_static = None
    if rope is not None:
        seq, (c, ab, swap), n_rope, n_scaled, scale = rope
        assert seq % tm == 0 and n_rope % tn == 0 and n_scaled % tn == 0
        sb = seq // tm
        table = pl.BlockSpec((tm, LANES), lambda i, j: (i % sb, 0))
        in_specs += [table, table, pl.BlockSpec((LANES, LANES), lambda i, j: (0, 0))]
        args += [c, ab, swap]
        rope_static = (n_rope // tn, n_scaled // tn, scale)
    return pl.pallas_call(
        functools.partial(_mm_kernel, w_is_nk=w_is_nk, has_scale=row_ss is not None, has_res=res is not None,
                          has_gain=gain is not None, rope=rope_static),
        grid=(m // tm, n // tn),
        in_specs=in_specs,
        out_specs=out_specs,
        out_shape=out_shape,
        compiler_params=_params("parallel", "arbitrary" if gain is not None else "parallel"),
        name="matmul",
    )(*args)


def _rope_tables(seq, head_dim, lanes_per_head):
    rot = head_dim // ROPE_FRACTION
    half = rot // 2
    inv_freq = jnp.power(jnp.float32(ROPE_THETA), -jnp.arange(half, dtype=F32) * (2.0 / rot))
    ang = jnp.arange(seq, dtype=F32)[:, None] * inv_freq[None, :]
    cos, sin = jnp.cos(ang), jnp.sin(ang)
    ones = jnp.ones((seq, lanes_per_head - rot), F32)
    zeros_h = jnp.zeros((seq, half), F32)
    zeros_r = jnp.zeros((seq, lanes_per_head - rot), F32)
    c = jnp.concatenate([cos, cos, ones], axis=1)
    a = jnp.concatenate([-sin, zeros_h, zeros_r], axis=1)
    b = jnp.concatenate([zeros_h, sin, zeros_r], axis=1)
    reps = LANES // lanes_per_head
    return tuple(jnp.tile(t, (1, reps)) for t in (c, a, b)), half


def _rope_apply(x, c, a, b, half):
    up = pltpu.roll(x, LANES - half, axis=1)
    dn = pltpu.roll(x, half, axis=1)
    return x * c + up * a + dn * b


def _rope_operands(tables, half, lanes_per_head):
    c, a, b = tables
    swap = np.zeros((LANES, LANES), np.float32)
    for head in range(0, LANES, lanes_per_head):
        for r in range(half):
            swap[head + r + half, head + r] = 1.0
            swap[head + r, head + r + half] = 1.0
    return c, a + b, jnp.asarray(swap, BF16)


def _idx_prep_kernel(p_ref, g_ref, beta_ref, c_ref, a_ref, b_ref, ik_ref, iw_ref, *, half, iw_scale):
    x = p_ref[:, :HEAD]
    mu = jnp.mean(x, axis=-1, keepdims=True)
    var = jnp.mean(jnp.square(x - mu), axis=-1, keepdims=True)
    y = (x - mu) * lax.rsqrt(var + LN_EPS)
    y = y * g_ref[...] + beta_ref[...]
    ik_ref[...] = _rope_apply(y, c_ref[...], a_ref[...], b_ref[...], half).astype(ik_ref.dtype)
    iw_ref[...] = p_ref[:, HEAD:] * jnp.float32(iw_scale)


def _idx_prep(p2, seq, ik_g, ik_b, tables, half):
    m = p2.shape[0]
    tm = _tile(seq, 512, SUBLANES)
    sb = seq // tm
    tab_spec = pl.BlockSpec((tm, LANES), lambda i: (i % sb, 0))
    vec_spec = pl.BlockSpec((1, HEAD), lambda i: (0, 0))
    blk = pl.BlockSpec((tm, HEAD), lambda i: (i, 0))
    return pl.pallas_call(
        functools.partial(_idx_prep_kernel, half=half, iw_scale=IDX_HEADS ** -0.5 * HEAD ** -0.5),
        grid=(m // tm,),
        in_specs=[pl.BlockSpec((tm, 2 * HEAD), lambda i: (i, 0)), vec_spec, vec_spec, tab_spec, tab_spec, tab_spec],
        out_specs=[blk, blk],
        out_shape=[jax.ShapeDtypeStruct((m, HEAD), BF16), jax.ShapeDtypeStruct((m, HEAD), F32)],
        compiler_params=_params("parallel"),
        name="idx_prep",
    )(p2, ik_g.reshape(1, HEAD), ik_b.reshape(1, HEAD), *tables)


DSA_TQ = 128
DSA_KC = 512
IDX_TQ = 128


def _indexer_kernel(iq_ref, ik_ref, iw_ref, mask_ref, key_ref, *, n_sel, n_chunks):
    qb = pl.program_id(1)
    tq, kc = IDX_TQ, DSA_KC
    nch = (qb * tq + tq - 1) // kc + 1
    idx_bits = (n_chunks * kc - 1).bit_length()
    iw = iw_ref[0]
    q_pos = qb * tq + lax.broadcasted_iota(jnp.int32, (tq, 1), 0)
    lane = lax.broadcasted_iota(jnp.int32, (1, kc), 1)

    def score_chunk(c, _):
        ikc = ik_ref[0, pl.ds(pl.multiple_of(c * kc, kc), kc), :]
        score = jnp.zeros((tq, kc), F32)
        for h in range(IDX_HEADS):
            rel = lax.dot_general(iq_ref[0, :, h * HEAD:(h + 1) * HEAD], ikc, (((1,), (1,)), ((), ())),
                                  preferred_element_type=F32)
            score = score + jnp.maximum(rel, 0.0) * iw[:, h:h + 1]
        bits = pltpu.bitcast(score, jnp.int32)
        key = jnp.where(bits < 0, bits ^ jnp.int32(0x7FFFFFFF), bits)
        key = jnp.where(bits == jnp.int32(INT_MIN), jnp.int32(0), key)
        key = jnp.where(c * kc + lane <= q_pos, key, jnp.int32(INT_MIN))
        key_ref[c] = key
        return 0

    lax.fori_loop(0, nch, score_chunk, 0)

    lane128 = lax.broadcasted_iota(jnp.int32, (1, LANES), 1)

    def for_blocks(c, fn):
        key = key_ref[c]
        return [fn(key[:, b * LANES:(b + 1) * LANES], c * kc + b * LANES + lane128) for b in range(kc // LANES)]

    def count(pred):
        def body(c, acc):
            for hit in for_blocks(c, lambda key, pos: jnp.where(pred(key, pos), jnp.int32(1), jnp.int32(0))):
                acc = acc + hit
            return acc
        acc = lax.fori_loop(0, nch, body, jnp.zeros((tq, LANES), jnp.int32))
        return jnp.broadcast_to(jnp.sum(acc, axis=1, keepdims=True), (tq, LANES))

    def write_mask(sel_fn):
        def store(c, tile):
            for r in range(tq // DSA_TQ):
                mask_ref[0, r, c] = tile[r * DSA_TQ:(r + 1) * DSA_TQ].astype(mask_ref.dtype)

        def body(c, _):
            store(c, jnp.concatenate(for_blocks(c, lambda key, pos: jnp.where(sel_fn(key, pos), 1.0, 0.0)), axis=1))
            return 0
        lax.fori_loop(0, nch, body, 0)

        def tail(c, _):
            store(c, jnp.zeros((tq, kc), F32))
            return 0
        lax.fori_loop(nch, n_chunks, tail, 0)

    cnt0 = count(lambda key, pos: key >= 0)
    nonneg = cnt0 >= n_sel
    t0 = jnp.where(nonneg, jnp.int32(0), jnp.int32(INT_MIN))
    n0 = jnp.where(nonneg, cnt0, jnp.broadcast_to(q_pos + 1, (tq, LANES)))
    n_bits = 31
    bits_per_check = 4

    def bit_step(i, t, n_ge):
        bit = jnp.where(i < n_bits, jnp.left_shift(jnp.int32(1), jnp.maximum(jnp.int32(n_bits - 1) - i, 0)), 0)
        cand = t | bit
        cnt = count(lambda key, pos: key >= cand)
        take = cnt >= n_sel
        return jnp.where(take, cand, t), jnp.where(take, cnt, n_ge)

    def search_cond(state):
        i, _, n_ge = state
        return jnp.logical_and(i < n_bits, jnp.max(n_ge) > n_sel)

    def search_body(state):
        i, t, n_ge = state
        for b in range(bits_per_check):
            t, n_ge = bit_step(i + b, t, n_ge)
        return i + bits_per_check, t, n_ge

    _, thr, n_ge = lax.while_loop(search_cond, search_body, (jnp.int32(0), t0, n0))
    real = thr != jnp.int32(INT_MIN)
    excess = jnp.max(jnp.where(jnp.logical_and(real, n_ge > n_sel), 1, 0)) > 0

    @pl.when(jnp.logical_not(excess))
    def _():
        thr_eff = jnp.maximum(thr, jnp.int32(INT_MIN + 1))
        write_mask(lambda key, pos: key >= thr_eff)

    @pl.when(excess)
    def _():
        need = n_sel - count(lambda key, pos: key > thr)

        def idx_step(i, j):
            cand = j | jnp.left_shift(jnp.int32(1), jnp.int32(idx_bits - 1) - i)
            cnt = count(lambda key, pos: jnp.logical_and(key == thr, pos < cand))
            return jnp.where(cnt < need, cand, j)

        last = lax.fori_loop(0, idx_bits, idx_step, jnp.zeros((tq, LANES), jnp.int32))
        write_mask(lambda key, pos: jnp.logical_or(
            key > thr, jnp.logical_and(jnp.logical_and(key == thr, pos <= last), real)))


def _indexer(iq, ik, iw, n_sel):
    bsz, seq = ik.shape[0], ik.shape[1]
    n_chunks = seq // DSA_KC
    nq = seq // DSA_TQ
    per_step = IDX_TQ // DSA_TQ
    assert seq % IDX_TQ == 0 and IDX_TQ % DSA_TQ == 0
    iq_arr, iq_blk = iq
    return pl.pallas_call(
        functools.partial(_indexer_kernel, n_sel=n_sel, n_chunks=n_chunks),
        grid=(bsz, seq // IDX_TQ),
        in_specs=[pl.BlockSpec((1, IDX_TQ, IDX_HEADS * HEAD), lambda b, q: (b, q, iq_blk)),
                  pl.BlockSpec((1, seq, HEAD), lambda b, q: (b, 0, 0)),
                  pl.BlockSpec((1, IDX_TQ, HEAD), lambda b, q: (b, q, 0))],
        out_specs=pl.BlockSpec((1, per_step, n_chunks, DSA_TQ, DSA_KC), lambda b, q: (b, q, 0, 0, 0)),
        out_shape=jax.ShapeDtypeStruct((bsz, nq, n_chunks, DSA_TQ, DSA_KC), BF16),
        scratch_shapes=[pltpu.VMEM((n_chunks, IDX_TQ, DSA_KC), jnp.int32)],
        compiler_params=_params("parallel", "arbitrary"),
        name="dsa_indexer",
    )(iq_arr, ik, iw)


def _lane_blocks(x):
    return [x[:, j * LANES:(j + 1) * LANES] for j in range(x.shape[1] // LANES)]


STREAMS = 4


def _two_sweep_attention(nch, rows, logits_fn, pv_fn, s_ref, mx_ref, ls_ref, acc_ref):
    mx_ref[...] = jnp.full(mx_ref.shape, -jnp.inf, F32)

    def sweep1_chunk(c, last):
        for i, logits2 in enumerate(logits_fn(c, last)):
            sl = slice(i * rows, (i + 1) * rows)
            s_ref[c, sl, :] = logits2
            mx = mx_ref[sl, :]
            for blk in _lane_blocks(logits2):
                mx = jnp.maximum(mx, blk)
            mx_ref[sl, :] = mx

    def pairs_then_rest(n, chunk_fn):
        def run(first, count, width):
            def trip(t, _):
                for k in range(width):
                    chunk_fn(first + width * t + k)
                return 0
            lax.fori_loop(0, count, trip, 0)

        triples = n // 3
        rest = n - 3 * triples
        run(0, triples, 3)
        run(3 * triples, rest // 2, 2)
        run(3 * triples + 2 * (rest // 2), rest % 2, 1)

    pairs_then_rest(nch - 1, lambda c: sweep1_chunk(c, False))
    sweep1_chunk(nch - 1, True)

    row_max = jnp.max(mx_ref[...], axis=1, keepdims=True)
    mx_ref[...] = jnp.broadcast_to(row_max, mx_ref.shape)
    ls_ref[...] = jnp.zeros(ls_ref.shape, F32)
    acc_ref[...] = jnp.zeros(acc_ref.shape, F32)

    def sweep2_chunk(c):
        for i in range(STREAMS):
            sl = slice(i * rows, (i + 1) * rows)
            m = mx_ref[sl, :]
            ps = [jnp.exp2(blk - m) for blk in _lane_blocks(s_ref[c, sl, :])]
            ls_ref[sl, :] += sum(ps[1:], ps[0])
            acc_ref[sl, :] += pv_fn(c, i, jnp.concatenate(ps, axis=1).astype(BF16))

    pairs_then_rest(nch, sweep2_chunk)
    return acc_ref[...] / jnp.sum(ls_ref[...], axis=1, keepdims=True)


def _softmax_scratch(rows, n_chunks, kc):
    return [pltpu.VMEM((n_chunks, rows, kc), F32),
            pltpu.VMEM((rows, LANES), F32),
            pltpu.VMEM((rows, LANES), F32),
            pltpu.VMEM((rows, HEAD), F32)]


def _dsa_attn_kernel(q_ref, k_ref, v_ref, mask_ref, o_ref, s_ref, mx_ref, ls_ref, acc_ref, *, groups):
    qb = pl.program_id(2)
    tq, kc = DSA_TQ, DSA_KC
    nch = (qb * tq) // kc + 1
    rows = groups * tq
    qs = [jnp.concatenate([q_ref[0, :, (i * groups + r) * HEAD:(i * groups + r + 1) * HEAD] for r in range(groups)],
                          axis=0) for i in range(STREAMS)]

    def logits_fn(c, last):
        start = pl.multiple_of(c * kc, kc)
        sel = (mask_ref[0, 0, c].astype(F32) > 0.5)[None]
        out = []
        for i in range(STREAMS):
            k_c = k_ref[0, pl.ds(start, kc), i * HEAD:(i + 1) * HEAD]
            raw = lax.dot_general(qs[i], k_c, (((1,), (1,)), ((), ())), preferred_element_type=F32)
            out.append(jnp.where(sel, raw.reshape(groups, tq, kc), NEG_INF).reshape(rows, kc))
        return out

    def pv_fn(c, i, p):
        v_c = v_ref[0, pl.ds(pl.multiple_of(c * kc, kc), kc), i * HEAD:(i + 1) * HEAD]
        return jnp.dot(p, v_c, preferred_element_type=F32)

    out = _two_sweep_attention(nch, rows, logits_fn, pv_fn, s_ref, mx_ref, ls_ref, acc_ref)
    for r in range(STREAMS * groups):
        o_ref[0, :, r * HEAD:(r + 1) * HEAD] = out[r * tq:(r + 1) * tq].astype(o_ref.dtype)


def _dsa_attention(q, k, v, mask, bsz, seq):
    groups = A_HEADS // A_KV_HEADS
    (q_arr, q0), (k_arr, k0), (v_arr, v0) = q, k, v
    nq = seq // DSA_TQ
    n_chunks = seq // DSA_KC
    qw = STREAMS * groups * HEAD
    kw = STREAMS * HEAD
    assert q0 % (STREAMS * groups) == 0 and k0 % STREAMS == 0 and v0 % STREAMS == 0
    return pl.pallas_call(
        functools.partial(_dsa_attn_kernel, groups=groups),
        grid=(bsz, A_KV_HEADS // STREAMS, nq),
        in_specs=[pl.BlockSpec((1, DSA_TQ, qw), lambda b, g, i: (b, i, q0 // (STREAMS * groups) + g)),
                  pl.BlockSpec((1, seq, kw), lambda b, g, i: (b, 0, k0 // STREAMS + g), pipeline_mode=pl.Buffered(1)),
                  pl.BlockSpec((1, seq, kw), lambda b, g, i: (b, 0, v0 // STREAMS + g), pipeline_mode=pl.Buffered(1)),
                  pl.BlockSpec((1, 1, n_chunks, DSA_TQ, DSA_KC), lambda b, g, i: (b, i, 0, 0, 0))],
        out_specs=pl.BlockSpec((1, DSA_TQ, qw), lambda b, g, i: (b, i, g)),
        out_shape=jax.ShapeDtypeStruct((bsz, seq, A_HEADS * HEAD), BF16),
        scratch_shapes=_softmax_scratch(STREAMS * groups * DSA_TQ, n_chunks, DSA_KC),
        compiler_params=_params("parallel", "parallel", "arbitrary"),
        name="dsa_attention",
    )(q_arr, k_arr, v_arr, mask)


def _fox_cum_kernel(f_ref, bias_ref, cum_t_ref):
    z = f_ref[0] + bias_ref[...]
    x = jnp.minimum(z, 0.0) - jnp.log1p(jnp.exp(-jnp.abs(z)))
    seq = x.shape[0]
    row = lax.broadcasted_iota(jnp.int32, (seq, 1), 0)
    d = 1
    while d < seq:
        x = x + jnp.where(row >= d, pltpu.roll(x, d, axis=0), 0.0)
        d *= 2
    cum_t_ref[0] = (x * jnp.float32(LOG2E)).T


def _fox_cum(f, bias):
    bsz, seq, _ = f.shape
    return pl.pallas_call(
        _fox_cum_kernel,
        grid=(bsz,),
        in_specs=[pl.BlockSpec((1, seq, HEAD), lambda b: (b, 0, 0)), pl.BlockSpec((1, HEAD), lambda b: (0, 0))],
        out_specs=pl.BlockSpec((1, HEAD, seq), lambda b: (b, 0, 0)),
        out_shape=jax.ShapeDtypeStruct((bsz, HEAD, seq), F32),
        compiler_params=_params("parallel"),
        name="fox_cum",
    )(f, bias)


FOX_TQ = 512
FOX_KC = 512


def _fox_attn_kernel(q_ref, k_ref, v_ref, cq_ref, ck_ref, o_ref, s_ref, mx_ref, ls_ref, acc_ref, *, scale2):
    qb = pl.program_id(2)
    tq, kc = FOX_TQ, FOX_KC
    heads = [slice(i * HEAD, (i + 1) * HEAD) for i in range(STREAMS)]
    qs = [(q_ref[0, :, h].astype(F32) * jnp.float32(scale2)).astype(q_ref.dtype) for h in heads]
    cqs = [jnp.broadcast_to(cq_ref[0, i, 0], (LANES, tq)).T for i in range(STREAMS)]

    def logits_fn(c, last):
        start = pl.multiple_of(c * kc, kc)
        out = []
        for i in range(STREAMS):
            raw = lax.dot_general(qs[i], k_ref[0, pl.ds(start, kc), heads[i]], (((1,), (1,)), ((), ())),
                                  preferred_element_type=F32)
            ck = ck_ref[0, i, c]
            blocks = [blk + cqs[i] - ck[:, j * LANES:(j + 1) * LANES] for j, blk in enumerate(_lane_blocks(raw))]
            logits2 = jnp.concatenate(blocks, axis=1)
            if last:
                row = lax.broadcasted_iota(jnp.int32, (tq, 1), 0)
                lane = lax.broadcasted_iota(jnp.int32, (1, kc), 1)
                logits2 = jnp.where(lane <= row, logits2, NEG_INF)
            out.append(logits2)
        return out

    def pv_fn(c, i, p):
        v_c = v_ref[0, pl.ds(pl.multiple_of(c * kc, kc), kc), heads[i]]
        return jnp.dot(p, v_c, preferred_element_type=F32)

    out = _two_sweep_attention(qb + 1, tq, logits_fn, pv_fn, s_ref, mx_ref, ls_ref, acc_ref)
    for i in range(STREAMS):
        o_ref[0, :, heads[i]] = out[i * tq:(i + 1) * tq].astype(o_ref.dtype)


def _fox_attention(p, cum_rows, bsz, seq):
    tq = FOX_TQ
    assert FOX_TQ == FOX_KC and seq % tq == 0
    n_chunks = seq // FOX_KC
    hw = STREAMS * HEAD
    cq_rows = cum_rows.reshape(bsz, B_HEADS, 1, 1, seq)
    return pl.pallas_call(
        functools.partial(_fox_attn_kernel, scale2=HEAD ** -0.5 * LOG2E),
        grid=(bsz, B_HEADS // STREAMS, seq // tq),
        in_specs=[pl.BlockSpec((1, tq, hw), lambda b, h, i: (b, i, h)),
                  pl.BlockSpec((1, seq, hw), lambda b, h, i: (b, 0, B_HEADS // STREAMS + h),
                               pipeline_mode=pl.Buffered(1)),
                  pl.BlockSpec((1, seq, hw), lambda b, h, i: (b, 0, 2 * B_HEADS // STREAMS + h),
                               pipeline_mode=pl.Buffered(1)),
                  pl.BlockSpec((1, STREAMS, 1, 1, tq), lambda b, h, i: (b, h, 0, 0, i)),
                  pl.BlockSpec((1, STREAMS, n_chunks, 1, FOX_KC), lambda b, h, i: (b, h, 0, 0, 0))],
        out_specs=pl.BlockSpec((1, tq, hw), lambda b, h, i: (b, i, h)),
        out_shape=jax.ShapeDtypeStruct((bsz, seq, B_HEADS * HEAD), BF16),
        scratch_shapes=_softmax_scratch(STREAMS * tq, n_chunks, FOX_KC),
        compiler_params=_params("parallel", "parallel", "arbitrary"),
        name="fox_attention",
    )(p, p, p, cq_rows, cum_rows)


def _swa_kernel(sinks_ref, q_ref, kp_ref, kc_ref, vp_ref, vc_ref, o_ref):
    n = pl.program_id(1)
    pair = pl.program_id(2)
    w, dh = WINDOW, C_HEAD_DIM
    groups = C_HEADS // C_KV_HEADS
    kv_per_blk = LANES // dh
    i = lax.broadcasted_iota(jnp.int32, (w, 1), 0)
    j = lax.broadcasted_iota(jnp.int32, (1, 2 * w), 1)
    valid = jnp.logical_and(jnp.logical_and(j > i, j <= i + w), n * w + j >= w)
    kk = jnp.concatenate([kp_ref[0], kc_ref[0]], axis=0)
    vv = jnp.concatenate([vp_ref[0], vc_ref[0]], axis=0)
    for g in range(kv_per_blk):
        k_g = kk[:, g * dh:(g + 1) * dh]
        v_g = vv[:, g * dh:(g + 1) * dh]
        for r in range(groups):
            col = (g * groups + r) * dh
            q = q_ref[0, :, col:col + dh]
            s = lax.dot_general(q, k_g, (((1,), (1,)), ((), ())), preferred_element_type=F32)
            logits = jnp.where(valid, s, NEG_INF)
            sink = sinks_ref[(pair * kv_per_blk + g) * groups + r]
            m = jnp.maximum(jnp.max(logits, axis=1, keepdims=True), sink)
            p = jnp.exp(logits - m)
            denom = jnp.sum(p, axis=1, keepdims=True) + jnp.exp(sink - m)
            o = jnp.dot((p / denom).astype(v_g.dtype), v_g, preferred_element_type=F32)
            o_ref[0, :, col:col + dh] = o.astype(o_ref.dtype)


def _swa_attention(pr, p, sinks, bsz, seq):
    w = WINDOW
    nb = seq // w
    kv_per_blk = LANES // C_HEAD_DIM
    pairs = C_KV_HEADS // kv_per_blk
    qw = kv_per_blk * (C_HEADS // C_KV_HEADS) * C_HEAD_DIM
    k0 = C_HEADS * C_HEAD_DIM // LANES
    v0 = k0 + C_KV_HEADS * C_HEAD_DIM // LANES
    prev = lambda col0: (lambda b, n, g: (b, jnp.maximum(n - 1, 0), col0 + g))
    cur = lambda col0: (lambda b, n, g: (b, n, col0 + g))
    blk = (1, w, LANES)
    return pl.pallas_call(
        _swa_kernel,
        grid=(bsz, nb, pairs),
        in_specs=[pl.BlockSpec(memory_space=pltpu.SMEM),
                  pl.BlockSpec((1, w, qw), lambda b, n, g: (b, n, g)),
                  pl.BlockSpec(blk, prev(k0)), pl.BlockSpec(blk, cur(k0)),
                  pl.BlockSpec(blk, prev(v0)), pl.BlockSpec(blk, cur(v0))],
        out_specs=pl.BlockSpec((1, w, qw), lambda b, n, g: (b, n, g)),
        out_shape=jax.ShapeDtypeStruct((bsz, seq, C_HEADS * C_HEAD_DIM), BF16),
        compiler_params=_params("parallel", "parallel", "parallel"),
        name="swa_attention",
    )(sinks, pr, pr, pr, p, p)


FFN_SUB_BLOCKS = 4
FFN_CAST_SLICES = 2


def _ffn_up_kernel(h_ref, wg_ref, wv_ref, cwg_ref, cwv_ref, cbg_ref, cbv_ref, o_ref, w_scr, u_scr, carry,
                   *, tiles_per_seq):
    mi = pl.program_id(0)
    ni = pl.program_id(1)
    tm = h_ref.shape[0]
    tn = wg_ref.shape[1]
    kdim = h_ref.shape[1]

    def cast_weights(k0, k1):
        w_scr[k0:k1, :tn] = wg_ref[k0:k1, :].astype(w_scr.dtype)
        w_scr[k0:k1, tn:] = wv_ref[k0:k1, :].astype(w_scr.dtype)

    @pl.when((mi % tiles_per_seq) == 0)
    def _():
        carry[ni] = jnp.zeros(carry.shape[1:], F32)

    u_scr[:SUBLANES, :] = carry[ni]
    sub = tm // FFN_SUB_BLOCKS

    def matmul(s, k0=0, k1=None, accumulate=False):
        k1 = kdim if k1 is None else k1
        rows = slice(s * sub, (s + 1) * sub)
        dst = slice(SUBLANES + s * sub, SUBLANES + (s + 1) * sub)
        part = jnp.dot(h_ref[rows, k0:k1], w_scr[k0:k1, :], preferred_element_type=F32)
        u_scr[dst, :] = u_scr[dst, :] + part if accumulate else part

    def conv(s, cols, cw_ref, cb_ref):
        window = u_scr[s * sub:s * sub + SUBLANES + sub, cols]
        acc = window * cw_ref[0:1, :]
        for tap in range(1, CONV_WIDTH):
            acc = pltpu.roll(acc, 1, axis=0) + window * cw_ref[tap:tap + 1, :]
        return cb_ref[...] + acc[SUBLANES:]

    def epilogue(s):
        gate = conv(s, slice(0, tn), cwg_ref, cbg_ref)
        val = conv(s, slice(tn, 2 * tn), cwv_ref, cbv_ref)
        half_gate = 0.5 * gate
        silu = half_gate * jnp.tanh(half_gate) + half_gate
        o_ref[s * sub:(s + 1) * sub, :] = (silu * val).astype(o_ref.dtype)

    kstep = kdim // FFN_CAST_SLICES
    for ks in range(FFN_CAST_SLICES):
        cast_weights(ks * kstep, (ks + 1) * kstep)
        matmul(0, ks * kstep, (ks + 1) * kstep, accumulate=ks > 0)
    for s in range(1, FFN_SUB_BLOCKS):
        matmul(s)
        epilogue(s - 1)
    epilogue(FFN_SUB_BLOCKS - 1)
    carry[ni] = u_scr[tm:, :]


def _ffn_up(h, w_up, conv_w, conv_b, layer, seq, *, tm=2048, tn=256):
    m, d = h.shape
    dff = w_up.shape[2] // 2
    tm = _tile(seq, tm, SUBLANES)
    tn = _tile(dff, tn)
    nt = dff // tn
    assert CONV_WIDTH - 1 <= SUBLANES and tm % (FFN_SUB_BLOCKS * SUBLANES) == 0
    conv_b = conv_b.reshape(conv_b.shape[0], 1, 2 * dff)
    return pl.pallas_call(
        functools.partial(_ffn_up_kernel, tiles_per_seq=seq // tm),
        grid=(m // tm, nt),
        in_specs=[pl.BlockSpec((tm, d), lambda i, j: (i, 0), pipeline_mode=pl.Buffered(1)),
                  pl.BlockSpec((None, d, tn), lambda i, j: (layer, 0, j)),
                  pl.BlockSpec((None, d, tn), lambda i, j: (layer, 0, j + nt)),
                  pl.BlockSpec((None, CONV_WIDTH, tn), lambda i, j: (layer, 0, j)),
                  pl.BlockSpec((None, CONV_WIDTH, tn), lambda i, j: (layer, 0, j + nt)),
                  pl.BlockSpec((None, 1, tn), lambda i, j: (layer, 0, j)),
                  pl.BlockSpec((None, 1, tn), lambda i, j: (layer, 0, j + nt))],
        out_specs=pl.BlockSpec((tm, tn), lambda i, j: (i, j)),
        out_shape=jax.ShapeDtypeStruct((m, dff), BF16),
        scratch_shapes=[pltpu.VMEM((d, 2 * tn), BF16),
                        pltpu.VMEM((SUBLANES + tm, 2 * tn), F32),
                        pltpu.VMEM((nt, SUBLANES, 2 * tn), F32)],
        compiler_params=_params("arbitrary", "arbitrary"),
        name="ffn_up",
    )(h, w_up, w_up, conv_w, conv_w, conv_b, conv_b)


OUT_PROJ_TN = 512


def _pad_rows(w, n):
    return jnp.pad(w, ((0, 0), (0, n - w.shape[1]), (0, 0)))


def _dsa_weight_col(j, *, tn):
    nq, niq, nkv2 = A_HEADS * HEAD // tn, IDX_HEADS * HEAD // tn, 2 * A_KV_HEADS * HEAD // tn
    return jnp.where(j < nq, j, jnp.where(j < nq + niq, j + nkv2, j - niq))


def _dsa_mixer(h, row_ss, x, w_in, w_small, ik_g, ik_b, w_out, slot, bsz, seq):
    a_q, a_kv, a_iq = A_HEADS * HEAD, A_KV_HEADS * HEAD, IDX_HEADS * HEAD
    tables, half = _rope_tables(seq, HEAD, HEAD)
    n_rope = a_q + a_iq + a_kv
    rope = (seq, _rope_operands(tables, half, HEAD), n_rope, a_q, HEAD ** -0.5 * LOG2E)
    p = _matmul(h, w_in, row_ss=row_ss, rope=rope, n=a_q + a_iq + 2 * a_kv, w_layer=slot, w_is_nk=True,
                col_block=_dsa_weight_col, out_dtype=BF16)
    p2 = _matmul(h, w_small, row_ss=row_ss, w_layer=slot, w_is_nk=True, out_dtype=F32)
    ik, iw = _idx_prep(p2, seq, ik_g, ik_b, tables, half)
    pr3 = p3 = p.reshape(bsz, seq, p.shape[1])
    n_sel = min(INDEX_TOPK, seq // 4)
    mask = _indexer((pr3, a_q // a_iq), ik.reshape(bsz, seq, HEAD), iw.reshape(bsz, seq, HEAD), n_sel)
    o = _dsa_attention((pr3, 0), (pr3, (a_q + a_iq) // HEAD), (p3, n_rope // HEAD), mask, bsz, seq)
    return _matmul(o.reshape(bsz * seq, a_q), w_out, w_layer=slot, res=x, out_dtype=F32, tn=OUT_PROJ_TN)


def _fox_mixer(h, row_ss, x, w_in, w_f, f_bias, w_out, slot, bsz, seq):
    d = h.shape[1]
    p = _matmul(h, w_in, row_ss=row_ss, n=3 * d, w_layer=slot, w_is_nk=True, out_dtype=BF16)
    f = _matmul(h, w_f, row_ss=row_ss, w_layer=slot, w_is_nk=True, out_dtype=F32, tn=HEAD)
    bias = jnp.pad(f_bias.astype(F32), (0, HEAD - B_HEADS)).reshape(1, HEAD)
    cum_t = _fox_cum(f.reshape(bsz, seq, HEAD), bias)
    cum_rows = cum_t[:, :B_HEADS].reshape(bsz, B_HEADS, seq // FOX_KC, 1, FOX_KC)
    o = _fox_attention(p.reshape(bsz, seq, 3 * d), cum_rows, bsz, seq)
    return _matmul(o.reshape(bsz * seq, d), w_out, w_layer=slot, res=x, out_dtype=F32, tn=OUT_PROJ_TN)


def _swa_mixer(h, row_ss, x, w_in, sinks, w_out, slot, bsz, seq):
    c_q, c_kv = C_HEADS * C_HEAD_DIM, C_KV_HEADS * C_HEAD_DIM
    tables, half = _rope_tables(seq, C_HEAD_DIM, C_HEAD_DIM)
    rope = (seq, _rope_operands(tables, half, C_HEAD_DIM), c_q + c_kv, c_q, C_HEAD_DIM ** -0.5)
    p = _matmul(h, w_in, row_ss=row_ss, rope=rope, w_layer=slot, out_dtype=BF16, tn=c_kv)
    p3 = p.reshape(bsz, seq, p.shape[1])
    o = _swa_attention(p3, p3, sinks.astype(F32), bsz, seq)
    return _matmul(o.reshape(bsz * seq, c_q), w_out, w_layer=slot, res=x, out_dtype=F32, tn=OUT_PROJ_TN)


def _conv_glu_ffn(h, x, w_up, conv_w, conv_b, w_down, layer, next_gain, seq):
    g = _ffn_up(h, w_up, conv_w, conv_b, layer, seq)
    return _matmul(g, w_down, w_layer=layer, res=x, gain=next_gain, out_dtype=F32, tm=512, tn=512)


def kernel(x, attn_norm, ffn_norm, final_norm, a_w_in, a_idx_k_norm_g, a_idx_k_norm_b, a_w_out,
           b_w_in, b_f_bias, b_w_out, c_w_in, c_sinks, c_w_out,
           ffn_w_up, ffn_conv_w, ffn_conv_b, ffn_w_down):
    bsz, seq, d = x.shape
    depth = attn_norm.shape[0]
    a_main = A_HEADS * HEAD + 2 * A_KV_HEADS * HEAD + IDX_HEADS * HEAD
    a_w_in, b_w_in = (jnp.swapaxes(w, 1, 2).astype(BF16) for w in (a_w_in, b_w_in))
    a_w_out, b_w_out, c_w_in, c_w_out, ffn_w_down = (
        w.astype(BF16) for w in (a_w_out, b_w_out, c_w_in, c_w_out, ffn_w_down))
    a_w_small = _pad_rows(a_w_in[:, a_main:, :], 2 * HEAD)
    b_w_f = _pad_rows(b_w_in[:, 3 * d:, :], HEAD)
    x = x.reshape(bsz * seq, d)
    h, row_ss = _rmsnorm(x, attn_norm[0], BF16), None
    for layer in range(depth):
        mixer, slot = layer % N_MIXERS, layer // N_MIXERS
        if mixer == 0:
            x = _dsa_mixer(h, row_ss, x, a_w_in, a_w_small, a_idx_k_norm_g[slot], a_idx_k_norm_b[slot], a_w_out,
                           slot, bsz, seq)
        elif mixer == 1:
            x = _fox_mixer(h, row_ss, x, b_w_in, b_w_f, b_f_bias[slot], b_w_out, slot, bsz, seq)
        else:
            x = _swa_mixer(h, row_ss, x, c_w_in, c_sinks[slot], c_w_out, slot, bsz, seq)
        h = _rmsnorm(x, ffn_norm[layer], BF16)
        if layer + 1 < depth:
            x, h, row_ss = _conv_glu_ffn(h, x, ffn_w_up, ffn_conv_w, ffn_conv_b, ffn_w_down, layer,
                                         attn_norm[layer + 1], seq)
        else:
            x = _conv_glu_ffn(h, x, ffn_w_up, ffn_conv_w, ffn_conv_b, ffn_w_down, layer, None, seq)
    return _rmsnorm(x, final_norm, F32).reshape(bsz, seq, d)
```
